```python
import math
import jax, jax.numpy as jnp
from jax import lax
import numpy as np

D_MODEL = 1024
BATCH = 2
SEQ = 8192
DEPTH = 1

D_MIX = D_MODEL
HG_HEADS = 4
HG_DK = 128
HG_DV = 128
HG_WIDTH = HG_HEADS * HG_DV
HG_CHUNK = 64
SG_GROUPS = 4
SG_CH = 128
SG_WIDTH = SG_GROUPS * SG_CH
SG_CHUNK = 128
IN_COLS = 3 * HG_HEADS * HG_DK + HG_WIDTH + 2 * SG_WIDTH
SPLIT_POINTS = (HG_HEADS * HG_DK, 2 * HG_HEADS * HG_DK, 2 * HG_HEADS * HG_DK + HG_WIDTH,
                2 * HG_HEADS * HG_DK + 2 * HG_WIDTH)
MOE_GROUPS = 4
MOE_EXPERTS_PER_GROUP = 8
MOE_EXPERTS = MOE_GROUPS * MOE_EXPERTS_PER_GROUP
MOE_TOP_K = 2
MOE_HIDDEN = 512
MOE_BLOCK = 128
ALPHA = (2.0 * DEPTH) ** 0.25
BETA = (8.0 * DEPTH) ** -0.25
LN_EPS = 1e-5
RMS_EPS = 1e-6

kernel_name = "hymba_hgrn2_gmlp_hiermoe_deepnorm_adaln"


def _layernorm(x, g, b):
    xf = x.astype(jnp.float32)
    mu = jnp.mean(xf, axis=-1, keepdims=True)
    var = jnp.mean(jnp.square(xf - mu), axis=-1, keepdims=True)
    y = (xf - mu) * lax.rsqrt(var + LN_EPS) * g.astype(jnp.float32) + b.astype(jnp.float32)
    return y.astype(x.dtype)


def _modulate(x, shift, scale):
    return x * (1.0 + scale[:, None, :]) + shift[:, None, :]


def _to_chunks(t, c):
    b, s, h, d = t.shape
    return t.reshape(b, s // c, c, h, d).transpose(1, 0, 3, 2, 4)


def _hgrn2_mixer(q, fz, iv, og, lb, gn_w):
    b, s, _ = q.shape
    dt = q.dtype
    q = jax.nn.silu(q.astype(jnp.float32)).reshape(b, s, HG_HEADS, HG_DK)
    lbh = lb.reshape(HG_HEADS, HG_DK)
    f = lbh + (1.0 - lbh) * jax.nn.sigmoid(fz.astype(jnp.float32).reshape(b, s, HG_HEADS, HG_DK))
    logf = jnp.log(f)
    k = 1.0 - f
    v = iv.astype(jnp.float32).reshape(b, s, HG_HEADS, HG_DV)
    qc, kc, vc, lfc = (_to_chunks(t, HG_CHUNK) for t in (q, k, v, logf))
    causal = jnp.tril(jnp.ones((HG_CHUNK, HG_CHUNK), dtype=bool))

    def step(state, inp):
        qq, kk, vv, lf = inp
        cum = jnp.cumsum(lf, axis=2)
        diff = cum[:, :, :, None, :] - cum[:, :, None, :, :]
        decay = jnp.exp(jnp.where(causal[None, None, :, :, None], diff, -jnp.inf))
        scores = jnp.einsum('bhtd,bhsd,bhtsd->bhts', qq, kk, decay)
        o_intra = jnp.einsum('bhts,bhsv->bhtv', scores, vv)
        o_inter = jnp.einsum('bhtd,bhdv->bhtv', qq * jnp.exp(cum), state)
        last = cum[:, :, -1:, :]
        new_state = jnp.exp(last[:, :, 0, :])[..., None] * state + \
            jnp.einsum('bhsd,bhsv->bhdv', kk * jnp.exp(last - cum), vv)
        return new_state, o_intra + o_inter

    s0 = jnp.zeros((b, HG_HEADS, HG_DK, HG_DV), jnp.float32)
    _, oc = lax.scan(step, s0, (qc, kc, vc, lfc))
    o = oc.transpose(1, 0, 3, 2, 4).reshape(b, s, HG_HEADS, HG_DV)
    o = o * lax.rsqrt(jnp.mean(jnp.square(o), axis=-1, keepdims=True) + RMS_EPS) * gn_w.astype(jnp.float32)
    o = o * jax.nn.silu(og.astype(jnp.float32).reshape(b, s, HG_HEADS, HG_DV))
    return o.reshape(b, s, HG_WIDTH).astype(dt)


def _spatial_gating_mixer(z, ln_g, ln_b, w_s, b_s):
    b, s, _ = z.shape
    z = jax.nn.gelu(z, approximate=False)
    u, v = z[..., :SG_WIDTH], z[..., SG_WIDTH:]
    v = _layernorm(v, ln_g, ln_b)
    v = v.reshape(b, s // SG_CHUNK, SG_CHUNK, SG_GROUPS, SG_CH)
    w_causal = w_s * jnp.tril(jnp.ones((SG_CHUNK, SG_CHUNK), dtype=w_s.dtype))
    mixed = jnp.einsum('gts,bnsgc->bntgc', w_causal, v) + b_s.T[None, None, :, :, None]
    return u * mixed.reshape(b, s, SG_WIDTH)


def _hier_moe(h, rg_w, rg_b, re_w, re_b, w_up, w_down):
    b, s, d = h.shape
    t = b * s
    xf = h.reshape(t, d)
    x32 = xf.astype(jnp.float32)
    g_logits = x32 @ rg_w.astype(jnp.float32) + rg_b.astype(jnp.float32)
    g_prob = jax.nn.softmax(g_logits, axis=-1)
    p_group, g_idx = lax.top_k(g_prob, 1)
    e_logits_all = jnp.einsum('td,gde->tge', x32, re_w.astype(jnp.float32)) + re_b.astype(jnp.float32)
    e_logits = jnp.take_along_axis(e_logits_all, g_idx[:, :, None], axis=1)[:, 0]
    e_prob = jax.nn.softmax(e_logits, axis=-1)
    top_p, top_j = lax.top_k(e_prob, MOE_TOP_K)
    weights = p_group * top_p / jnp.sum(top_p, axis=-1, keepdims=True)
    expert = g_idx * MOE_EXPERTS_PER_GROUP + top_j

    n_assign = t * MOE_TOP_K
    e_flat = expert.reshape(n_assign)
    tok_flat = jnp.repeat(jnp.arange(t, dtype=jnp.int32), MOE_TOP_K)
    w_flat = weights.reshape(n_assign)
    order = jnp.argsort(e_flat)
    e_sorted, tok_sorted, w_sorted = e_flat[order], tok_flat[order], w_flat[order]
    counts = jax.ops.segment_sum(jnp.ones((n_assign,), jnp.int32), e_flat, num_segments=MOE_EXPERTS)
    start = jnp.cumsum(counts) - counts
    padded = (counts + MOE_BLOCK - 1) // MOE_BLOCK * MOE_BLOCK
    pend = jnp.cumsum(padded)
    pstart = pend - padded
    dest = pstart[e_sorted] + (jnp.arange(n_assign, dtype=jnp.int32) - start[e_sorted])
    n_blocks = -(-n_assign // MOE_BLOCK) + MOE_EXPERTS
    n_rows = n_blocks * MOE_BLOCK
    pad_tok = jnp.zeros((n_rows,), jnp.int32).at[dest].set(tok_sorted)
    pad_w = jnp.zeros((n_rows,), jnp.float32).at[dest].set(w_sorted)
    block_e = jnp.minimum(jnp.searchsorted(pend, jnp.arange(n_blocks, dtype=jnp.int32) * MOE_BLOCK,
                                           side='right'), MOE_EXPERTS - 1).astype(jnp.int32)
    xb = xf[pad_tok].reshape(n_blocks, MOE_BLOCK, d)

    def expert_block(args):
        xblk, e = args
        gu = xblk @ w_up[e]
        return (jax.nn.silu(gu[:, :MOE_HIDDEN]) * gu[:, MOE_HIDDEN:]) @ w_down[e]

    yb = lax.map(expert_block, (xb, block_e)).reshape(n_rows, d)
    y = (yb.astype(jnp.float32) * pad_w[:, None]).astype(h.dtype)
    out = jnp.zeros((t, d), h.dtype).at[pad_tok].add(y)
    return out.reshape(b, s, d)


def setup_inputs(seed: int = 0) -> dict:
    key = jax.random.key(seed)
    ks = jax.random.split(key, 24)
    f32 = jnp.float32
    L, D = DEPTH, D_MODEL

    def nrm(k, shape, scale):
        return jax.random.normal(k, shape, f32) * scale

    return {
        "x": nrm(ks[0], (BATCH, SEQ, D), 1.0),
        "c": nrm(ks[1], (BATCH, D), 1.0),
        "w_ada": nrm(ks[2], (L, D, 6 * D), 0.1 * D ** -0.5),
        "b_ada": nrm(ks[3], (L, 6 * D), 0.01),
        "w_in": nrm(ks[4], (L, D, IN_COLS), D ** -0.5),
        "lb_logits": nrm(ks[5], (L + 1, HG_HEADS * HG_DK), 1.0),
        "hg_norm_w": 1.0 + nrm(ks[6], (L, HG_DV), 0.02),
        "sg_ln_g": 1.0 + nrm(ks[7], (L, SG_WIDTH), 0.02),
        "sg_ln_b": nrm(ks[8], (L, SG_WIDTH), 0.02),
        "sg_w": nrm(ks[9], (L, SG_GROUPS, SG_CHUNK, SG_CHUNK), SG_CHUNK ** -0.5),
        "sg_b": 1.0 + nrm(ks[10], (L, SG_GROUPS, SG_CHUNK), 0.02),
        "w_out": nrm(ks[11], (L, D_MIX, D), BETA * D_MIX ** -0.5),
        "ln1_g": 1.0 + nrm(ks[12], (L, D), 0.02),
        "ln1_b": nrm(ks[13], (L, D), 0.02),
        "router_group_w": nrm(ks[14], (L, D, MOE_GROUPS), D ** -0.5),
        "router_group_b": nrm(ks[15], (L, MOE_GROUPS), 0.01),
        "router_expert_w": nrm(ks[16], (L, MOE_GROUPS, D, MOE_EXPERTS_PER_GROUP), D ** -0.5),
        "router_expert_b": nrm(ks[17], (L, MOE_GROUPS, MOE_EXPERTS_PER_GROUP), 0.01),
        "w_up": nrm(ks[18], (L, MOE_EXPERTS, D, 2 * MOE_HIDDEN), D ** -0.5),
        "w_down": nrm(ks[19], (L, MOE_EXPERTS, MOE_HIDDEN, D), BETA * MOE_HIDDEN ** -0.5),
        "ln2_g": 1.0 + nrm(ks[20], (L, D), 0.02),
        "ln2_b": nrm(ks[21], (L, D), 0.02),
    }


def reference(x, c, w_ada, b_ada, w_in, lb_logits, hg_norm_w, sg_ln_g, sg_ln_b, sg_w, sg_b,
              w_out, ln1_g, ln1_b, router_group_w, router_group_b, router_expert_w,
              router_expert_b, w_up, w_down, ln2_g, ln2_b):
    lower_bounds = jnp.cumsum(jax.nn.softmax(lb_logits.astype(jnp.float32), axis=0), axis=0)
    c_act = jax.nn.silu(c)
    for l in range(DEPTH):
        ada = c_act @ w_ada[l] + b_ada[l]
        sh1, sc1, g1, sh2, sc2, g2 = jnp.split(ada, 6, axis=-1)
        h = _modulate(x, sh1, sc1)
        proj = h @ w_in[l]
        q, fz, iv, og, z = jnp.split(proj, SPLIT_POINTS, axis=-1)
        y_a = _hgrn2_mixer(q, fz, iv, og, lower_bounds[l], hg_norm_w[l])
        y_b = _spatial_gating_mixer(z, sg_ln_g[l], sg_ln_b[l], sg_w[l], sg_b[l])
        y = jnp.concatenate([y_a, y_b], axis=-1) @ w_out[l]
        x = _layernorm(ALPHA * x + (1.0 + g1[:, None, :]) * y, ln1_g[l], ln1_b[l])
        h2 = _modulate(x, sh2, sc2)
        m = _hier_moe(h2, router_group_w[l], router_group_b[l], router_expert_w[l],
                      router_expert_b[l], w_up[l], w_down[l])
        x = _layernorm(ALPHA * x + (1.0 + g2[:, None, :]) * m, ln2_g[l], ln2_b[l])
    return x
```

```python
import functools

import jax
import jax.numpy as jnp
from jax import lax
from jax.experimental import pallas as pl
from jax.experimental.pallas import tpu as pltpu
from jax.experimental.pallas import tpu_sc as plsc

F32 = jnp.float32
BF16 = jnp.bfloat16
I32 = jnp.int32
U32 = jnp.uint32

HG_HEADS = 4
HG_DK = 128
HG_CHUNK = 64
HG_SUB = 16
SG_GROUPS = 4
SG_CH = 128
SG_CHUNK = 128
MOE_GROUPS = 4
MOE_EPG = 8
MOE_EXPERTS = MOE_GROUPS * MOE_EPG
MOE_HIDDEN = 512
DEPTH = 1
ALPHA = (2.0 * DEPTH) ** 0.25
LN_EPS = 1e-5
RMS_EPS = 1e-6

LANES = 128
SUBLANES = 8
SC_WINDOW = 128
SC_ROW_WORDS = 256

MIX_ROWS = 512
SORT_ROWS = 512
MOE_ROWS = 256
OUT_ROWS = 512
VMEM_LIMIT = 52 * 1024 * 1024


def _dot(a, b):
    return jnp.dot(a, b, preferred_element_type=F32)


def _dot_nt(a, b):
    return lax.dot_general(a, b, (((1,), (1,)), ((), ())), preferred_element_type=F32)


def _dot_tn(a, b):
    return lax.dot_general(a, b, (((0,), (0,)), ((), ())), preferred_element_type=F32)


def _sigmoid(x):
    return jax.nn.sigmoid(x)


def _gelu_exact(x):
    return 0.5 * x * (1.0 + lax.erf(x * (2.0 ** -0.5)))


def _layernorm(x, g, b):
    mu = jnp.mean(x, axis=-1, keepdims=True)
    xc = x - mu
    var = jnp.mean(xc * xc, axis=-1, keepdims=True)
    return xc * lax.rsqrt(var + LN_EPS) * g + b


def _split_bf16(x):
    hi = x.astype(BF16)
    lo = (x - hi.astype(F32)).astype(BF16)
    return hi, lo


def _ada_kernel(c_ref, w_ref, b_ref, o_ref):
    c = c_ref[...]
    ca = c * _sigmoid(c)
    o_ref[...] = jnp.dot(ca, w_ref[...], preferred_element_type=F32,
                         precision=lax.Precision.HIGHEST) + b_ref[...]


def _ada_call(c, w_ada, b_ada):
    bsz, d = c.shape
    n = w_ada.shape[1]
    rows = -(-bsz // SUBLANES) * SUBLANES
    c_pad = jnp.pad(c, ((0, rows - bsz), (0, 0)))
    out = pl.pallas_call(
        _ada_kernel,
        grid=(n // d,),
        in_specs=[pl.BlockSpec((rows, d), lambda i: (0, 0)),
                  pl.BlockSpec((d, d), lambda i: (0, i)),
                  pl.BlockSpec((1, d), lambda i: (0, i))],
        out_specs=pl.BlockSpec((rows, d), lambda i: (0, i)),
        out_shape=jax.ShapeDtypeStruct((rows, n), F32),
        name="ada",
    )(c_pad, w_ada, b_ada.reshape(1, n))
    return out[:bsz].reshape(bsz, n // d, d)


def _hgrn2_chunk(c, proj_ref, y_ref, st_ref, g_ref, k_ref, lb, gnw, tril):
    hw = HG_HEADS * HG_DK
    cl = HG_CHUNK
    r0 = pl.multiple_of(c * cl, cl)
    rows = pl.ds(r0, cl)
    qz = proj_ref[rows, 0:hw]
    fz = proj_ref[rows, hw:2 * hw]
    v = proj_ref[rows, 2 * hw:3 * hw]
    og = proj_ref[rows, 3 * hw:4 * hw]

    q = qz * _sigmoid(qz)
    f = lb + (1.0 - lb) * _sigmoid(fz)
    lf = jnp.log(f)
    k = 1.0 - f
    lf_hi, lf_lo = _split_bf16(lf)
    g = _dot(tril, lf_hi) + _dot(tril, lf_lo)
    glast = g[cl - 1:cl, :]
    qg = (q * jnp.exp(g)).astype(BF16)
    kd = (k * jnp.exp(glast - g)).astype(BF16)
    vb = v.astype(BF16)
    g_ref[...] = g
    k_ref[...] = k

    sub_row = lax.broadcasted_iota(I32, (cl, HG_DK), 0)
    lane_c = lax.broadcasted_iota(I32, (HG_SUB, cl), 1)
    half = HG_SUB // 2
    trow_full = lax.broadcasted_iota(I32, (HG_SUB, HG_DK), 0)
    trow_half = lax.broadcasted_iota(I32, (half, HG_DK), 0) + half
    lane_half = lax.broadcasted_iota(I32, (half, cl), 1)

    for hd in range(HG_HEADS):
        cs = slice(hd * HG_DK, (hd + 1) * HG_DK)
        st = st_ref[hd]
        o = _dot_nt(qg[:, cs], st.astype(BF16))
        gh = g[:, cs]
        kh = k[:, cs]
        a_rows = []
        for i in range(cl // HG_SUB):
            b0 = i * HG_SUB
            gb = gh[b0:b0 + HG_SUB]
            qb = q[b0:b0 + HG_SUB, cs]
            if i > 0:
                ref = g_ref[b0 - 1:b0, cs]
                qt = (qb * jnp.exp(gb - ref)).astype(BF16)
                kt = (kh * jnp.exp(jnp.where(sub_row < b0, ref - gh, -jnp.inf))).astype(BF16)
                a = _dot_nt(qt, kt)
            else:
                a = jnp.zeros((HG_SUB, cl), F32)
            for s in range(HG_SUB):
                gs = g_ref[b0 + s:b0 + s + 1, cs]
                ks = k_ref[b0 + s:b0 + s + 1, cs]
                if s < half:
                    e = jnp.exp(jnp.where(trow_full >= s, gb - gs, -jnp.inf))
                    col = jnp.sum(qb * e * ks, axis=-1, keepdims=True)
                    a = jnp.where(lane_c == b0 + s, col, a)
                else:
                    e = jnp.exp(jnp.where(trow_half >= s, gb[half:] - gs, -jnp.inf))
                    col = jnp.sum(qb[half:] * e * ks, axis=-1, keepdims=True)
                    a = jnp.concatenate(
                        [a[:half], jnp.where(lane_half == b0 + s, col, a[half:])], axis=0)
            a_rows.append(a)
        amat = jnp.concatenate(a_rows, axis=0).astype(BF16)
        o = o + _dot(amat, vb[:, cs])
        st_ref[hd] = st * jnp.exp(glast[:, cs]) + _dot_tn(vb[:, cs], kd[:, cs])
        ms = jnp.mean(o * o, axis=-1, keepdims=True)
        ogh = og[:, cs]
        ya = o * lax.rsqrt(ms + RMS_EPS) * gnw * (ogh * _sigmoid(ogh))
        y_ref[rows, cs] = ya.astype(BF16)


def _route(logits):
    ts = logits.shape[0]
    lane = lax.broadcasted_iota(I32, (ts, LANES), 1)
    neg = -jnp.inf
    gl = jnp.where(lane < MOE_GROUPS, logits, neg)
    gmax = jnp.max(gl, axis=-1, keepdims=True)
    gidx = jnp.min(jnp.where(gl == gmax, lane, LANES), axis=-1, keepdims=True)
    p_group = 1.0 / jnp.sum(jnp.exp(gl - gmax), axis=-1, keepdims=True)
    e_lo = MOE_GROUPS + gidx * MOE_EPG
    el = jnp.where((lane >= e_lo) & (lane < e_lo + MOE_EPG), logits, neg)
    m1 = jnp.max(el, axis=-1, keepdims=True)
    i1 = jnp.min(jnp.where(el == m1, lane, LANES), axis=-1, keepdims=True)
    el2 = jnp.where(lane == i1, neg, el)
    m2 = jnp.max(el2, axis=-1, keepdims=True)
    i2 = jnp.min(jnp.where(el2 == m2, lane, LANES), axis=-1, keepdims=True)
    esum = jnp.sum(jnp.exp(el - m1), axis=-1, keepdims=True)
    p1 = 1.0 / esum
    p2 = jnp.exp(m2 - m1) / esum
    den = p1 + p2
    w1 = p_group * p1 / den
    w2 = p_group * p2 / den
    e1 = (i1 - MOE_GROUPS).astype(F32)
    e2 = (i2 - MOE_GROUPS).astype(F32)
    return jnp.where(lane == 0, e1, jnp.where(lane == 1, e2,
                     jnp.where(lane == 2, w1, jnp.where(lane == 3, w2, 0.0))))


def _mix_kernel(x_ref, ada_ref, win_ref, lbl_ref, gnw_ref, sglg_ref, sglb_ref, sgw_ref, sgbt_ref,
                wout_ref, ln1g_ref, ln1b_ref, wrh_ref, wrl_ref, br_ref,
                x1_ref, h2p_ref, route_ref,
                st_ref, proj_ref, y_ref, g_ref, k_ref):
    ts, d = x_ref.shape
    hw = HG_HEADS * HG_DK
    sgw = SG_GROUPS * SG_CH

    @pl.when(pl.program_id(1) == 0)
    def _():
        st_ref[...] = jnp.zeros_like(st_ref)

    x = x_ref[...]
    ada = ada_ref[...]
    sh1, sc1, g1, sh2, sc2 = (ada[i:i + 1] for i in range(5))
    h = (x * (1.0 + sc1) + sh1).astype(BF16)
    proj_ref[...] = _dot(h, win_ref[...])

    lbl = lbl_ref[...]
    slots = [lbl[i:i + 1] for i in range(lbl.shape[0])]
    mx = functools.reduce(jnp.maximum, slots)
    ex = [jnp.exp(s - mx) for s in slots]
    lb = ex[0] / functools.reduce(lambda a, b: a + b, ex)

    ci = lax.broadcasted_iota(I32, (HG_CHUNK, HG_CHUNK), 0)
    cj = lax.broadcasted_iota(I32, (HG_CHUNK, HG_CHUNK), 1)
    tril = (ci >= cj).astype(BF16)
    gnw = gnw_ref[...]

    def chunk_body(c, carry):
        _hgrn2_chunk(c, proj_ref, y_ref, st_ref, g_ref, k_ref, lb, gnw, tril)
        return carry

    lax.fori_loop(0, ts // HG_CHUNK, chunk_body, 0)

    pi = lax.broadcasted_iota(I32, (SG_CHUNK, SG_CHUNK), 0)
    pj = lax.broadcasted_iota(I32, (SG_CHUNK, SG_CHUNK), 1)
    sglg = sglg_ref[...]
    sglb = sglb_ref[...]
    for p in range(ts // SG_CHUNK):
        rows = slice(p * SG_CHUNK, (p + 1) * SG_CHUNK)
        z = _gelu_exact(proj_ref[rows, 4 * hw:4 * hw + 2 * sgw])
        u = z[:, :sgw]
        vn = _layernorm(z[:, sgw:], sglg, sglb).astype(BF16)
        for gi in range(SG_GROUPS):
            cs = slice(gi * SG_CH, (gi + 1) * SG_CH)
            wc = jnp.where(pi >= pj, sgw_ref[gi], 0.0).astype(BF16)
            mixed = _dot(wc, vn[:, cs]) + sgbt_ref[:, gi:gi + 1]
            y_ref[rows, hw + gi * SG_CH:hw + (gi + 1) * SG_CH] = (u[:, cs] * mixed).astype(BF16)

    y = _dot(y_ref[...], wout_ref[...])
    x1 = _layernorm(ALPHA * x + (1.0 + g1) * y, ln1g_ref[...], ln1b_ref[...])
    x1_ref[...] = x1
    h2 = x1 * (1.0 + sc2) + sh2

    h2_hi, h2_lo = _split_bf16(h2)
    bits = lax.bitcast_convert_type(h2_hi.astype(F32), U32)
    h2p_ref[...] = bits[:, d // 2:] | (bits[:, :d // 2] >> 16)

    wrh = wrh_ref[...]
    logits = _dot(h2_hi, wrh) + _dot(h2_lo, wrh) + _dot(h2_hi, wrl_ref[...]) + br_ref[...]
    route_ref[...] = _route(logits)


def _mix_call(x, ada, w_in_b, lb_logits, gnw, sglg, sglb, sg_w, sg_bt, w_out_b, ln1g, ln1b, wrh, wrl, br):
    bsz, s, d = x.shape
    ts = min(MIX_ROWS, s)
    ncol = w_in_b.shape[1]
    hw = HG_HEADS * HG_DK
    full = lambda a: pl.BlockSpec(a.shape, lambda b, j: (0,) * a.ndim)
    return pl.pallas_call(
        _mix_kernel,
        grid=(bsz, s // ts),
        in_specs=[pl.BlockSpec((None, ts, d), lambda b, j: (b, j, 0)),
                  pl.BlockSpec((None,) + ada.shape[1:], lambda b, j: (b, 0, 0)),
                  full(w_in_b), full(lb_logits), full(gnw), full(sglg), full(sglb), full(sg_w),
                  full(sg_bt), full(w_out_b), full(ln1g), full(ln1b), full(wrh), full(wrl), full(br)],
        out_specs=[pl.BlockSpec((None, ts, d), lambda b, j: (b, j, 0)),
                   pl.BlockSpec((None, ts, d // 2), lambda b, j: (b, j, 0)),
                   pl.BlockSpec((None, ts, LANES), lambda b, j: (b, j, 0))],
        out_shape=[jax.ShapeDtypeStruct((bsz, s, d), F32),
                   jax.ShapeDtypeStruct((bsz, s, d // 2), U32),
                   jax.ShapeDtypeStruct((bsz, s, LANES), F32)],
        scratch_shapes=[pltpu.VMEM((HG_HEADS, HG_DK, HG_DK), F32),
                        pltpu.VMEM((ts, ncol), F32),
                        pltpu.VMEM((ts, d), BF16),
                        pltpu.VMEM((HG_CHUNK, hw), F32),
                        pltpu.VMEM((HG_CHUNK, hw), F32)],
        compiler_params=pltpu.CompilerParams(
            dimension_semantics=("arbitrary", "arbitrary"), vmem_limit_bytes=VMEM_LIMIT),
        name="mix",
    )(x, ada, w_in_b, lb_logits, gnw, sglg, sglb, sg_w, sg_bt, w_out_b, ln1g, ln1b, wrh, wrl, br)


def _sort_kernel(route_ref, dest_ref, blk_ref, cnt_ref, pst_ref, carry_ref):
    ph = pl.program_id(0)
    i = pl.program_id(1)
    tk = route_ref.shape[0]
    nb = blk_ref.shape[0]
    lane = lax.broadcasted_iota(I32, (tk, LANES), 1)
    r = route_ref[...]
    oh1 = lane == r[:, 0:1].astype(I32)
    oh2 = lane == r[:, 1:2].astype(I32)
    hot = jnp.where(oh1 | oh2, 1.0, 0.0)
    colsum = jnp.sum(hot, axis=0, keepdims=True)

    @pl.when((ph == 0) & (i == 0))
    def _():
        cnt_ref[...] = jnp.zeros_like(cnt_ref)

    @pl.when(ph == 0)
    def _():
        cnt_ref[...] += colsum

    @pl.when((ph == 1) & (i == 0))
    def _():
        nblk = jnp.floor((cnt_ref[...] + (MOE_ROWS - 1)) * (1.0 / MOE_ROWS))
        ei = lax.broadcasted_iota(I32, (LANES, LANES), 0)
        ej = lax.broadcasted_iota(I32, (LANES, LANES), 1)
        upper = (ei < ej).astype(BF16)
        nblk8 = jnp.broadcast_to(nblk, (SUBLANES, LANES)).astype(BF16)
        pstart = _dot(nblk8, upper)[0:1]
        pend = pstart + nblk
        pst_ref[...] = pstart * float(MOE_ROWS)
        carry_ref[...] = jnp.zeros_like(carry_ref)
        jrow = lax.broadcasted_iota(I32, (nb, LANES), 0).astype(F32)
        lane_b = lax.broadcasted_iota(I32, (nb, LANES), 1)
        done = jnp.where((pend <= jrow) & (lane_b < MOE_EXPERTS), 1.0, 0.0)
        bexp = jnp.minimum(jnp.sum(done, axis=-1, keepdims=True), float(MOE_EXPERTS - 1))
        total = jnp.sum(jnp.where(lane_b == MOE_EXPERTS - 1, pend, 0.0), axis=-1, keepdims=True)
        blk_ref[...] = jnp.where(lane_b == 0, bexp, jnp.where(lane_b == 1, total, 0.0)).astype(I32)

    @pl.when(ph == 1)
    def _():
        ti = lax.broadcasted_iota(I32, (tk, tk), 0)
        tj = lax.broadcasted_iota(I32, (tk, tk), 1)
        before = (ti > tj).astype(BF16)
        base = _dot(before, hot.astype(BF16)) + carry_ref[...] + pst_ref[...]
        d1 = jnp.sum(jnp.where(oh1, base, 0.0), axis=-1, keepdims=True)
        d2 = jnp.sum(jnp.where(oh2, base, 0.0), axis=-1, keepdims=True)
        dest_ref[...] = jnp.where(lane == 0, d1, jnp.where(lane == 1, d2, 0.0)).astype(I32)
        carry_ref[...] += colsum


def _sort_call(route, n_blocks):
    t = route.shape[0]
    tk = min(SORT_ROWS, t)
    return pl.pallas_call(
        _sort_kernel,
        grid=(2, t // tk),
        in_specs=[pl.BlockSpec((tk, LANES), lambda p, i: (i, 0))],
        out_specs=[pl.BlockSpec((tk, LANES), lambda p, i: (i * p, 0)),
                   pl.BlockSpec((n_blocks, LANES), lambda p, i: (0, 0))],
        out_shape=[jax.ShapeDtypeStruct((t, LANES), I32),
                   jax.ShapeDtypeStruct((n_blocks, LANES), I32)],
        scratch_shapes=[pltpu.VMEM((1, LANES), F32), pltpu.VMEM((1, LANES), F32),
                        pltpu.VMEM((1, LANES), F32)],
        compiler_params=pltpu.CompilerParams(dimension_semantics=("arbitrary", "arbitrary")),
        name="sort",
    )(route)


def _sc_mesh():
    return plsc.VectorSubcoreMesh(core_axis_name="c", subcore_axis_name="s")


def _sc_scatter_rows2(rows, idx_a, idx_b, n_out):
    n, w = rows.shape

    @pl.kernel(out_type=jax.ShapeDtypeStruct((n_out, w), rows.dtype), mesh=_sc_mesh(), scratch_types=[])
    def scatter(x_hbm, ia_hbm, ib_hbm, o_hbm):
        def body(x_vmem, ia_vmem, ib_vmem):
            pltpu.sync_copy(x_vmem, o_hbm.at[ia_vmem.at[0]])
            pltpu.sync_copy(x_vmem, o_hbm.at[ib_vmem.at[0]])

        pltpu.emit_pipeline(
            body,
            grid=(n // SC_WINDOW,),
            in_specs=[pl.BlockSpec((SC_WINDOW, w), lambda i: (i, 0)),
                      pl.BlockSpec((1, SC_WINDOW), lambda i: (0, i)),
                      pl.BlockSpec((1, SC_WINDOW), lambda i: (0, i))],
            out_specs=[],
            core_axis_name=("c", "s"),
            dimension_semantics=(pltpu.PARALLEL,),
        )(x_hbm, ia_hbm, ib_hbm)

    return scatter(rows, idx_a.reshape(1, n), idx_b.reshape(1, n))


def _sc_gather_rows(src, idx):
    n = idx.shape[0]
    w = src.shape[1]

    @pl.kernel(out_type=jax.ShapeDtypeStruct((n, w), src.dtype), mesh=_sc_mesh(), scratch_types=[])
    def gather(x_hbm, i_hbm, o_hbm):
        def body(i_vmem, o_vmem):
            pltpu.sync_copy(x_hbm.at[i_vmem.at[0]], o_vmem)

        pltpu.emit_pipeline(
            body,
            grid=(n // SC_WINDOW,),
            in_specs=[pl.BlockSpec((1, SC_WINDOW), lambda i: (0, i))],
            out_specs=[pl.BlockSpec((SC_WINDOW, w), lambda i: (i, 0))],
            core_axis_name=("c", "s"),
            dimension_semantics=(pltpu.PARALLEL,),
        )(i_hbm, o_hbm)

    return gather(src, idx.reshape(1, n))


def _piece_index(row_idx, pieces):
    return (row_idx[:, None] * pieces + jnp.arange(pieces, dtype=I32)[None, :]).reshape(-1)


def _moe_kernel(be_ref, nused_ref, xb_ref, wup_ref, wdn_ref, y_ref, wupb_ref, wdnb_ref):
    j = pl.program_id(0)
    e = be_ref[j]
    e_prev = be_ref[jnp.maximum(j - 1, 0)]

    @pl.when((j == 0) | (e != e_prev))
    def _():
        wupb_ref[...] = wup_ref[...].astype(BF16)
        wdnb_ref[...] = wdn_ref[...].astype(BF16)

    @pl.when(j < nused_ref[0])
    def _():
        w = xb_ref[...]
        lo = lax.bitcast_convert_type(w << 16, F32)
        hi = lax.bitcast_convert_type(w & jnp.uint32(0xFFFF0000), F32)
        xrow = jnp.concatenate([lo, hi], axis=1).astype(BF16)
        gu = _dot(xrow, wupb_ref[...])
        gate = gu[:, :MOE_HIDDEN]
        act = (gate * _sigmoid(gate) * gu[:, MOE_HIDDEN:]).astype(BF16)
        y_ref[...] = _dot(act, wdnb_ref[...])


def _moe_call(block_expert, n_used, xb, w_up, w_down):
    n_rows, half = xb.shape
    d = 2 * half
    n_blocks = n_rows // MOE_ROWS
    hid2 = w_up.shape[2]
    hid = w_down.shape[1]
    last = lambda j, be, nu: (jnp.minimum(j, nu[0] - 1), 0)
    return pl.pallas_call(
        _moe_kernel,
        grid_spec=pltpu.PrefetchScalarGridSpec(
            num_scalar_prefetch=2,
            grid=(n_blocks,),
            in_specs=[pl.BlockSpec((MOE_ROWS, half), last),
                      pl.BlockSpec((None, d, hid2), lambda j, be, nu: (be[j], 0, 0)),
                      pl.BlockSpec((None, hid, d), lambda j, be, nu: (be[j], 0, 0))],
            out_specs=pl.BlockSpec((MOE_ROWS, d), last),
            scratch_shapes=[pltpu.VMEM((d, hid2), BF16), pltpu.VMEM((hid, d), BF16)]),
        out_shape=jax.ShapeDtypeStruct((n_rows, d), F32),
        compiler_params=pltpu.CompilerParams(
            dimension_semantics=("arbitrary",), vmem_limit_bytes=VMEM_LIMIT),
        name="moe",
    )(block_expert, n_used, xb, w_up, w_down)


def _combine_kernel(x1_ref, ya_ref, yb_ref, route_ref, ada_ref, g_ref, b_ref, o_ref):
    r = route_ref[...]
    m = ya_ref[...] * r[:, 2:3] + yb_ref[...] * r[:, 3:4]
    g2 = ada_ref[5:6]
    o_ref[...] = _layernorm(ALPHA * x1_ref[...] + (1.0 + g2) * m, g_ref[...], b_ref[...])


def _combine_call(x1, y2, route, ada, ln2g, ln2b):
    bsz, s, d = x1.shape
    ts = min(OUT_ROWS, s)
    nj = s // ts
    return pl.pallas_call(
        _combine_kernel,
        grid=(bsz, nj),
        in_specs=[pl.BlockSpec((None, ts, d), lambda b, j: (b, j, 0)),
                  pl.BlockSpec((None, ts, d), lambda b, j: (0, b * nj + j, 0)),
                  pl.BlockSpec((None, ts, d), lambda b, j: (1, b * nj + j, 0)),
                  pl.BlockSpec((None, ts, LANES), lambda b, j: (b, j, 0)),
                  pl.BlockSpec((None,) + ada.shape[1:], lambda b, j: (b, 0, 0)),
                  pl.BlockSpec((1, d), lambda b, j: (0, 0)),
                  pl.BlockSpec((1, d), lambda b, j: (0, 0))],
        out_specs=pl.BlockSpec((None, ts, d), lambda b, j: (b, j, 0)),
        out_shape=jax.ShapeDtypeStruct((bsz, s, d), F32),
        compiler_params=pltpu.CompilerParams(dimension_semantics=("arbitrary", "arbitrary")),
        name="combine",
    )(x1, y2, y2, route, ada, ln2g, ln2b)


def kernel(x, c, w_ada, b_ada, w_in, lb_logits, hg_norm_w, sg_ln_g, sg_ln_b, sg_w, sg_b, w_out, ln1_g, ln1_b, router_group_w, router_group_b, router_expert_w, router_expert_b, w_up, w_down, ln2_g, ln2_b):
    assert w_in.shape[0] == DEPTH
    bsz, s, d = x.shape
    t = bsz * s
    l = 0

    ada = _ada_call(c, w_ada[l], b_ada[l])

    wr = jnp.concatenate(
        [router_group_w[l], router_expert_w[l].transpose(1, 0, 2).reshape(d, MOE_EXPERTS)], axis=1)
    wr = jnp.pad(wr, ((0, 0), (0, LANES - wr.shape[1])))
    br = jnp.concatenate([router_group_b[l], router_expert_b[l].reshape(MOE_EXPERTS)])
    br = jnp.pad(br, (0, LANES - br.shape[0])).reshape(1, LANES)
    wrh, wrl = _split_bf16(wr)

    x1, h2p, route = _mix_call(
        x, ada, w_in[l].astype(BF16), lb_logits, hg_norm_w[l].reshape(1, -1),
        sg_ln_g[l].reshape(1, -1), sg_ln_b[l].reshape(1, -1), sg_w[l], sg_b[l].T,
        w_out[l].astype(BF16), ln1_g[l].reshape(1, d), ln1_b[l].reshape(1, d), wrh, wrl, br)

    n_blocks = -(-(2 * t) // MOE_ROWS) + MOE_EXPERTS
    n_rows = n_blocks * MOE_ROWS
    dest, blk = _sort_call(route.reshape(t, LANES), n_blocks)
    d0, d1 = dest[:, 0], dest[:, 1]

    half = d // 2
    pin = half // SC_ROW_WORDS
    xb = _sc_scatter_rows2(h2p.reshape(t * pin, SC_ROW_WORDS), _piece_index(d0, pin),
                           _piece_index(d1, pin), n_rows * pin).reshape(n_rows, half)

    yb = _moe_call(blk[:, 0], blk[0:1, 1].reshape(1), xb, w_up[l], w_down[l])

    pout = d // SC_ROW_WORDS
    y2 = _sc_gather_rows(yb.reshape(n_rows * pout, SC_ROW_WORDS),
                         _piece_index(jnp.concatenate([d0, d1]), pout)).reshape(2, t, d)

    return _combine_call(x1, y2, route, ada, ln2_g[l].reshape(1, d), ln2_b[l].reshape(1, d))
```

```python
import functools

import jax
import jax.numpy as jnp
from jax import lax
from jax.experimental import pallas as pl
from jax.experimental.pallas import tpu as pltpu
from jax.experimental.pallas import tpu_sc as plsc

F32 = jnp.float32
BF16 = jnp.bfloat16
I32 = jnp.int32
U32 = jnp.uint32

HG_HEADS = 4
HG_DK = 128
HG_CHUNK = 64
HG_SUB = 16
SG_GROUPS = 4
SG_CH = 128
SG_CHUNK = 128
MOE_GROUPS = 4
MOE_EPG = 8
MOE_EXPERTS = MOE_GROUPS * MOE_EPG
MOE_HIDDEN = 512
DEPTH = 1
ALPHA = (2.0 * DEPTH) ** 0.25
LN_EPS = 1e-5
RMS_EPS = 1e-6

LANES = 128
SUBLANES = 8
SC_WINDOW = 128
SC_ROW_WORDS = 256

MIX_ROWS = 512
SORT_ROWS = 512
MOE_ROWS = 256
OUT_ROWS = 512
VMEM_LIMIT = 52 * 1024 * 1024


def _dot(a, b):
    return jnp.dot(a, b, preferred_element_type=F32)


def _dot_nt(a, b):
    return lax.dot_general(a, b, (((1,), (1,)), ((), ())), preferred_element_type=F32)


def _dot_tn(a, b):
    return lax.dot_general(a, b, (((0,), (0,)), ((), ())), preferred_element_type=F32)


def _sigmoid(x):
    return jax.nn.sigmoid(x)


def _gelu_exact(x):
    return 0.5 * x * (1.0 + lax.erf(x * (2.0 ** -0.5)))


def _layernorm(x, g, b):
    mu = jnp.mean(x, axis=-1, keepdims=True)
    xc = x - mu
    var = jnp.mean(xc * xc, axis=-1, keepdims=True)
    return xc * lax.rsqrt(var + LN_EPS) * g + b


def _split_bf16(x):
    hi = x.astype(BF16)
    lo = (x - hi.astype(F32)).astype(BF16)
    return hi, lo


def _ada_kernel(c_ref, w_ref, b_ref, o_ref):
    c = c_ref[...]
    ca = c * _sigmoid(c)
    o_ref[...] = jnp.dot(ca, w_ref[...], preferred_element_type=F32,
                         precision=lax.Precision.HIGHEST) + b_ref[...]


def _ada_call(c, w_ada, b_ada):
    bsz, d = c.shape
    n = w_ada.shape[1]
    rows = -(-bsz // SUBLANES) * SUBLANES
    c_pad = jnp.pad(c, ((0, rows - bsz), (0, 0)))
    out = pl.pallas_call(
        _ada_kernel,
        grid=(n // d,),
        in_specs=[pl.BlockSpec((rows, d), lambda i: (0, 0)),
                  pl.BlockSpec((d, d), lambda i: (0, i)),
                  pl.BlockSpec((1, d), lambda i: (0, i))],
        out_specs=pl.BlockSpec((rows, d), lambda i: (0, i)),
        out_shape=jax.ShapeDtypeStruct((rows, n), F32),
        name="ada",
    )(c_pad, w_ada, b_ada.reshape(1, n))
    return out[:bsz].reshape(bsz, n // d, d)


def _hgrn2_chunk(c, proj_ref, y_ref, st_ref, g_ref, k_ref, lb, gnw, tril):
    hw = HG_HEADS * HG_DK
    cl = HG_CHUNK
    r0 = pl.multiple_of(c * cl, cl)
    rows = pl.ds(r0, cl)
    qz = proj_ref[rows, 0:hw]
    fz = proj_ref[rows, hw:2 * hw]
    v = proj_ref[rows, 2 * hw:3 * hw]
    og = proj_ref[rows, 3 * hw:4 * hw]

    q = qz * _sigmoid(qz)
    f = lb + (1.0 - lb) * _sigmoid(fz)
    lf = jnp.log(f)
    k = 1.0 - f
    lf_hi, lf_lo = _split_bf16(lf)
    g = _dot(tril, lf_hi) + _dot(tril, lf_lo)
    glast = g[cl - 1:cl, :]
    qg = (q * jnp.exp(g)).astype(BF16)
    kd = (k * jnp.exp(glast - g)).astype(BF16)
    vb = v.astype(BF16)
    g_ref[...] = g
    k_ref[...] = k

    sub_row = lax.broadcasted_iota(I32, (cl, HG_DK), 0)
    lane_c = lax.broadcasted_iota(I32, (HG_SUB, cl), 1)
    half = HG_SUB // 2
    trow_full = lax.broadcasted_iota(I32, (HG_SUB, HG_DK), 0)
    trow_half = lax.broadcasted_iota(I32, (half, HG_DK), 0) + half
    lane_half = lax.broadcasted_iota(I32, (half, cl), 1)

    for hd in range(HG_HEADS):
        cs = slice(hd * HG_DK, (hd + 1) * HG_DK)
        st = st_ref[hd]
        o = _dot_nt(qg[:, cs], st.astype(BF16))
        gh = g[:, cs]
        kh = k[:, cs]
        a_rows = []
        for i in range(cl // HG_SUB):
            b0 = i * HG_SUB
            gb = gh[b0:b0 + HG_SUB]
            qb = q[b0:b0 + HG_SUB, cs]
            if i > 0:
                ref = g_ref[b0 - 1:b0, cs]
                qt = (qb * jnp.exp(gb - ref)).astype(BF16)
                kt = (kh * jnp.exp(jnp.where(sub_row < b0, ref - gh, -jnp.inf))).astype(BF16)
                a = _dot_nt(qt, kt)
            else:
                a = jnp.zeros((HG_SUB, cl), F32)
            for s in range(HG_SUB):
                gs = g_ref[b0 + s:b0 + s + 1, cs]
                ks = k_ref[b0 + s:b0 + s + 1, cs]
                if s < half:
                    e = jnp.exp(jnp.where(trow_full >= s, gb - gs, -jnp.inf))
                    col = jnp.sum(qb * e * ks, axis=-1, keepdims=True)
                    a = jnp.where(lane_c == b0 + s, col, a)
                else:
                    e = jnp.exp(jnp.where(trow_half >= s, gb[half:] - gs, -jnp.inf))
                    col = jnp.sum(qb[half:] * e * ks, axis=-1, keepdims=True)
                    a = jnp.concatenate(
                        [a[:half], jnp.where(lane_half == b0 + s, col, a[half:])], axis=0)
            a_rows.append(a)
        amat = jnp.concatenate(a_rows, axis=0).astype(BF16)
        o = o + _dot(amat, vb[:, cs])
        st_ref[hd] = st * jnp.exp(glast[:, cs]) + _dot_tn(vb[:, cs], kd[:, cs])
        ms = jnp.mean(o * o, axis=-1, keepdims=True)
        ogh = og[:, cs]
        ya = o * lax.rsqrt(ms + RMS_EPS) * gnw * (ogh * _sigmoid(ogh))
        y_ref[rows, cs] = ya.astype(BF16)


def _route(logits):
    ts = logits.shape[0]
    lane = lax.broadcasted_iota(I32, (ts, LANES), 1)
    neg = -jnp.inf
    gl = jnp.where(lane < MOE_GROUPS, logits, neg)
    gmax = jnp.max(gl, axis=-1, keepdims=True)
    gidx = jnp.min(jnp.where(gl == gmax, lane, LANES), axis=-1, keepdims=True)
    p_group = 1.0 / jnp.sum(jnp.exp(gl - gmax), axis=-1, keepdims=True)
    e_lo = MOE_GROUPS + gidx * MOE_EPG
    el = jnp.where((lane >= e_lo) & (lane < e_lo + MOE_EPG), logits, neg)
    m1 = jnp.max(el, axis=-1, keepdims=True)
    i1 = jnp.min(jnp.where(el == m1, lane, LANES), axis=-1, keepdims=True)
    el2 = jnp.where(lane == i1, neg, el)
    m2 = jnp.max(el2, axis=-1, keepdims=True)
    i2 = jnp.min(jnp.where(el2 == m2, lane, LANES), axis=-1, keepdims=True)
    esum = jnp.sum(jnp.exp(el - m1), axis=-1, keepdims=True)
    p1 = 1.0 / esum
    p2 = jnp.exp(m2 - m1) / esum
    den = p1 + p2
    w1 = p_group * p1 / den
    w2 = p_group * p2 / den
    e1 = (i1 - MOE_GROUPS).astype(F32)
    e2 = (i2 - MOE_GROUPS).astype(F32)
    return jnp.where(lane == 0, e1, jnp.where(lane == 1, e2,
                     jnp.where(lane == 2, w1, jnp.where(lane == 3, w2, 0.0))))


def _mix_kernel(x_ref, ada_ref, win_ref, lbl_ref, gnw_ref, sglg_ref, sglb_ref, sgw_ref, sgbt_ref,
                wout_ref, ln1g_ref, ln1b_ref, wrh_ref, wrl_ref, br_ref,
                x1_ref, route_ref, *rest):
    h2p_refs = rest[:-5]
    st_ref, proj_ref, y_ref, g_ref, k_ref = rest[-5:]
    ts, d = x_ref.shape
    hw = HG_HEADS * HG_DK
    sgw = SG_GROUPS * SG_CH

    @pl.when(pl.program_id(1) == 0)
    def _():
        st_ref[...] = jnp.zeros_like(st_ref)

    x = x_ref[...]
    ada = ada_ref[...]
    sh1, sc1, g1, sh2, sc2 = (ada[i:i + 1] for i in range(5))
    h = (x * (1.0 + sc1) + sh1).astype(BF16)
    proj_ref[...] = _dot(h, win_ref[...])

    lbl = lbl_ref[...]
    slots = [lbl[i:i + 1] for i in range(lbl.shape[0])]
    mx = functools.reduce(jnp.maximum, slots)
    ex = [jnp.exp(s - mx) for s in slots]
    lb = ex[0] / functools.reduce(lambda a, b: a + b, ex)

    ci = lax.broadcasted_iota(I32, (HG_CHUNK, HG_CHUNK), 0)
    cj = lax.broadcasted_iota(I32, (HG_CHUNK, HG_CHUNK), 1)
    tril = (ci >= cj).astype(BF16)
    gnw = gnw_ref[...]

    def chunk_body(c, carry):
        _hgrn2_chunk(c, proj_ref, y_ref, st_ref, g_ref, k_ref, lb, gnw, tril)
        return carry

    lax.fori_loop(0, ts // HG_CHUNK, chunk_body, 0)

    pi = lax.broadcasted_iota(I32, (SG_CHUNK, SG_CHUNK), 0)
    pj = lax.broadcasted_iota(I32, (SG_CHUNK, SG_CHUNK), 1)
    sglg = sglg_ref[...]
    sglb = sglb_ref[...]
    for p in range(ts // SG_CHUNK):
        rows = slice(p * SG_CHUNK, (p + 1) * SG_CHUNK)
        z = _gelu_exact(proj_ref[rows, 4 * hw:4 * hw + 2 * sgw])
        u = z[:, :sgw]
        vn = _layernorm(z[:, sgw:], sglg, sglb).astype(BF16)
        for gi in range(SG_GROUPS):
            cs = slice(gi * SG_CH, (gi + 1) * SG_CH)
            wc = jnp.where(pi >= pj, sgw_ref[gi], 0.0).astype(BF16)
            mixed = _dot(wc, vn[:, cs]) + sgbt_ref[:, gi:gi + 1]
            y_ref[rows, hw + gi * SG_CH:hw + (gi + 1) * SG_CH] = (u[:, cs] * mixed).astype(BF16)

    y = _dot(y_ref[...], wout_ref[...])
    x1 = _layernorm(ALPHA * x + (1.0 + g1) * y, ln1g_ref[...], ln1b_ref[...])
    x1_ref[...] = x1
    h2 = x1 * (1.0 + sc2) + sh2

    h2_hi, h2_lo = _split_bf16(h2)
    bits = lax.bitcast_convert_type(h2_hi.astype(F32), U32)
    packed = bits[:, d // 2:] | (bits[:, :d // 2] >> 16)
    for i, ref in enumerate(h2p_refs):
        ref[...] = packed[:, i * SC_ROW_WORDS:(i + 1) * SC_ROW_WORDS]

    wrh = wrh_ref[...]
    logits = _dot(h2_hi, wrh) + _dot(h2_lo, wrh) + _dot(h2_hi, wrl_ref[...]) + br_ref[...]
    route_ref[...] = _route(logits)


def _mix_call(x, ada, w_in_b, lb_logits, gnw, sglg, sglb, sg_w, sg_bt, w_out_b, ln1g, ln1b, wrh, wrl, br):
    bsz, s, d = x.shape
    ts = min(MIX_ROWS, s)
    ncol = w_in_b.shape[1]
    hw = HG_HEADS * HG_DK
    n_pieces = d // 2 // SC_ROW_WORDS
    full = lambda a: pl.BlockSpec(a.shape, lambda b, j: (0,) * a.ndim)
    return pl.pallas_call(
        _mix_kernel,
        grid=(bsz, s // ts),
        in_specs=[pl.BlockSpec((None, ts, d), lambda b, j: (b, j, 0)),
                  pl.BlockSpec((None,) + ada.shape[1:], lambda b, j: (b, 0, 0)),
                  full(w_in_b), full(lb_logits), full(gnw), full(sglg), full(sglb), full(sg_w),
                  full(sg_bt), full(w_out_b), full(ln1g), full(ln1b), full(wrh), full(wrl), full(br)],
        out_specs=[pl.BlockSpec((None, ts, d), lambda b, j: (b, j, 0)),
                   pl.BlockSpec((None, ts, LANES), lambda b, j: (b, j, 0))]
                  + [pl.BlockSpec((None, ts, SC_ROW_WORDS), lambda b, j: (b, j, 0))] * n_pieces,
        out_shape=[jax.ShapeDtypeStruct((bsz, s, d), F32),
                   jax.ShapeDtypeStruct((bsz, s, LANES), F32)]
                  + [jax.ShapeDtypeStruct((bsz, s, SC_ROW_WORDS), U32)] * n_pieces,
        scratch_shapes=[pltpu.VMEM((HG_HEADS, HG_DK, HG_DK), F32),
                        pltpu.VMEM((ts, ncol), F32),
                        pltpu.VMEM((ts, d), BF16),
                        pltpu.VMEM((HG_CHUNK, hw), F32),
                        pltpu.VMEM((HG_CHUNK, hw), F32)],
        compiler_params=pltpu.CompilerParams(
            dimension_semantics=("arbitrary", "arbitrary"), vmem_limit_bytes=VMEM_LIMIT),
        name="mix",
    )(x, ada, w_in_b, lb_logits, gnw, sglg, sglb, sg_w, sg_bt, w_out_b, ln1g, ln1b, wrh, wrl, br)


def _sort_kernel(route_ref, dest_ref, blk_ref, cnt_ref, pst_ref, carry_ref):
    ph = pl.program_id(0)
    i = pl.program_id(1)
    tk = route_ref.shape[0]
    nb = blk_ref.shape[0]
    lane = lax.broadcasted_iota(I32, (tk, LANES), 1)
    r = route_ref[...]
    oh1 = lane == r[:, 0:1].astype(I32)
    oh2 = lane == r[:, 1:2].astype(I32)
    hot = jnp.where(oh1 | oh2, 1.0, 0.0)
    colsum = jnp.sum(hot, axis=0, keepdims=True)

    @pl.when((ph == 0) & (i == 0))
    def _():
        cnt_ref[...] = jnp.zeros_like(cnt_ref)

    @pl.when(ph == 0)
    def _():
        cnt_ref[...] += colsum

    @pl.when((ph == 1) & (i == 0))
    def _():
        nblk = jnp.floor((cnt_ref[...] + (MOE_ROWS - 1)) * (1.0 / MOE_ROWS))
        ei = lax.broadcasted_iota(I32, (LANES, LANES), 0)
        ej = lax.broadcasted_iota(I32, (LANES, LANES), 1)
        upper = (ei < ej).astype(BF16)
        nblk8 = jnp.broadcast_to(nblk, (SUBLANES, LANES)).astype(BF16)
        pstart = _dot(nblk8, upper)[0:1]
        pend = pstart + nblk
        pst_ref[...] = pstart * float(MOE_ROWS)
        carry_ref[...] = jnp.zeros_like(carry_ref)
        jrow = lax.broadcasted_iota(I32, (nb, LANES), 0).astype(F32)
        lane_b = lax.broadcasted_iota(I32, (nb, LANES), 1)
        done = jnp.where((pend <= jrow) & (lane_b < MOE_EXPERTS), 1.0, 0.0)
        bexp = jnp.minimum(jnp.sum(done, axis=-1, keepdims=True), float(MOE_EXPERTS - 1))
        total = jnp.sum(jnp.where(lane_b == MOE_EXPERTS - 1, pend, 0.0), axis=-1, keepdims=True)
        blk_ref[...] = jnp.where(lane_b == 0, bexp, jnp.where(lane_b == 1, total, 0.0)).astype(I32)

    @pl.when(ph == 1)
    def _():
        ti = lax.broadcasted_iota(I32, (tk, tk), 0)
        tj = lax.broadcasted_iota(I32, (tk, tk), 1)
        before = (ti > tj).astype(BF16)
        base = _dot(before, hot.astype(BF16)) + carry_ref[...] + pst_ref[...]
        d1 = jnp.sum(jnp.where(oh1, base, 0.0), axis=-1, keepdims=True)
        d2 = jnp.sum(jnp.where(oh2, base, 0.0), axis=-1, keepdims=True)
        dest_ref[...] = jnp.where(lane == 0, d1, jnp.where(lane == 1, d2, 0.0)).astype(I32)
        carry_ref[...] += colsum


def _sort_call(route, n_blocks):
    t = route.shape[0]
    tk = min(SORT_ROWS, t)
    return pl.pallas_call(
        _sort_kernel,
        grid=(2, t // tk),
        in_specs=[pl.BlockSpec((tk, LANES), lambda p, i: (i, 0))],
        out_specs=[pl.BlockSpec((tk, LANES), lambda p, i: (i * p, 0)),
                   pl.BlockSpec((n_blocks, LANES), lambda p, i: (0, 0))],
        out_shape=[jax.ShapeDtypeStruct((t, LANES), I32),
                   jax.ShapeDtypeStruct((n_blocks, LANES), I32)],
        scratch_shapes=[pltpu.VMEM((1, LANES), F32), pltpu.VMEM((1, LANES), F32),
                        pltpu.VMEM((1, LANES), F32)],
        compiler_params=pltpu.CompilerParams(dimension_semantics=("arbitrary", "arbitrary")),
        name="sort",
    )(route)


def _sc_mesh():
    return plsc.VectorSubcoreMesh(core_axis_name="c", subcore_axis_name="s")


def _sc_scatter_rows2(rows, idx_a, idx_b, n_out):
    n, w = rows.shape

    @pl.kernel(out_type=jax.ShapeDtypeStruct((n_out, w), rows.dtype), mesh=_sc_mesh(), scratch_types=[])
    def scatter(x_hbm, ia_hbm, ib_hbm, o_hbm):
        def body(x_vmem, ia_vmem, ib_vmem):
            pltpu.sync_copy(x_vmem, o_hbm.at[ia_vmem.at[0]])
            pltpu.sync_copy(x_vmem, o_hbm.at[ib_vmem.at[0]])

        pltpu.emit_pipeline(
            body,
            grid=(n // SC_WINDOW,),
            in_specs=[pl.BlockSpec((SC_WINDOW, w), lambda i: (i, 0)),
                      pl.BlockSpec((1, SC_WINDOW), lambda i: (0, i)),
                      pl.BlockSpec((1, SC_WINDOW), lambda i: (0, i))],
            out_specs=[],
            core_axis_name=("c", "s"),
            dimension_semantics=(pltpu.PARALLEL,),
        )(x_hbm, ia_hbm, ib_hbm)

    return scatter(rows, idx_a.reshape(1, n), idx_b.reshape(1, n))


def _sc_gather_rows(src, idx):
    n = idx.shape[0]
    w = src.shape[1]

    @pl.kernel(out_type=jax.ShapeDtypeStruct((n, w), src.dtype), mesh=_sc_mesh(), scratch_types=[])
    def gather(x_hbm, i_hbm, o_hbm):
        def body(i_vmem, o_vmem):
            pltpu.sync_copy(x_hbm.at[i_vmem.at[0]], o_vmem)

        pltpu.emit_pipeline(
            body,
            grid=(n // SC_WINDOW,),
            in_specs=[pl.BlockSpec((1, SC_WINDOW), lambda i: (0, i))],
            out_specs=[pl.BlockSpec((SC_WINDOW, w), lambda i: (i, 0))],
            core_axis_name=("c", "s"),
            dimension_semantics=(pltpu.PARALLEL,),
        )(i_hbm, o_hbm)

    return gather(src, idx.reshape(1, n))


def _moe_kernel(n_in, be_ref, nused_ref, *refs):
    xb_refs = refs[:n_in]
    wup_ref, wdn_ref = refs[n_in:n_in + 2]
    y_refs = refs[n_in + 2:-2]
    wupb_ref, wdnb_ref = refs[-2:]
    j = pl.program_id(0)
    e = be_ref[j]
    e_prev = be_ref[jnp.maximum(j - 1, 0)]

    @pl.when((j == 0) | (e != e_prev))
    def _():
        wupb_ref[...] = wup_ref[...].astype(BF16)
        wdnb_ref[...] = wdn_ref[...].astype(BF16)

    @pl.when(j < nused_ref[0])
    def _():
        w = jnp.concatenate([r[...] for r in xb_refs], axis=1)
        lo = lax.bitcast_convert_type(w << 16, F32)
        hi = lax.bitcast_convert_type(w & jnp.uint32(0xFFFF0000), F32)
        xrow = jnp.concatenate([lo, hi], axis=1).astype(BF16)
        gu = _dot(xrow, wupb_ref[...])
        gate = gu[:, :MOE_HIDDEN]
        act = (gate * _sigmoid(gate) * gu[:, MOE_HIDDEN:]).astype(BF16)
        y = _dot(act, wdnb_ref[...])
        for i, ref in enumerate(y_refs):
            ref[...] = y[:, i * SC_ROW_WORDS:(i + 1) * SC_ROW_WORDS]


def _moe_call(block_expert, n_used, xb_pieces, w_up, w_down):
    n_rows = xb_pieces[0].shape[0]
    n_in = len(xb_pieces)
    d = 2 * n_in * SC_ROW_WORDS
    n_out = d // SC_ROW_WORDS
    n_blocks = n_rows // MOE_ROWS
    hid2 = w_up.shape[2]
    hid = w_down.shape[1]
    last = lambda j, be, nu: (jnp.minimum(j, nu[0] - 1), 0)
    return pl.pallas_call(
        functools.partial(_moe_kernel, n_in),
        grid_spec=pltpu.PrefetchScalarGridSpec(
            num_scalar_prefetch=2,
            grid=(n_blocks,),
            in_specs=[pl.BlockSpec((MOE_ROWS, SC_ROW_WORDS), last)] * n_in + [
                      pl.BlockSpec((None, d, hid2), lambda j, be, nu: (be[j], 0, 0)),
                      pl.BlockSpec((None, hid, d), lambda j, be, nu: (be[j], 0, 0))],
            out_specs=[pl.BlockSpec((MOE_ROWS, SC_ROW_WORDS), last)] * n_out,
            scratch_shapes=[pltpu.VMEM((d, hid2), BF16), pltpu.VMEM((hid, d), BF16)]),
        out_shape=[jax.ShapeDtypeStruct((n_rows, SC_ROW_WORDS), F32)] * n_out,
        compiler_params=pltpu.CompilerParams(
            dimension_semantics=("arbitrary",), vmem_limit_bytes=VMEM_LIMIT),
        name="moe",
    )(block_expert, n_used, *xb_pieces, w_up, w_down)


def _combine_kernel(n_pieces, x1_ref, route_ref, ada_ref, g_ref, b_ref, *refs):
    o_ref = refs[-1]
    ya = jnp.concatenate([r[...] for r in refs[:n_pieces]], axis=1)
    yb = jnp.concatenate([r[...] for r in refs[n_pieces:2 * n_pieces]], axis=1)
    r = route_ref[...]
    m = ya * r[:, 2:3] + yb * r[:, 3:4]
    g2 = ada_ref[5:6]
    o_ref[...] = _layernorm(ALPHA * x1_ref[...] + (1.0 + g2) * m, g_ref[...], b_ref[...])


def _combine_call(x1, y_pieces, route, ada, ln2g, ln2b):
    bsz, s, d = x1.shape
    ts = min(OUT_ROWS, s)
    nj = s // ts
    n_pieces = len(y_pieces)
    slot = lambda k: pl.BlockSpec((ts, SC_ROW_WORDS), lambda b, j: (k * bsz * nj + b * nj + j, 0))
    return pl.pallas_call(
        functools.partial(_combine_kernel, n_pieces),
        grid=(bsz, nj),
        in_specs=[pl.BlockSpec((None, ts, d), lambda b, j: (b, j, 0)),
                  pl.BlockSpec((None, ts, LANES), lambda b, j: (b, j, 0)),
                  pl.BlockSpec((None,) + ada.shape[1:], lambda b, j: (b, 0, 0)),
                  pl.BlockSpec((1, d), lambda b, j: (0, 0)),
                  pl.BlockSpec((1, d), lambda b, j: (0, 0))]
                 + [slot(0)] * n_pieces + [slot(1)] * n_pieces,
        out_specs=pl.BlockSpec((None, ts, d), lambda b, j: (b, j, 0)),
        out_shape=jax.ShapeDtypeStruct((bsz, s, d), F32),
        compiler_params=pltpu.CompilerParams(dimension_semantics=("arbitrary", "arbitrary")),
        name="combine",
    )(x1, route, ada, ln2g, ln2b, *y_pieces, *y_pieces)


def kernel(x, c, w_ada, b_ada, w_in, lb_logits, hg_norm_w, sg_ln_g, sg_ln_b, sg_w, sg_b, w_out, ln1_g, ln1_b, router_group_w, router_group_b, router_expert_w, router_expert_b, w_up, w_down, ln2_g, ln2_b):
    assert w_in.shape[0] == DEPTH
    bsz, s, d = x.shape
    t = bsz * s
    l = 0

    ada = _ada_call(c, w_ada[l], b_ada[l])

    wr = jnp.concatenate(
        [router_group_w[l], router_expert_w[l].transpose(1, 0, 2).reshape(d, MOE_EXPERTS)], axis=1)
    wr = jnp.pad(wr, ((0, 0), (0, LANES - wr.shape[1])))
    br = jnp.concatenate([router_group_b[l], router_expert_b[l].reshape(MOE_EXPERTS)])
    br = jnp.pad(br, (0, LANES - br.shape[0])).reshape(1, LANES)
    wrh, wrl = _split_bf16(wr)

    x1, route, *h2p = _mix_call(
        x, ada, w_in[l].astype(BF16), lb_logits, hg_norm_w[l].reshape(1, -1),
        sg_ln_g[l].reshape(1, -1), sg_ln_b[l].reshape(1, -1), sg_w[l], sg_b[l].T,
        w_out[l].astype(BF16), ln1_g[l].reshape(1, d), ln1_b[l].reshape(1, d), wrh, wrl, br)

    n_blocks = -(-(2 * t) // MOE_ROWS) + MOE_EXPERTS
    n_rows = n_blocks * MOE_ROWS
    dest, blk = _sort_call(route.reshape(t, LANES), n_blocks)
    d0, d1 = dest[:, 0], dest[:, 1]

    xb = [_sc_scatter_rows2(p.reshape(t, SC_ROW_WORDS), d0, d1, n_rows) for p in h2p]
    yb = _moe_call(blk[:, 0], blk[0:1, 1].reshape(1), xb, w_up[l], w_down[l])
    d01 = jnp.concatenate([d0, d1])
    y2 = [_sc_gather_rows(p, d01) for p in yb]
    return _combine_call(x1, y2, route, ada, ln2_g[l].reshape(1, d), ln2_b[l].reshape(1, d))
```

```python
import functools

import jax
import jax.numpy as jnp
from jax import lax
from jax.experimental import pallas as pl
from jax.experimental.pallas import tpu as pltpu
from jax.experimental.pallas import tpu_sc as plsc

F32 = jnp.float32
BF16 = jnp.bfloat16
I32 = jnp.int32
U32 = jnp.uint32

HG_HEADS = 4
HG_DK = 128
HG_CHUNK = 64
HG_SUB = 8
SG_GROUPS = 4
SG_CH = 128
SG_CHUNK = 128
MOE_GROUPS = 4
MOE_EPG = 8
MOE_EXPERTS = MOE_GROUPS * MOE_EPG
MOE_HIDDEN = 512
DEPTH = 1
ALPHA = (2.0 * DEPTH) ** 0.25
LN_EPS = 1e-5
RMS_EPS = 1e-6
LOG2E = 1.4426950408889634

LANES = 128
SUBLANES = 8
SC_WINDOW = 128
SC_ROW_WORDS = 256

MIX_ROWS = 512
SORT_ROWS = 512
MOE_ROWS = 256
OUT_ROWS = 512
VMEM_LIMIT = 52 * 1024 * 1024


def _dot(a, b):
    return jnp.dot(a, b, preferred_element_type=F32)


def _dot_nt(a, b):
    return lax.dot_general(a, b, (((1,), (1,)), ((), ())), preferred_element_type=F32)


def _dot_tn(a, b):
    return lax.dot_general(a, b, (((0,), (0,)), ((), ())), preferred_element_type=F32)


def _sigmoid(x):
    return jax.nn.sigmoid(x)


def _gelu_exact(x):
    return 0.5 * x * (1.0 + lax.erf(x * (2.0 ** -0.5)))


def _layernorm(x, g, b):
    mu = jnp.mean(x, axis=-1, keepdims=True)
    xc = x - mu
    var = jnp.mean(xc * xc, axis=-1, keepdims=True)
    return xc * lax.rsqrt(var + LN_EPS) * g + b


def _split_bf16(x):
    hi = x.astype(BF16)
    lo = (x - hi.astype(F32)).astype(BF16)
    return hi, lo


def _ada_kernel(c_ref, w_ref, b_ref, o_ref):
    c = c_ref[...]
    ca = c * _sigmoid(c)
    o_ref[...] = jnp.dot(ca, w_ref[...], preferred_element_type=F32,
                         precision=lax.Precision.HIGHEST) + b_ref[...]


def _ada_call(c, w_ada, b_ada):
    bsz, d = c.shape
    n = w_ada.shape[1]
    rows = -(-bsz // SUBLANES) * SUBLANES
    c_pad = jnp.pad(c, ((0, rows - bsz), (0, 0)))
    out = pl.pallas_call(
        _ada_kernel,
        grid=(n // d,),
        in_specs=[pl.BlockSpec((rows, d), lambda i: (0, 0)),
                  pl.BlockSpec((d, d), lambda i: (0, i)),
                  pl.BlockSpec((1, d), lambda i: (0, i))],
        out_specs=pl.BlockSpec((rows, d), lambda i: (0, i)),
        out_shape=jax.ShapeDtypeStruct((rows, n), F32),
        name="ada",
    )(c_pad, w_ada, b_ada.reshape(1, n))
    return out[:bsz].reshape(bsz, n // d, d)


def _hgrn2_chunk(c, proj_ref, y_ref, st_ref, g_ref, h_ref, lb, gnw, tril):
    hw = HG_HEADS * HG_DK
    cl = HG_CHUNK
    r0 = pl.multiple_of(c * cl, cl)
    rows = pl.ds(r0, cl)
    qz = proj_ref[rows, 0:hw]
    fz = proj_ref[rows, hw:2 * hw]
    v = proj_ref[rows, 2 * hw:3 * hw]
    og = proj_ref[rows, 3 * hw:4 * hw]

    q = qz * _sigmoid(qz)
    f = lb + (1.0 - lb) * _sigmoid(fz)
    lf = jnp.log(f)
    k = 1.0 - f
    lf_hi, lf_lo = _split_bf16(lf)
    g = (_dot(tril, lf_hi) + _dot(tril, lf_lo)) * LOG2E
    glast = g[cl - 1:cl, :]
    qg = (q * jnp.exp2(g)).astype(BF16)
    kd = (k * jnp.exp2(glast - g)).astype(BF16)
    vb = v.astype(BF16)
    g_ref[...] = g
    h_ref[...] = g - jnp.log(k) * LOG2E

    n_sub = cl // HG_SUB
    lane_c = lax.broadcasted_iota(I32, (HG_SUB, cl), 1)
    trow = lax.broadcasted_iota(I32, (HG_SUB, cl), 0)

    for hd in range(HG_HEADS):
        cs = slice(hd * HG_DK, (hd + 1) * HG_DK)
        st = st_ref[hd]
        o = _dot_nt(qg[:, cs], st.astype(BF16))
        gh = g[:, cs]
        kh = k[:, cs]
        qh = q[:, cs]
        acc = [None] * n_sub
        w = cl // 2
        while w >= HG_SUB:
            for p in range(0, cl, 2 * w):
                ref = g_ref[p + w - 1:p + w, cs]
                qt = (qh[p + w:p + 2 * w] * jnp.exp2(gh[p + w:p + 2 * w] - ref)).astype(BF16)
                kt = (kh[p:p + w] * jnp.exp2(ref - gh[p:p + w])).astype(BF16)
                pieces = [kt]
                if p:
                    pieces.insert(0, jnp.zeros((p, HG_DK), BF16))
                if cl - p - w:
                    pieces.append(jnp.zeros((cl - p - w, HG_DK), BF16))
                blk = _dot_nt(qt, jnp.concatenate(pieces, axis=0))
                for r in range(w // HG_SUB):
                    i = (p + w) // HG_SUB + r
                    part = blk[r * HG_SUB:(r + 1) * HG_SUB]
                    acc[i] = part if acc[i] is None else acc[i] + part
            w //= 2
        a_rows = []
        for i in range(n_sub):
            b0 = i * HG_SUB
            gb = gh[b0:b0 + HG_SUB]
            qb = qh[b0:b0 + HG_SUB]
            a = jnp.zeros((HG_SUB, cl), F32) if acc[i] is None else acc[i]
            for s in range(HG_SUB):
                hs = h_ref[b0 + s:b0 + s + 1, cs]
                col = jnp.sum(qb * jnp.exp2(gb - hs), axis=-1, keepdims=True)
                a = jnp.where(lane_c == b0 + s, col, a)
            a_rows.append(jnp.where(lane_c <= b0 + trow, a, 0.0))
        amat = jnp.concatenate(a_rows, axis=0).astype(BF16)
        o = o + _dot(amat, vb[:, cs])
        st_ref[hd] = st * jnp.exp2(glast[:, cs]) + _dot_tn(vb[:, cs], kd[:, cs])
        ms = jnp.mean(o * o, axis=-1, keepdims=True)
        ogh = og[:, cs]
        ya = o * lax.rsqrt(ms + RMS_EPS) * gnw * (ogh * _sigmoid(ogh))
        y_ref[rows, cs] = ya.astype(BF16)


def _route(logits):
    ts = logits.shape[0]
    lane = lax.broadcasted_iota(I32, (ts, LANES), 1)
    neg = -jnp.inf
    gl = jnp.where(lane < MOE_GROUPS, logits, neg)
    gmax = jnp.max(gl, axis=-1, keepdims=True)
    gidx = jnp.min(jnp.where(gl == gmax, lane, LANES), axis=-1, keepdims=True)
    p_group = 1.0 / jnp.sum(jnp.exp(gl - gmax), axis=-1, keepdims=True)
    e_lo = MOE_GROUPS + gidx * MOE_EPG
    el = jnp.where((lane >= e_lo) & (lane < e_lo + MOE_EPG), logits, neg)
    m1 = jnp.max(el, axis=-1, keepdims=True)
    i1 = jnp.min(jnp.where(el == m1, lane, LANES), axis=-1, keepdims=True)
    el2 = jnp.where(lane == i1, neg, el)
    m2 = jnp.max(el2, axis=-1, keepdims=True)
    i2 = jnp.min(jnp.where(el2 == m2, lane, LANES), axis=-1, keepdims=True)
    esum = jnp.sum(jnp.exp(el - m1), axis=-1, keepdims=True)
    p1 = 1.0 / esum
    p2 = jnp.exp(m2 - m1) / esum
    den = p1 + p2
    w1 = p_group * p1 / den
    w2 = p_group * p2 / den
    e1 = (i1 - MOE_GROUPS).astype(F32)
    e2 = (i2 - MOE_GROUPS).astype(F32)
    return jnp.where(lane == 0, e1, jnp.where(lane == 1, e2,
                     jnp.where(lane == 2, w1, jnp.where(lane == 3, w2, 0.0))))


def _mix_kernel(x_ref, ada_ref, win_ref, lbl_ref, gnw_ref, sglg_ref, sglb_ref, sgw_ref, sgbt_ref,
                wout_ref, ln1g_ref, ln1b_ref, wrh_ref, wrl_ref, br_ref,
                x1_ref, route_ref, *rest):
    h2p_refs = rest[:-5]
    st_ref, proj_ref, y_ref, g_ref, k_ref = rest[-5:]
    ts, d = x_ref.shape
    hw = HG_HEADS * HG_DK
    sgw = SG_GROUPS * SG_CH

    @pl.when(pl.program_id(1) == 0)
    def _():
        st_ref[...] = jnp.zeros_like(st_ref)

    x = x_ref[...]
    ada = ada_ref[...]
    sh1, sc1, g1, sh2, sc2 = (ada[i:i + 1] for i in range(5))
    h = (x * (1.0 + sc1) + sh1).astype(BF16)
    proj_ref[...] = _dot(h, win_ref[...])

    lbl = lbl_ref[...]
    slots = [lbl[i:i + 1] for i in range(lbl.shape[0])]
    mx = functools.reduce(jnp.maximum, slots)
    ex = [jnp.exp(s - mx) for s in slots]
    lb = ex[0] / functools.reduce(lambda a, b: a + b, ex)

    ci = lax.broadcasted_iota(I32, (HG_CHUNK, HG_CHUNK), 0)
    cj = lax.broadcasted_iota(I32, (HG_CHUNK, HG_CHUNK), 1)
    tril = (ci >= cj).astype(BF16)
    gnw = gnw_ref[...]

    def chunk_body(c, carry):
        _hgrn2_chunk(c, proj_ref, y_ref, st_ref, g_ref, k_ref, lb, gnw, tril)
        return carry

    lax.fori_loop(0, ts // HG_CHUNK, chunk_body, 0, unroll=True)

    pi = lax.broadcasted_iota(I32, (SG_CHUNK, SG_CHUNK), 0)
    pj = lax.broadcasted_iota(I32, (SG_CHUNK, SG_CHUNK), 1)
    sglg = sglg_ref[...]
    sglb = sglb_ref[...]
    for p in range(ts // SG_CHUNK):
        rows = slice(p * SG_CHUNK, (p + 1) * SG_CHUNK)
        z = _gelu_exact(proj_ref[rows, 4 * hw:4 * hw + 2 * sgw])
        u = z[:, :sgw]
        vn = _layernorm(z[:, sgw:], sglg, sglb).astype(BF16)
        for gi in range(SG_GROUPS):
            cs = slice(gi * SG_CH, (gi + 1) * SG_CH)
            wc = jnp.where(pi >= pj, sgw_ref[gi], 0.0).astype(BF16)
            mixed = _dot(wc, vn[:, cs]) + sgbt_ref[:, gi:gi + 1]
            y_ref[rows, hw + gi * SG_CH:hw + (gi + 1) * SG_CH] = (u[:, cs] * mixed).astype(BF16)

    y = _dot(y_ref[...], wout_ref[...])
    x1 = _layernorm(ALPHA * x + (1.0 + g1) * y, ln1g_ref[...], ln1b_ref[...])
    x1_ref[...] = x1
    h2 = x1 * (1.0 + sc2) + sh2

    h2_hi, h2_lo = _split_bf16(h2)
    bits = lax.bitcast_convert_type(h2_hi.astype(F32), U32)
    packed = bits[:, d // 2:] | (bits[:, :d // 2] >> 16)
    for i, ref in enumerate(h2p_refs):
        ref[...] = packed[:, i * SC_ROW_WORDS:(i + 1) * SC_ROW_WORDS]

    wrh = wrh_ref[...]
    logits = _dot(h2_hi, wrh) + _dot(h2_lo, wrh) + _dot(h2_hi, wrl_ref[...]) + br_ref[...]
    route_ref[...] = _route(logits)


def _mix_call(x, ada, w_in_b, lb_logits, gnw, sglg, sglb, sg_w, sg_bt, w_out_b, ln1g, ln1b, wrh, wrl, br):
    bsz, s, d = x.shape
    ts = min(MIX_ROWS, s)
    ncol = w_in_b.shape[1]
    hw = HG_HEADS * HG_DK
    n_pieces = d // 2 // SC_ROW_WORDS
    full = lambda a: pl.BlockSpec(a.shape, lambda b, j: (0,) * a.ndim)
    return pl.pallas_call(
        _mix_kernel,
        grid=(bsz, s // ts),
        in_specs=[pl.BlockSpec((None, ts, d), lambda b, j: (b, j, 0)),
                  pl.BlockSpec((None,) + ada.shape[1:], lambda b, j: (b, 0, 0)),
                  full(w_in_b), full(lb_logits), full(gnw), full(sglg), full(sglb), full(sg_w),
                  full(sg_bt), full(w_out_b), full(ln1g), full(ln1b), full(wrh), full(wrl), full(br)],
        out_specs=[pl.BlockSpec((None, ts, d), lambda b, j: (b, j, 0)),
                   pl.BlockSpec((None, ts, LANES), lambda b, j: (b, j, 0))]
                  + [pl.BlockSpec((None, ts, SC_ROW_WORDS), lambda b, j: (b, j, 0))] * n_pieces,
        out_shape=[jax.ShapeDtypeStruct((bsz, s, d), F32),
                   jax.ShapeDtypeStruct((bsz, s, LANES), F32)]
                  + [jax.ShapeDtypeStruct((bsz, s, SC_ROW_WORDS), U32)] * n_pieces,
        scratch_shapes=[pltpu.VMEM((HG_HEADS, HG_DK, HG_DK), F32),
                        pltpu.VMEM((ts, ncol), F32),
                        pltpu.VMEM((ts, d), BF16),
                        pltpu.VMEM((HG_CHUNK, hw), F32),
                        pltpu.VMEM((HG_CHUNK, hw), F32)],
        compiler_params=pltpu.CompilerParams(
            dimension_semantics=("arbitrary", "arbitrary"), vmem_limit_bytes=VMEM_LIMIT),
        name="mix",
    )(x, ada, w_in_b, lb_logits, gnw, sglg, sglb, sg_w, sg_bt, w_out_b, ln1g, ln1b, wrh, wrl, br)


def _sort_kernel(route_ref, dest_ref, blk_ref, cnt_ref, pst_ref, carry_ref):
    ph = pl.program_id(0)
    i = pl.program_id(1)
    tk = route_ref.shape[0]
    nb = blk_ref.shape[0]
    lane = lax.broadcasted_iota(I32, (tk, LANES), 1)
    r = route_ref[...]
    oh1 = lane == r[:, 0:1].astype(I32)
    oh2 = lane == r[:, 1:2].astype(I32)
    hot = jnp.where(oh1 | oh2, 1.0, 0.0)
    colsum = jnp.sum(hot, axis=0, keepdims=True)

    @pl.when((ph == 0) & (i == 0))
    def _():
        cnt_ref[...] = jnp.zeros_like(cnt_ref)

    @pl.when(ph == 0)
    def _():
        cnt_ref[...] += colsum

    @pl.when((ph == 1) & (i == 0))
    def _():
        nblk = jnp.floor((cnt_ref[...] + (MOE_ROWS - 1)) * (1.0 / MOE_ROWS))
        ei = lax.broadcasted_iota(I32, (LANES, LANES), 0)
        ej = lax.broadcasted_iota(I32, (LANES, LANES), 1)
        upper = (ei < ej).astype(BF16)
        nblk8 = jnp.broadcast_to(nblk, (SUBLANES, LANES)).astype(BF16)
        pstart = _dot(nblk8, upper)[0:1]
        pend = pstart + nblk
        pst_ref[...] = pstart * float(MOE_ROWS)
        carry_ref[...] = jnp.zeros_like(carry_ref)
        jrow = lax.broadcasted_iota(I32, (nb, LANES), 0).astype(F32)
        lane_b = lax.broadcasted_iota(I32, (nb, LANES), 1)
        done = jnp.where((pend <= jrow) & (lane_b < MOE_EXPERTS), 1.0, 0.0)
        bexp = jnp.minimum(jnp.sum(done, axis=-1, keepdims=True), float(MOE_EXPERTS - 1))
        total = jnp.sum(jnp.where(lane_b == MOE_EXPERTS - 1, pend, 0.0), axis=-1, keepdims=True)
        blk_ref[...] = jnp.where(lane_b == 0, bexp, jnp.where(lane_b == 1, total, 0.0)).astype(I32)

    @pl.when(ph == 1)
    def _():
        ti = lax.broadcasted_iota(I32, (tk, tk), 0)
        tj = lax.broadcasted_iota(I32, (tk, tk), 1)
        before = (ti > tj).astype(BF16)
        base = _dot(before, hot.astype(BF16)) + carry_ref[...] + pst_ref[...]
        d1 = jnp.sum(jnp.where(oh1, base, 0.0), axis=-1, keepdims=True)
        d2 = jnp.sum(jnp.where(oh2, base, 0.0), axis=-1, keepdims=True)
        dest_ref[...] = jnp.where(lane == 0, d1, jnp.where(lane == 1, d2, 0.0)).astype(I32)
        carry_ref[...] += colsum


def _sort_call(route, n_blocks):
    t = route.shape[0]
    tk = min(SORT_ROWS, t)
    return pl.pallas_call(
        _sort_kernel,
        grid=(2, t // tk),
        in_specs=[pl.BlockSpec((tk, LANES), lambda p, i: (i, 0))],
        out_specs=[pl.BlockSpec((tk, LANES), lambda p, i: (i * p, 0)),
                   pl.BlockSpec((n_blocks, LANES), lambda p, i: (0, 0))],
        out_shape=[jax.ShapeDtypeStruct((t, LANES), I32),
                   jax.ShapeDtypeStruct((n_blocks, LANES), I32)],
        scratch_shapes=[pltpu.VMEM((1, LANES), F32), pltpu.VMEM((1, LANES), F32),
                        pltpu.VMEM((1, LANES), F32)],
        compiler_params=pltpu.CompilerParams(dimension_semantics=("arbitrary", "arbitrary")),
        name="sort",
    )(route)


def _sc_mesh():
    return plsc.VectorSubcoreMesh(core_axis_name="c", subcore_axis_name="s")


def _sc_scatter_rows2(rows, idx_a, idx_b, n_out):
    n, w = rows.shape

    @pl.kernel(out_type=jax.ShapeDtypeStruct((n_out, w), rows.dtype), mesh=_sc_mesh(), scratch_types=[])
    def scatter(x_hbm, ia_hbm, ib_hbm, o_hbm):
        def body(x_vmem, ia_vmem, ib_vmem):
            pltpu.sync_copy(x_vmem, o_hbm.at[ia_vmem.at[0]])
            pltpu.sync_copy(x_vmem, o_hbm.at[ib_vmem.at[0]])

        pltpu.emit_pipeline(
            body,
            grid=(n // SC_WINDOW,),
            in_specs=[pl.BlockSpec((SC_WINDOW, w), lambda i: (i, 0)),
                      pl.BlockSpec((1, SC_WINDOW), lambda i: (0, i)),
                      pl.BlockSpec((1, SC_WINDOW), lambda i: (0, i))],
            out_specs=[],
            core_axis_name=("c", "s"),
            dimension_semantics=(pltpu.PARALLEL,),
        )(x_hbm, ia_hbm, ib_hbm)

    return scatter(rows, idx_a.reshape(1, n), idx_b.reshape(1, n))


def _sc_gather_rows(src, idx):
    n = idx.shape[0]
    w = src.shape[1]

    @pl.kernel(out_type=jax.ShapeDtypeStruct((n, w), src.dtype), mesh=_sc_mesh(), scratch_types=[])
    def gather(x_hbm, i_hbm, o_hbm):
        def body(i_vmem, o_vmem):
            pltpu.sync_copy(x_hbm.at[i_vmem.at[0]], o_vmem)

        pltpu.emit_pipeline(
            body,
            grid=(n // SC_WINDOW,),
            in_specs=[pl.BlockSpec((1, SC_WINDOW), lambda i: (0, i))],
            out_specs=[pl.BlockSpec((SC_WINDOW, w), lambda i: (i, 0))],
            core_axis_name=("c", "s"),
            dimension_semantics=(pltpu.PARALLEL,),
        )(i_hbm, o_hbm)

    return gather(src, idx.reshape(1, n))


def _moe_kernel(n_in, be_ref, nused_ref, *refs):
    xb_refs = refs[:n_in]
    wup_ref, wdn_ref = refs[n_in:n_in + 2]
    y_refs = refs[n_in + 2:-2]
    wupb_ref, wdnb_ref = refs[-2:]
    j = pl.program_id(0)
    e = be_ref[j]
    e_prev = be_ref[jnp.maximum(j - 1, 0)]

    @pl.when((j == 0) | (e != e_prev))
    def _():
        wupb_ref[...] = wup_ref[...].astype(BF16)
        wdnb_ref[...] = wdn_ref[...].astype(BF16)

    @pl.when(j < nused_ref[0])
    def _():
        w = jnp.concatenate([r[...] for r in xb_refs], axis=1)
        lo = lax.bitcast_convert_type(w << 16, F32)
        hi = lax.bitcast_convert_type(w & jnp.uint32(0xFFFF0000), F32)
        xrow = jnp.concatenate([lo, hi], axis=1).astype(BF16)
        gu = _dot(xrow, wupb_ref[...])
        gate = gu[:, :MOE_HIDDEN]
        act = (gate * _sigmoid(gate) * gu[:, MOE_HIDDEN:]).astype(BF16)
        y = _dot(act, wdnb_ref[...])
        for i, ref in enumerate(y_refs):
            ref[...] = y[:, i * SC_ROW_WORDS:(i + 1) * SC_ROW_WORDS]


def _moe_call(block_expert, n_used, xb_pieces, w_up, w_down):
    n_rows = xb_pieces[0].shape[0]
    n_in = len(xb_pieces)
    d = 2 * n_in * SC_ROW_WORDS
    n_out = d // SC_ROW_WORDS
    n_blocks = n_rows // MOE_ROWS
    hid2 = w_up.shape[2]
    hid = w_down.shape[1]
    last = lambda j, be, nu: (jnp.minimum(j, nu[0] - 1), 0)
    return pl.pallas_call(
        functools.partial(_moe_kernel, n_in),
        grid_spec=pltpu.PrefetchScalarGridSpec(
            num_scalar_prefetch=2,
            grid=(n_blocks,),
            in_specs=[pl.BlockSpec((MOE_ROWS, SC_ROW_WORDS), last)] * n_in + [
                      pl.BlockSpec((None, d, hid2), lambda j, be, nu: (be[j], 0, 0)),
                      pl.BlockSpec((None, hid, d), lambda j, be, nu: (be[j], 0, 0))],
            out_specs=[pl.BlockSpec((MOE_ROWS, SC_ROW_WORDS), last)] * n_out,
            scratch_shapes=[pltpu.VMEM((d, hid2), BF16), pltpu.VMEM((hid, d), BF16)]),
        out_shape=[jax.ShapeDtypeStruct((n_rows, SC_ROW_WORDS), F32)] * n_out,
        compiler_params=pltpu.CompilerParams(
            dimension_semantics=("arbitrary",), vmem_limit_bytes=VMEM_LIMIT),
        name="moe",
    )(block_expert, n_used, *xb_pieces, w_up, w_down)


def _combine_kernel(n_pieces, x1_ref, route_ref, ada_ref, g_ref, b_ref, *refs):
    o_ref = refs[-1]
    ya = jnp.concatenate([r[...] for r in refs[:n_pieces]], axis=1)
    yb = jnp.concatenate([r[...] for r in refs[n_pieces:2 * n_pieces]], axis=1)
    r = route_ref[...]
    m = ya * r[:, 2:3] + yb * r[:, 3:4]
    g2 = ada_ref[5:6]
    o_ref[...] = _layernorm(ALPHA * x1_ref[...] + (1.0 + g2) * m, g_ref[...], b_ref[...])


def _combine_call(x1, y_pieces, route, ada, ln2g, ln2b):
    bsz, s, d = x1.shape
    ts = min(OUT_ROWS, s)
    nj = s // ts
    n_pieces = len(y_pieces)
    slot = lambda k: pl.BlockSpec((ts, SC_ROW_WORDS), lambda b, j: (k * bsz * nj + b * nj + j, 0))
    return pl.pallas_call(
        functools.partial(_combine_kernel, n_pieces),
        grid=(bsz, nj),
        in_specs=[pl.BlockSpec((None, ts, d), lambda b, j: (b, j, 0)),
                  pl.BlockSpec((None, ts, LANES), lambda b, j: (b, j, 0)),
                  pl.BlockSpec((None,) + ada.shape[1:], lambda b, j: (b, 0, 0)),
                  pl.BlockSpec((1, d), lambda b, j: (0, 0)),
                  pl.BlockSpec((1, d), lambda b, j: (0, 0))]
                 + [slot(0)] * n_pieces + [slot(1)] * n_pieces,
        out_specs=pl.BlockSpec((None, ts, d), lambda b, j: (b, j, 0)),
        out_shape=jax.ShapeDtypeStruct((bsz, s, d), F32),
        compiler_params=pltpu.CompilerParams(dimension_semantics=("arbitrary", "arbitrary")),
        name="combine",
    )(x1, route, ada, ln2g, ln2b, *y_pieces, *y_pieces)


def kernel(x, c, w_ada, b_ada, w_in, lb_logits, hg_norm_w, sg_ln_g, sg_ln_b, sg_w, sg_b, w_out, ln1_g, ln1_b, router_group_w, router_group_b, router_expert_w, router_expert_b, w_up, w_down, ln2_g, ln2_b):
    assert w_in.shape[0] == DEPTH
    bsz, s, d = x.shape
    t = bsz * s
    l = 0

    ada = _ada_call(c, w_ada[l], b_ada[l])

    wr = jnp.concatenate(
        [router_group_w[l], router_expert_w[l].transpose(1, 0, 2).reshape(d, MOE_EXPERTS)], axis=1)
    wr = jnp.pad(wr, ((0, 0), (0, LANES - wr.shape[1])))
    br = jnp.concatenate([router_group_b[l], router_expert_b[l].reshape(MOE_EXPERTS)])
    br = jnp.pad(br, (0, LANES - br.shape[0])).reshape(1, LANES)
    wrh, wrl = _split_bf16(wr)

    x1, route, *h2p = _mix_call(
        x, ada, w_in[l].astype(BF16), lb_logits, hg_norm_w[l].reshape(1, -1),
        sg_ln_g[l].reshape(1, -1), sg_ln_b[l].reshape(1, -1), sg_w[l], sg_b[l].T,
        w_out[l].astype(BF16), ln1_g[l].reshape(1, d), ln1_b[l].reshape(1, d), wrh, wrl, br)

    n_blocks = -(-(2 * t) // MOE_ROWS) + MOE_EXPERTS
    n_rows = n_blocks * MOE_ROWS
    dest, blk = _sort_call(route.reshape(t, LANES), n_blocks)
    d0, d1 = dest[:, 0], dest[:, 1]

    xb = [_sc_scatter_rows2(p.reshape(t, SC_ROW_WORDS), d0, d1, n_rows) for p in h2p]
    yb = _moe_call(blk[:, 0], blk[0:1, 1].reshape(1), xb, w_up[l], w_down[l])
    d01 = jnp.concatenate([d0, d1])
    y2 = [_sc_gather_rows(p, d01) for p in yb]
    return _combine_call(x1, y2, route, ada, ln2_g[l].reshape(1, d), ln2_b[l].reshape(1, d))
```

```python
import functools

import jax
import jax.numpy as jnp
from jax import lax
from jax.experimental import pallas as pl
from jax.experimental.pallas import tpu as pltpu
from jax.experimental.pallas import tpu_sc as plsc

F32 = jnp.float32
BF16 = jnp.bfloat16
I32 = jnp.int32
U32 = jnp.uint32

HG_HEADS = 4
HG_DK = 128
HG_CHUNK = 64
HG_SUB = 8
SG_GROUPS = 4
SG_CH = 128
SG_CHUNK = 128
MOE_GROUPS = 4
MOE_EPG = 8
MOE_EXPERTS = MOE_GROUPS * MOE_EPG
MOE_HIDDEN = 512
DEPTH = 1
ALPHA = (2.0 * DEPTH) ** 0.25
LN_EPS = 1e-5
RMS_EPS = 1e-6
LOG2E = 1.4426950408889634

LANES = 128
SUBLANES = 8
SC_WINDOW = 128
SC_ROW_WORDS = 256

MIX_ROWS = 512
SORT_ROWS = 1024
MOE_ROWS = 256
OUT_ROWS = 512
VMEM_LIMIT = 52 * 1024 * 1024


def _dot(a, b):
    return jnp.dot(a, b, preferred_element_type=F32)


def _dot_nt(a, b):
    return lax.dot_general(a, b, (((1,), (1,)), ((), ())), preferred_element_type=F32)


def _dot_tn(a, b):
    return lax.dot_general(a, b, (((0,), (0,)), ((), ())), preferred_element_type=F32)


def _sigmoid(x):
    return jax.nn.sigmoid(x)


def _gelu_exact(x):
    return 0.5 * x * (1.0 + lax.erf(x * (2.0 ** -0.5)))


def _layernorm(x, g, b):
    mu = jnp.mean(x, axis=-1, keepdims=True)
    xc = x - mu
    var = jnp.mean(xc * xc, axis=-1, keepdims=True)
    return xc * lax.rsqrt(var + LN_EPS) * g + b


def _pack_bf16_pairs(x):
    n = x.shape[1]
    bits = lax.bitcast_convert_type(x.astype(BF16).astype(F32), U32)
    return bits[:, n // 2:] | (bits[:, :n // 2] >> 16)


def _unpack_bf16_pairs(w):
    lo = lax.bitcast_convert_type(w << 16, F32)
    hi = lax.bitcast_convert_type(w & jnp.uint32(0xFFFF0000), F32)
    return jnp.concatenate([lo, hi], axis=1)


def _split_bf16(x):
    hi = x.astype(BF16)
    lo = (x - hi.astype(F32)).astype(BF16)
    return hi, lo


def _ada_kernel(c_ref, w_ref, b_ref, o_ref):
    c = c_ref[...]
    ca = c * _sigmoid(c)
    o_ref[...] = jnp.dot(ca, w_ref[...], preferred_element_type=F32,
                         precision=lax.Precision.HIGHEST) + b_ref[...]


def _ada_call(c, w_ada, b_ada):
    bsz, d = c.shape
    n = w_ada.shape[1]
    rows = -(-bsz // SUBLANES) * SUBLANES
    c_pad = jnp.pad(c, ((0, rows - bsz), (0, 0)))
    out = pl.pallas_call(
        _ada_kernel,
        grid=(n // d,),
        in_specs=[pl.BlockSpec((rows, d), lambda i: (0, 0)),
                  pl.BlockSpec((d, d), lambda i: (0, i)),
                  pl.BlockSpec((1, d), lambda i: (0, i))],
        out_specs=pl.BlockSpec((rows, d), lambda i: (0, i)),
        out_shape=jax.ShapeDtypeStruct((rows, n), F32),
        name="ada",
    )(c_pad, w_ada, b_ada.reshape(1, n))
    return out[:bsz].reshape(bsz, n // d, d)


def _hgrn2_chunk(c, proj_ref, y_ref, st_ref, g_ref, h_ref, lb, gnw, tril):
    hw = HG_HEADS * HG_DK
    cl = HG_CHUNK
    r0 = pl.multiple_of(c * cl, cl)
    rows = pl.ds(r0, cl)
    qz = proj_ref[rows, 0:hw]
    fz = proj_ref[rows, hw:2 * hw]
    v = proj_ref[rows, 2 * hw:3 * hw]
    og = proj_ref[rows, 3 * hw:4 * hw]

    q = qz * _sigmoid(qz)
    f = lb + (1.0 - lb) * _sigmoid(fz)
    lf = jnp.log(f)
    k = 1.0 - f
    lf_hi, lf_lo = _split_bf16(lf)
    g = (_dot(tril, lf_hi) + _dot(tril, lf_lo)) * LOG2E
    glast = g[cl - 1:cl, :]
    qg = (q * jnp.exp2(g)).astype(BF16)
    kd = (k * jnp.exp2(glast - g)).astype(BF16)
    vb = v.astype(BF16)
    g_ref[...] = g
    h_ref[...] = g - jnp.log(k) * LOG2E

    n_sub = cl // HG_SUB
    lane_c = lax.broadcasted_iota(I32, (HG_SUB, cl), 1)
    trow = lax.broadcasted_iota(I32, (HG_SUB, cl), 0)

    for hd in range(HG_HEADS):
        cs = slice(hd * HG_DK, (hd + 1) * HG_DK)
        st = st_ref[hd]
        o = _dot_nt(qg[:, cs], st.astype(BF16))
        gh = g[:, cs]
        kh = k[:, cs]
        qh = q[:, cs]
        acc = [None] * n_sub
        w = cl // 2
        while w >= HG_SUB:
            for p in range(0, cl, 2 * w):
                ref = g_ref[p + w - 1:p + w, cs]
                qt = (qh[p + w:p + 2 * w] * jnp.exp2(gh[p + w:p + 2 * w] - ref)).astype(BF16)
                kt = (kh[p:p + w] * jnp.exp2(ref - gh[p:p + w])).astype(BF16)
                pieces = [kt]
                if p:
                    pieces.insert(0, jnp.zeros((p, HG_DK), BF16))
                if cl - p - w:
                    pieces.append(jnp.zeros((cl - p - w, HG_DK), BF16))
                blk = _dot_nt(qt, jnp.concatenate(pieces, axis=0))
                for r in range(w // HG_SUB):
                    i = (p + w) // HG_SUB + r
                    part = blk[r * HG_SUB:(r + 1) * HG_SUB]
                    acc[i] = part if acc[i] is None else acc[i] + part
            w //= 2
        a_rows = []
        for i in range(n_sub):
            b0 = i * HG_SUB
            gb = gh[b0:b0 + HG_SUB]
            qb = qh[b0:b0 + HG_SUB]
            a = jnp.zeros((HG_SUB, cl), F32) if acc[i] is None else acc[i]
            for s in range(HG_SUB):
                hs = h_ref[b0 + s:b0 + s + 1, cs]
                col = jnp.sum(qb * jnp.exp2(gb - hs), axis=-1, keepdims=True)
                a = jnp.where(lane_c == b0 + s, col, a)
            a_rows.append(jnp.where(lane_c <= b0 + trow, a, 0.0))
        amat = jnp.concatenate(a_rows, axis=0).astype(BF16)
        o = o + _dot(amat, vb[:, cs])
        st_ref[hd] = st * jnp.exp2(glast[:, cs]) + _dot_tn(vb[:, cs], kd[:, cs])
        ms = jnp.mean(o * o, axis=-1, keepdims=True)
        ogh = og[:, cs]
        ya = o * lax.rsqrt(ms + RMS_EPS) * gnw * (ogh * _sigmoid(ogh))
        y_ref[rows, cs] = ya.astype(BF16)


def _route(logits):
    ts = logits.shape[0]
    lane = lax.broadcasted_iota(I32, (ts, LANES), 1)
    neg = -jnp.inf
    gl = jnp.where(lane < MOE_GROUPS, logits, neg)
    gmax = jnp.max(gl, axis=-1, keepdims=True)
    gidx = jnp.min(jnp.where(gl == gmax, lane, LANES), axis=-1, keepdims=True)
    p_group = 1.0 / jnp.sum(jnp.exp(gl - gmax), axis=-1, keepdims=True)
    e_lo = MOE_GROUPS + gidx * MOE_EPG
    el = jnp.where((lane >= e_lo) & (lane < e_lo + MOE_EPG), logits, neg)
    m1 = jnp.max(el, axis=-1, keepdims=True)
    i1 = jnp.min(jnp.where(el == m1, lane, LANES), axis=-1, keepdims=True)
    el2 = jnp.where(lane == i1, neg, el)
    m2 = jnp.max(el2, axis=-1, keepdims=True)
    i2 = jnp.min(jnp.where(el2 == m2, lane, LANES), axis=-1, keepdims=True)
    esum = jnp.sum(jnp.exp(el - m1), axis=-1, keepdims=True)
    p1 = 1.0 / esum
    p2 = jnp.exp(m2 - m1) / esum
    den = p1 + p2
    w1 = p_group * p1 / den
    w2 = p_group * p2 / den
    e1 = (i1 - MOE_GROUPS).astype(F32)
    e2 = (i2 - MOE_GROUPS).astype(F32)
    return jnp.where(lane == 0, e1, jnp.where(lane == 1, e2,
                     jnp.where(lane == 2, w1, jnp.where(lane == 3, w2, 0.0))))


def _mix_kernel(x_ref, ada_ref, win_ref, lbl_ref, gnw_ref, sglg_ref, sglb_ref, sgw_ref, sgbt_ref,
                wout_ref, ln1g_ref, ln1b_ref, wrh_ref, wrl_ref, br_ref,
                x1_ref, route_ref, *rest):
    h2p_refs = rest[:-5]
    st_ref, proj_ref, y_ref, g_ref, k_ref = rest[-5:]
    ts, d = x_ref.shape
    hw = HG_HEADS * HG_DK
    sgw = SG_GROUPS * SG_CH

    @pl.when(pl.program_id(1) == 0)
    def _():
        st_ref[...] = jnp.zeros_like(st_ref)

    x = x_ref[...]
    ada = ada_ref[...]
    sh1, sc1, g1, sh2, sc2 = (ada[i:i + 1] for i in range(5))
    h = (x * (1.0 + sc1) + sh1).astype(BF16)
    proj_ref[...] = _dot(h, win_ref[...])

    lbl = lbl_ref[...]
    slots = [lbl[i:i + 1] for i in range(lbl.shape[0])]
    mx = functools.reduce(jnp.maximum, slots)
    ex = [jnp.exp(s - mx) for s in slots]
    lb = ex[0] / functools.reduce(lambda a, b: a + b, ex)

    ci = lax.broadcasted_iota(I32, (HG_CHUNK, HG_CHUNK), 0)
    cj = lax.broadcasted_iota(I32, (HG_CHUNK, HG_CHUNK), 1)
    tril = (ci >= cj).astype(BF16)
    gnw = gnw_ref[...]

    def chunk_body(c, carry):
        _hgrn2_chunk(c, proj_ref, y_ref, st_ref, g_ref, k_ref, lb, gnw, tril)
        return carry

    lax.fori_loop(0, ts // HG_CHUNK, chunk_body, 0, unroll=True)

    pi = lax.broadcasted_iota(I32, (SG_CHUNK, SG_CHUNK), 0)
    pj = lax.broadcasted_iota(I32, (SG_CHUNK, SG_CHUNK), 1)
    sglg = sglg_ref[...]
    sglb = sglb_ref[...]
    for p in range(ts // SG_CHUNK):
        rows = slice(p * SG_CHUNK, (p + 1) * SG_CHUNK)
        z = _gelu_exact(proj_ref[rows, 4 * hw:4 * hw + 2 * sgw])
        u = z[:, :sgw]
        vn = _layernorm(z[:, sgw:], sglg, sglb).astype(BF16)
        for gi in range(SG_GROUPS):
            cs = slice(gi * SG_CH, (gi + 1) * SG_CH)
            wc = jnp.where(pi >= pj, sgw_ref[gi], 0.0).astype(BF16)
            mixed = _dot(wc, vn[:, cs]) + sgbt_ref[:, gi:gi + 1]
            y_ref[rows, hw + gi * SG_CH:hw + (gi + 1) * SG_CH] = (u[:, cs] * mixed).astype(BF16)

    y = _dot(y_ref[...], wout_ref[...])
    x1 = _layernorm(ALPHA * x + (1.0 + g1) * y, ln1g_ref[...], ln1b_ref[...])
    x1_ref[...] = x1
    h2 = x1 * (1.0 + sc2) + sh2

    h2_hi, h2_lo = _split_bf16(h2)
    packed = _pack_bf16_pairs(h2)
    for i, ref in enumerate(h2p_refs):
        ref[...] = packed[:, i * SC_ROW_WORDS:(i + 1) * SC_ROW_WORDS]

    wrh = wrh_ref[...]
    logits = _dot(h2_hi, wrh) + _dot(h2_lo, wrh) + _dot(h2_hi, wrl_ref[...]) + br_ref[...]
    route_ref[...] = _route(logits)


def _mix_call(x, ada, w_in_b, lb_logits, gnw, sglg, sglb, sg_w, sg_bt, w_out_b, ln1g, ln1b, wrh, wrl, br):
    bsz, s, d = x.shape
    ts = min(MIX_ROWS, s)
    ncol = w_in_b.shape[1]
    hw = HG_HEADS * HG_DK
    n_pieces = d // 2 // SC_ROW_WORDS
    full = lambda a: pl.BlockSpec(a.shape, lambda b, j: (0,) * a.ndim)
    return pl.pallas_call(
        _mix_kernel,
        grid=(bsz, s // ts),
        in_specs=[pl.BlockSpec((None, ts, d), lambda b, j: (b, j, 0)),
                  pl.BlockSpec((None,) + ada.shape[1:], lambda b, j: (b, 0, 0)),
                  full(w_in_b), full(lb_logits), full(gnw), full(sglg), full(sglb), full(sg_w),
                  full(sg_bt), full(w_out_b), full(ln1g), full(ln1b), full(wrh), full(wrl), full(br)],
        out_specs=[pl.BlockSpec((None, ts, d), lambda b, j: (b, j, 0)),
                   pl.BlockSpec((None, ts, LANES), lambda b, j: (b, j, 0))]
                  + [pl.BlockSpec((None, ts, SC_ROW_WORDS), lambda b, j: (b, j, 0))] * n_pieces,
        out_shape=[jax.ShapeDtypeStruct((bsz, s, d), F32),
                   jax.ShapeDtypeStruct((bsz, s, LANES), F32)]
                  + [jax.ShapeDtypeStruct((bsz, s, SC_ROW_WORDS), U32)] * n_pieces,
        scratch_shapes=[pltpu.VMEM((HG_HEADS, HG_DK, HG_DK), F32),
                        pltpu.VMEM((ts, ncol), F32),
                        pltpu.VMEM((ts, d), BF16),
                        pltpu.VMEM((HG_CHUNK, hw), F32),
                        pltpu.VMEM((HG_CHUNK, hw), F32)],
        compiler_params=pltpu.CompilerParams(
            dimension_semantics=("arbitrary", "arbitrary"), vmem_limit_bytes=VMEM_LIMIT),
        name="mix",
    )(x, ada, w_in_b, lb_logits, gnw, sglg, sglb, sg_w, sg_bt, w_out_b, ln1g, ln1b, wrh, wrl, br)


def _sort_kernel(route_ref, dest_ref, blk_ref, cnt_ref, pst_ref, carry_ref):
    ph = pl.program_id(0)
    i = pl.program_id(1)
    tk = route_ref.shape[0]
    nb = blk_ref.shape[0]
    lane = lax.broadcasted_iota(I32, (tk, LANES), 1)
    r = route_ref[...]
    oh1 = lane == r[:, 0:1].astype(I32)
    oh2 = lane == r[:, 1:2].astype(I32)
    hot = jnp.where(oh1 | oh2, 1.0, 0.0)
    colsum = jnp.sum(hot, axis=0, keepdims=True)

    @pl.when((ph == 0) & (i == 0))
    def _():
        cnt_ref[...] = jnp.zeros_like(cnt_ref)

    @pl.when(ph == 0)
    def _():
        cnt_ref[...] += colsum

    @pl.when((ph == 1) & (i == 0))
    def _():
        nblk = jnp.floor((cnt_ref[...] + (MOE_ROWS - 1)) * (1.0 / MOE_ROWS))
        ei = lax.broadcasted_iota(I32, (LANES, LANES), 0)
        ej = lax.broadcasted_iota(I32, (LANES, LANES), 1)
        upper = (ei < ej).astype(BF16)
        nblk8 = jnp.broadcast_to(nblk, (SUBLANES, LANES)).astype(BF16)
        pstart = _dot(nblk8, upper)[0:1]
        pend = pstart + nblk
        pst_ref[...] = pstart * float(MOE_ROWS)
        carry_ref[...] = jnp.zeros_like(carry_ref)
        jrow = lax.broadcasted_iota(I32, (nb, LANES), 0).astype(F32)
        lane_b = lax.broadcasted_iota(I32, (nb, LANES), 1)
        done = jnp.where((pend <= jrow) & (lane_b < MOE_EXPERTS), 1.0, 0.0)
        bexp = jnp.minimum(jnp.sum(done, axis=-1, keepdims=True), float(MOE_EXPERTS - 1))
        total = jnp.sum(jnp.where(lane_b == MOE_EXPERTS - 1, pend, 0.0), axis=-1, keepdims=True)
        blk_ref[...] = jnp.where(lane_b == 0, bexp, jnp.where(lane_b == 1, total, 0.0)).astype(I32)

    @pl.when(ph == 1)
    def _():
        ti = lax.broadcasted_iota(I32, (tk, tk), 0)
        tj = lax.broadcasted_iota(I32, (tk, tk), 1)
        before = (ti > tj).astype(BF16)
        base = _dot(before, hot.astype(BF16)) + carry_ref[...] + pst_ref[...]
        d1 = jnp.sum(jnp.where(oh1, base, 0.0), axis=-1, keepdims=True)
        d2 = jnp.sum(jnp.where(oh2, base, 0.0), axis=-1, keepdims=True)
        dest_ref[...] = jnp.where(lane == 0, d1, jnp.where(lane == 1, d2, 0.0)).astype(I32)
        carry_ref[...] += colsum


def _sort_call(route, n_blocks):
    t = route.shape[0]
    tk = min(SORT_ROWS, t)
    return pl.pallas_call(
        _sort_kernel,
        grid=(2, t // tk),
        in_specs=[pl.BlockSpec((tk, LANES), lambda p, i: (i, 0))],
        out_specs=[pl.BlockSpec((tk, LANES), lambda p, i: (i * p, 0)),
                   pl.BlockSpec((n_blocks, LANES), lambda p, i: (0, 0))],
        out_shape=[jax.ShapeDtypeStruct((t, LANES), I32),
                   jax.ShapeDtypeStruct((n_blocks, LANES), I32)],
        scratch_shapes=[pltpu.VMEM((1, LANES), F32), pltpu.VMEM((1, LANES), F32),
                        pltpu.VMEM((1, LANES), F32)],
        compiler_params=pltpu.CompilerParams(dimension_semantics=("arbitrary", "arbitrary")),
        name="sort",
    )(route)


def _sc_mesh():
    return plsc.VectorSubcoreMesh(core_axis_name="c", subcore_axis_name="s")


def _sc_scatter_rows2(rows, idx_a, idx_b, n_out):
    n, w = rows.shape

    @pl.kernel(out_type=jax.ShapeDtypeStruct((n_out, w), rows.dtype), mesh=_sc_mesh(), scratch_types=[])
    def scatter(x_hbm, ia_hbm, ib_hbm, o_hbm):
        def body(x_vmem, ia_vmem, ib_vmem):
            pltpu.sync_copy(x_vmem, o_hbm.at[ia_vmem.at[0]])
            pltpu.sync_copy(x_vmem, o_hbm.at[ib_vmem.at[0]])

        pltpu.emit_pipeline(
            body,
            grid=(n // SC_WINDOW,),
            in_specs=[pl.BlockSpec((SC_WINDOW, w), lambda i: (i, 0)),
                      pl.BlockSpec((1, SC_WINDOW), lambda i: (0, i)),
                      pl.BlockSpec((1, SC_WINDOW), lambda i: (0, i))],
            out_specs=[],
            core_axis_name=("c", "s"),
            dimension_semantics=(pltpu.PARALLEL,),
        )(x_hbm, ia_hbm, ib_hbm)

    return scatter(rows, idx_a.reshape(1, n), idx_b.reshape(1, n))


def _sc_gather_rows(src, idx):
    n = idx.shape[0]
    w = src.shape[1]

    @pl.kernel(out_type=jax.ShapeDtypeStruct((n, w), src.dtype), mesh=_sc_mesh(), scratch_types=[])
    def gather(x_hbm, i_hbm, o_hbm):
        def body(i_vmem, o_vmem):
            pltpu.sync_copy(x_hbm.at[i_vmem.at[0]], o_vmem)

        pltpu.emit_pipeline(
            body,
            grid=(n // SC_WINDOW,),
            in_specs=[pl.BlockSpec((1, SC_WINDOW), lambda i: (0, i))],
            out_specs=[pl.BlockSpec((SC_WINDOW, w), lambda i: (i, 0))],
            core_axis_name=("c", "s"),
            dimension_semantics=(pltpu.PARALLEL,),
        )(i_hbm, o_hbm)

    return gather(src, idx.reshape(1, n))


def _moe_kernel(n_in, be_ref, nused_ref, *refs):
    xb_refs = refs[:n_in]
    wup_ref, wdn_ref = refs[n_in:n_in + 2]
    y_refs = refs[n_in + 2:-2]
    wupb_ref, wdnb_ref = refs[-2:]
    j = pl.program_id(0)
    e = be_ref[j]
    e_prev = be_ref[jnp.maximum(j - 1, 0)]

    @pl.when((j == 0) | (e != e_prev))
    def _():
        wupb_ref[...] = wup_ref[...].astype(BF16)
        wdnb_ref[...] = wdn_ref[...].astype(BF16)

    @pl.when(j < nused_ref[0])
    def _():
        w = jnp.concatenate([r[...] for r in xb_refs], axis=1)
        xrow = _unpack_bf16_pairs(w).astype(BF16)
        gu = _dot(xrow, wupb_ref[...])
        gate = gu[:, :MOE_HIDDEN]
        act = (gate * _sigmoid(gate) * gu[:, MOE_HIDDEN:]).astype(BF16)
        y = _pack_bf16_pairs(_dot(act, wdnb_ref[...]))
        for i, ref in enumerate(y_refs):
            ref[...] = y[:, i * SC_ROW_WORDS:(i + 1) * SC_ROW_WORDS]


def _moe_call(block_expert, n_used, xb_pieces, w_up, w_down):
    n_rows = xb_pieces[0].shape[0]
    n_in = len(xb_pieces)
    d = 2 * n_in * SC_ROW_WORDS
    n_blocks = n_rows // MOE_ROWS
    hid2 = w_up.shape[2]
    hid = w_down.shape[1]
    last = lambda j, be, nu: (jnp.minimum(j, nu[0] - 1), 0)
    return pl.pallas_call(
        functools.partial(_moe_kernel, n_in),
        grid_spec=pltpu.PrefetchScalarGridSpec(
            num_scalar_prefetch=2,
            grid=(n_blocks,),
            in_specs=[pl.BlockSpec((MOE_ROWS, SC_ROW_WORDS), last)] * n_in + [
                      pl.BlockSpec((None, d, hid2), lambda j, be, nu: (be[j], 0, 0)),
                      pl.BlockSpec((None, hid, d), lambda j, be, nu: (be[j], 0, 0))],
            out_specs=[pl.BlockSpec((MOE_ROWS, SC_ROW_WORDS), last)] * n_in,
            scratch_shapes=[pltpu.VMEM((d, hid2), BF16), pltpu.VMEM((hid, d), BF16)]),
        out_shape=[jax.ShapeDtypeStruct((n_rows, SC_ROW_WORDS), U32)] * n_in,
        compiler_params=pltpu.CompilerParams(
            dimension_semantics=("arbitrary",), vmem_limit_bytes=VMEM_LIMIT),
        name="moe",
    )(block_expert, n_used, *xb_pieces, w_up, w_down)


def _combine_kernel(n_pieces, x1_ref, route_ref, ada_ref, g_ref, b_ref, *refs):
    o_ref = refs[-1]
    ya = _unpack_bf16_pairs(jnp.concatenate([r[...] for r in refs[:n_pieces]], axis=1))
    yb = _unpack_bf16_pairs(jnp.concatenate([r[...] for r in refs[n_pieces:2 * n_pieces]], axis=1))
    r = route_ref[...]
    m = ya * r[:, 2:3] + yb * r[:, 3:4]
    g2 = ada_ref[5:6]
    o_ref[...] = _layernorm(ALPHA * x1_ref[...] + (1.0 + g2) * m, g_ref[...], b_ref[...])


def _combine_call(x1, y_pieces, route, ada, ln2g, ln2b):
    bsz, s, d = x1.shape
    ts = min(OUT_ROWS, s)
    nj = s // ts
    n_pieces = len(y_pieces)
    slot = lambda k: pl.BlockSpec((ts, SC_ROW_WORDS), lambda b, j: (k * bsz * nj + b * nj + j, 0))
    return pl.pallas_call(
        functools.partial(_combine_kernel, n_pieces),
        grid=(bsz, nj),
        in_specs=[pl.BlockSpec((None, ts, d), lambda b, j: (b, j, 0)),
                  pl.BlockSpec((None, ts, LANES), lambda b, j: (b, j, 0)),
                  pl.BlockSpec((None,) + ada.shape[1:], lambda b, j: (b, 0, 0)),
                  pl.BlockSpec((1, d), lambda b, j: (0, 0)),
                  pl.BlockSpec((1, d), lambda b, j: (0, 0))]
                 + [slot(0)] * n_pieces + [slot(1)] * n_pieces,
        out_specs=pl.BlockSpec((None, ts, d), lambda b, j: (b, j, 0)),
        out_shape=jax.ShapeDtypeStruct((bsz, s, d), F32),
        compiler_params=pltpu.CompilerParams(dimension_semantics=("arbitrary", "arbitrary")),
        name="combine",
    )(x1, route, ada, ln2g, ln2b, *y_pieces, *y_pieces)


def kernel(x, c, w_ada, b_ada, w_in, lb_logits, hg_norm_w, sg_ln_g, sg_ln_b, sg_w, sg_b, w_out, ln1_g, ln1_b, router_group_w, router_group_b, router_expert_w, router_expert_b, w_up, w_down, ln2_g, ln2_b):
    assert w_in.shape[0] == DEPTH
    bsz, s, d = x.shape
    t = bsz * s
    l = 0

    ada = _ada_call(c, w_ada[l], b_ada[l])

    wr = jnp.concatenate(
        [router_group_w[l], router_expert_w[l].transpose(1, 0, 2).reshape(d, MOE_EXPERTS)], axis=1)
    wr = jnp.pad(wr, ((0, 0), (0, LANES - wr.shape[1])))
    br = jnp.concatenate([router_group_b[l], router_expert_b[l].reshape(MOE_EXPERTS)])
    br = jnp.pad(br, (0, LANES - br.shape[0])).reshape(1, LANES)
    wrh, wrl = _split_bf16(wr)

    x1, route, *h2p = _mix_call(
        x, ada, w_in[l].astype(BF16), lb_logits, hg_norm_w[l].reshape(1, -1),
        sg_ln_g[l].reshape(1, -1), sg_ln_b[l].reshape(1, -1), sg_w[l], sg_b[l].T,
        w_out[l].astype(BF16), ln1_g[l].reshape(1, d), ln1_b[l].reshape(1, d), wrh, wrl, br)

    n_blocks = -(-(2 * t) // MOE_ROWS) + MOE_EXPERTS
    n_rows = n_blocks * MOE_ROWS
    dest, blk = _sort_call(route.reshape(t, LANES), n_blocks)
    d0, d1 = dest[:, 0], dest[:, 1]

    xb = [_sc_scatter_rows2(p.reshape(t, SC_ROW_WORDS), d0, d1, n_rows) for p in h2p]
    yb = _moe_call(blk[:, 0], blk[0:1, 1].reshape(1), xb, w_up[l], w_down[l])
    d01 = jnp.concatenate([d0, d1])
    y2 = [_sc_gather_rows(p, d01) for p in yb]
    return _combine_call(x1, y2, route, ada, ln2_g[l].reshape(1, d), ln2_b[l].reshape(1, d))
```

```python
import functools

import jax
import jax.numpy as jnp
from jax import lax
from jax.experimental import pallas as pl
from jax.experimental.pallas import tpu as pltpu
from jax.experimental.pallas import tpu_sc as plsc

F32 = jnp.float32
BF16 = jnp.bfloat16
I32 = jnp.int32
U32 = jnp.uint32

HG_HEADS = 4
HG_DK = 128
HG_CHUNK = 64
HG_SUB = 8
SG_GROUPS = 4
SG_CH = 128
SG_CHUNK = 128
MOE_GROUPS = 4
MOE_EPG = 8
MOE_EXPERTS = MOE_GROUPS * MOE_EPG
MOE_HIDDEN = 512
DEPTH = 1
ALPHA = (2.0 * DEPTH) ** 0.25
LN_EPS = 1e-5
RMS_EPS = 1e-6
LOG2E = 1.4426950408889634

LANES = 128
SUBLANES = 8
SC_WINDOW = 128
SC_ROW_WORDS = 256

MIX_ROWS = 512
PROJ_PIECE = 256
SORT_ROWS = 1024
MOE_ROWS = 256
OUT_ROWS = 512
VMEM_LIMIT = 52 * 1024 * 1024


def _dot(a, b):
    return jnp.dot(a, b, preferred_element_type=F32)


def _dot_nt(a, b):
    return lax.dot_general(a, b, (((1,), (1,)), ((), ())), preferred_element_type=F32)


def _dot_tn(a, b):
    return lax.dot_general(a, b, (((0,), (0,)), ((), ())), preferred_element_type=F32)


def _sigmoid(x):
    return jax.nn.sigmoid(x)


def _gelu_exact(x):
    return 0.5 * x * (1.0 + lax.erf(x * (2.0 ** -0.5)))


def _layernorm(x, g, b):
    mu = jnp.mean(x, axis=-1, keepdims=True)
    xc = x - mu
    var = jnp.mean(xc * xc, axis=-1, keepdims=True)
    return xc * lax.rsqrt(var + LN_EPS) * g + b


def _pack_bf16_pairs(x):
    n = x.shape[1]
    bits = lax.bitcast_convert_type(x.astype(BF16).astype(F32), U32)
    return bits[:, n // 2:] | (bits[:, :n // 2] >> 16)


def _unpack_bf16_pairs(w):
    lo = lax.bitcast_convert_type(w << 16, F32)
    hi = lax.bitcast_convert_type(w & jnp.uint32(0xFFFF0000), F32)
    return jnp.concatenate([lo, hi], axis=1)


def _split_bf16(x):
    hi = x.astype(BF16)
    lo = (x - hi.astype(F32)).astype(BF16)
    return hi, lo


def _ada_kernel(c_ref, w_ref, b_ref, o_ref):
    c = c_ref[...]
    ca = c * _sigmoid(c)
    o_ref[...] = jnp.dot(ca, w_ref[...], preferred_element_type=F32,
                         precision=lax.Precision.HIGHEST) + b_ref[...]


def _ada_call(c, w_ada, b_ada):
    bsz, d = c.shape
    n = w_ada.shape[1]
    rows = -(-bsz // SUBLANES) * SUBLANES
    c_pad = jnp.pad(c, ((0, rows - bsz), (0, 0)))
    out = pl.pallas_call(
        _ada_kernel,
        grid=(n // d,),
        in_specs=[pl.BlockSpec((rows, d), lambda i: (0, 0)),
                  pl.BlockSpec((d, d), lambda i: (0, i)),
                  pl.BlockSpec((1, d), lambda i: (0, i))],
        out_specs=pl.BlockSpec((rows, d), lambda i: (0, i)),
        out_shape=jax.ShapeDtypeStruct((rows, n), F32),
        name="ada",
    )(c_pad, w_ada, b_ada.reshape(1, n))
    return out[:bsz].reshape(bsz, n // d, d)


def _hgrn2_chunk(c, proj_ref, y_ref, st_ref, g_ref, h_ref, lb, gnw, tril, overlap):
    hw = HG_HEADS * HG_DK
    cl = HG_CHUNK
    rows = pl.ds(c * cl, cl)
    qz = proj_ref[rows, 0:hw]
    fz = proj_ref[rows, hw:2 * hw]
    v = proj_ref[rows, 2 * hw:3 * hw]
    og = proj_ref[rows, 3 * hw:4 * hw]

    q = qz * _sigmoid(qz)
    f = lb + (1.0 - lb) * _sigmoid(fz)
    lf = jnp.log(f)
    k = 1.0 - f
    lf_hi, lf_lo = _split_bf16(lf)
    g = (_dot(tril, lf_hi) + _dot(tril, lf_lo)) * LOG2E
    glast = g[cl - 1:cl, :]
    qg = (q * jnp.exp2(g)).astype(BF16)
    kd = (k * jnp.exp2(glast - g)).astype(BF16)
    vb = v.astype(BF16)
    g_ref[...] = g
    h_ref[...] = g - jnp.log(k) * LOG2E

    n_sub = cl // HG_SUB
    lane_c = lax.broadcasted_iota(I32, (HG_SUB, cl), 1)
    trow = lax.broadcasted_iota(I32, (HG_SUB, cl), 0)

    heads = [slice(hd * HG_DK, (hd + 1) * HG_DK) for hd in range(HG_HEADS)]

    o_state = []
    acc_all = []
    for hd, cs in enumerate(heads):
        st = st_ref[hd]
        o_state.append(_dot_nt(qg[:, cs], st.astype(BF16)))
        st_ref[hd] = st * jnp.exp2(glast[:, cs]) + _dot_tn(vb[:, cs], kd[:, cs])
        gh = g[:, cs]
        kh = k[:, cs]
        qh = q[:, cs]
        acc = [None] * n_sub
        w = cl // 2
        while w >= HG_SUB:
            for p in range(0, cl, 2 * w):
                ref = g_ref[p + w - 1:p + w, cs]
                qt = (qh[p + w:p + 2 * w] * jnp.exp2(gh[p + w:p + 2 * w] - ref)).astype(BF16)
                kt = (kh[p:p + w] * jnp.exp2(ref - gh[p:p + w])).astype(BF16)
                pieces = [kt]
                if p:
                    pieces.insert(0, jnp.zeros((p, HG_DK), BF16))
                if cl - p - w:
                    pieces.append(jnp.zeros((cl - p - w, HG_DK), BF16))
                blk = _dot_nt(qt, jnp.concatenate(pieces, axis=0))
                for r in range(w // HG_SUB):
                    i = (p + w) // HG_SUB + r
                    part = blk[r * HG_SUB:(r + 1) * HG_SUB]
                    acc[i] = part if acc[i] is None else acc[i] + part
            w //= 2
        acc_all.append(acc)

    overlap()

    for hd, cs in enumerate(heads):
        gh = g[:, cs]
        qh = q[:, cs]
        a_rows = []
        for i in range(n_sub):
            b0 = i * HG_SUB
            gb = gh[b0:b0 + HG_SUB]
            qb = qh[b0:b0 + HG_SUB]
            a = jnp.zeros((HG_SUB, cl), F32) if acc_all[hd][i] is None else acc_all[hd][i]
            for s in range(HG_SUB):
                hs = h_ref[b0 + s:b0 + s + 1, cs]
                col = jnp.sum(qb * jnp.exp2(gb - hs), axis=-1, keepdims=True)
                a = jnp.where(lane_c == b0 + s, col, a)
            a_rows.append(jnp.where(lane_c <= b0 + trow, a, 0.0))
        amat = jnp.concatenate(a_rows, axis=0).astype(BF16)
        o = o_state[hd] + _dot(amat, vb[:, cs])
        ms = jnp.mean(o * o, axis=-1, keepdims=True)
        ogh = og[:, cs]
        ya = o * lax.rsqrt(ms + RMS_EPS) * gnw * (ogh * _sigmoid(ogh))
        y_ref[rows, cs] = ya.astype(BF16)


def _route(logits):
    ts = logits.shape[0]
    lane = lax.broadcasted_iota(I32, (ts, LANES), 1)
    neg = -jnp.inf
    gl = jnp.where(lane < MOE_GROUPS, logits, neg)
    gmax = jnp.max(gl, axis=-1, keepdims=True)
    gidx = jnp.min(jnp.where(gl == gmax, lane, LANES), axis=-1, keepdims=True)
    p_group = 1.0 / jnp.sum(jnp.exp(gl - gmax), axis=-1, keepdims=True)
    e_lo = MOE_GROUPS + gidx * MOE_EPG
    el = jnp.where((lane >= e_lo) & (lane < e_lo + MOE_EPG), logits, neg)
    m1 = jnp.max(el, axis=-1, keepdims=True)
    i1 = jnp.min(jnp.where(el == m1, lane, LANES), axis=-1, keepdims=True)
    el2 = jnp.where(lane == i1, neg, el)
    m2 = jnp.max(el2, axis=-1, keepdims=True)
    i2 = jnp.min(jnp.where(el2 == m2, lane, LANES), axis=-1, keepdims=True)
    esum = jnp.sum(jnp.exp(el - m1), axis=-1, keepdims=True)
    p1 = 1.0 / esum
    p2 = jnp.exp(m2 - m1) / esum
    den = p1 + p2
    w1 = p_group * p1 / den
    w2 = p_group * p2 / den
    e1 = (i1 - MOE_GROUPS).astype(F32)
    e2 = (i2 - MOE_GROUPS).astype(F32)
    return jnp.where(lane == 0, e1, jnp.where(lane == 1, e2,
                     jnp.where(lane == 2, w1, jnp.where(lane == 3, w2, 0.0))))


def _mix_kernel(nt, x_ref, xn_ref, ada_ref, adan_ref, win_ref, lbl_ref, gnw_ref, sglg_ref, sglb_ref,
                sgw_ref, sgbt_ref, wout_ref, ln1g_ref, ln1b_ref, wrh_ref, wrl_ref, br_ref,
                x1_ref, route_ref, *rest):
    h2p_refs = rest[:-6]
    st_ref, proj_ref, projn_ref, y_ref, g_ref, k_ref = rest[-6:]
    ts, d = x_ref.shape
    hw = HG_HEADS * HG_DK
    sgw = SG_GROUPS * SG_CH
    ncol = proj_ref.shape[1]
    step = pl.program_id(0)

    def modulated(xr, adar):
        a = adar[...]
        return (xr[...] * (1.0 + a[1:2]) + a[0:1]).astype(BF16)

    @pl.when(step % nt == 0)
    def _():
        st_ref[...] = jnp.zeros_like(st_ref)

    @pl.when(step == 0)
    def _():
        proj_ref[...] = _dot(modulated(x_ref, ada_ref), win_ref[...])

    x = x_ref[...]
    ada = ada_ref[...]
    g1, sh2, sc2 = ada[2:3], ada[3:4], ada[4:5]
    hn = modulated(xn_ref, adan_ref)

    lbl = lbl_ref[...]
    slots = [lbl[i:i + 1] for i in range(lbl.shape[0])]
    mx = functools.reduce(jnp.maximum, slots)
    ex = [jnp.exp(s - mx) for s in slots]
    lb = ex[0] / functools.reduce(lambda a, b: a + b, ex)

    ci = lax.broadcasted_iota(I32, (HG_CHUNK, HG_CHUNK), 0)
    cj = lax.broadcasted_iota(I32, (HG_CHUNK, HG_CHUNK), 1)
    tril = (ci >= cj).astype(BF16)
    gnw = gnw_ref[...]

    n_chunks = ts // HG_CHUNK
    n_sg = ts // SG_CHUNK
    n_pp = ncol // PROJ_PIECE
    n_tail = max(n_pp - n_chunks, 0)
    slot_of = [p if p < n_pp - n_tail else n_chunks + (2 * (p - (n_pp - n_tail))) // max(n_tail, 1)
               for p in range(n_pp)]

    def next_proj_pieces(slot):
        for p in range(n_pp):
            if min(slot_of[p], n_chunks + 1) == slot:
                cols = slice(p * PROJ_PIECE, (p + 1) * PROJ_PIECE)
                projn_ref[:, cols] = _dot(hn, win_ref[:, cols])

    for c in range(n_chunks):
        _hgrn2_chunk(c, proj_ref, y_ref, st_ref, g_ref, k_ref, lb, gnw, tril,
                     functools.partial(next_proj_pieces, c))

    pi = lax.broadcasted_iota(I32, (SG_CHUNK, SG_CHUNK), 0)
    pj = lax.broadcasted_iota(I32, (SG_CHUNK, SG_CHUNK), 1)
    sglg = sglg_ref[...]
    sglb = sglb_ref[...]
    for p in range(n_sg):
        rows = slice(p * SG_CHUNK, (p + 1) * SG_CHUNK)
        z = _gelu_exact(proj_ref[rows, 4 * hw:4 * hw + 2 * sgw])
        u = z[:, :sgw]
        vn = _layernorm(z[:, sgw:], sglg, sglb).astype(BF16)
        for gi in range(SG_GROUPS):
            cs = slice(gi * SG_CH, (gi + 1) * SG_CH)
            wc = jnp.where(pi >= pj, sgw_ref[gi], 0.0).astype(BF16)
            mixed = _dot(wc, vn[:, cs]) + sgbt_ref[:, gi:gi + 1]
            y_ref[rows, hw + gi * SG_CH:hw + (gi + 1) * SG_CH] = (u[:, cs] * mixed).astype(BF16)

    y = _dot(y_ref[...], wout_ref[...])
    next_proj_pieces(n_chunks)
    x1 = _layernorm(ALPHA * x + (1.0 + g1) * y, ln1g_ref[...], ln1b_ref[...])
    x1_ref[...] = x1
    h2 = x1 * (1.0 + sc2) + sh2

    h2_hi, h2_lo = _split_bf16(h2)
    packed = _pack_bf16_pairs(h2)
    for i, ref in enumerate(h2p_refs):
        ref[...] = packed[:, i * SC_ROW_WORDS:(i + 1) * SC_ROW_WORDS]

    wrh = wrh_ref[...]
    logits = _dot(h2_hi, wrh) + _dot(h2_lo, wrh) + _dot(h2_hi, wrl_ref[...]) + br_ref[...]
    next_proj_pieces(n_chunks + 1)
    route_ref[...] = _route(logits)
    proj_ref[...] = projn_ref[...]


def _mix_call(x, ada, w_in_b, lb_logits, gnw, sglg, sglb, sg_w, sg_bt, w_out_b, ln1g, ln1b, wrh, wrl, br):
    bsz, s, d = x.shape
    ts = min(MIX_ROWS, s)
    nt = s // ts
    n_steps = bsz * nt
    ncol = w_in_b.shape[1]
    hw = HG_HEADS * HG_DK
    n_pieces = d // 2 // SC_ROW_WORDS
    const = lambda a: pl.BlockSpec(a.shape, lambda i: (0,) * a.ndim, pipeline_mode=pl.Buffered(1))
    cur = lambda i: (i // nt, i % nt, 0)
    nxt = lambda i: cur(jnp.minimum(i + 1, n_steps - 1))
    return pl.pallas_call(
        functools.partial(_mix_kernel, nt),
        grid=(n_steps,),
        in_specs=[pl.BlockSpec((None, ts, d), cur),
                  pl.BlockSpec((None, ts, d), nxt),
                  pl.BlockSpec((None,) + ada.shape[1:], lambda i: (cur(i)[0], 0, 0)),
                  pl.BlockSpec((None,) + ada.shape[1:], lambda i: (nxt(i)[0], 0, 0)),
                  const(w_in_b), const(lb_logits), const(gnw), const(sglg), const(sglb), const(sg_w),
                  const(sg_bt), const(w_out_b), const(ln1g), const(ln1b), const(wrh), const(wrl),
                  const(br)],
        out_specs=[pl.BlockSpec((None, ts, d), cur),
                   pl.BlockSpec((None, ts, LANES), cur)]
                  + [pl.BlockSpec((None, ts, SC_ROW_WORDS), cur)] * n_pieces,
        out_shape=[jax.ShapeDtypeStruct((bsz, s, d), F32),
                   jax.ShapeDtypeStruct((bsz, s, LANES), F32)]
                  + [jax.ShapeDtypeStruct((bsz, s, SC_ROW_WORDS), U32)] * n_pieces,
        scratch_shapes=[pltpu.VMEM((HG_HEADS, HG_DK, HG_DK), F32),
                        pltpu.VMEM((ts, ncol), F32),
                        pltpu.VMEM((ts, ncol), F32),
                        pltpu.VMEM((ts, d), BF16),
                        pltpu.VMEM((HG_CHUNK, hw), F32),
                        pltpu.VMEM((HG_CHUNK, hw), F32)],
        compiler_params=pltpu.CompilerParams(
            dimension_semantics=("arbitrary",), vmem_limit_bytes=VMEM_LIMIT),
        name="mix",
    )(x, x, ada, ada, w_in_b, lb_logits, gnw, sglg, sglb, sg_w, sg_bt, w_out_b, ln1g, ln1b, wrh, wrl, br)


def _sort_kernel(route_ref, dest_ref, blk_ref, cnt_ref, pst_ref, carry_ref):
    ph = pl.program_id(0)
    i = pl.program_id(1)
    tk = route_ref.shape[0]
    nb = blk_ref.shape[0]
    lane = lax.broadcasted_iota(I32, (tk, LANES), 1)
    r = route_ref[...]
    oh1 = lane == r[:, 0:1].astype(I32)
    oh2 = lane == r[:, 1:2].astype(I32)
    hot = jnp.where(oh1 | oh2, 1.0, 0.0)
    colsum = jnp.sum(hot, axis=0, keepdims=True)

    @pl.when((ph == 0) & (i == 0))
    def _():
        cnt_ref[...] = jnp.zeros_like(cnt_ref)

    @pl.when(ph == 0)
    def _():
        cnt_ref[...] += colsum

    @pl.when((ph == 1) & (i == 0))
    def _():
        nblk = jnp.floor((cnt_ref[...] + (MOE_ROWS - 1)) * (1.0 / MOE_ROWS))
        ei = lax.broadcasted_iota(I32, (LANES, LANES), 0)
        ej = lax.broadcasted_iota(I32, (LANES, LANES), 1)
        upper = (ei < ej).astype(BF16)
        nblk8 = jnp.broadcast_to(nblk, (SUBLANES, LANES)).astype(BF16)
        pstart = _dot(nblk8, upper)[0:1]
        pend = pstart + nblk
        pst_ref[...] = pstart * float(MOE_ROWS)
        carry_ref[...] = jnp.zeros_like(carry_ref)
        jrow = lax.broadcasted_iota(I32, (nb, LANES), 0).astype(F32)
        lane_b = lax.broadcasted_iota(I32, (nb, LANES), 1)
        done = jnp.where((pend <= jrow) & (lane_b < MOE_EXPERTS), 1.0, 0.0)
        bexp = jnp.minimum(jnp.sum(done, axis=-1, keepdims=True), float(MOE_EXPERTS - 1))
        total = jnp.sum(jnp.where(lane_b == MOE_EXPERTS - 1, pend, 0.0), axis=-1, keepdims=True)
        blk_ref[...] = jnp.where(lane_b == 0, bexp, jnp.where(lane_b == 1, total, 0.0)).astype(I32)

    @pl.when(ph == 1)
    def _():
        ti = lax.broadcasted_iota(I32, (tk, tk), 0)
        tj = lax.broadcasted_iota(I32, (tk, tk), 1)
        before = (ti > tj).astype(BF16)
        base = _dot(before, hot.astype(BF16)) + carry_ref[...] + pst_ref[...]
        d1 = jnp.sum(jnp.where(oh1, base, 0.0), axis=-1, keepdims=True)
        d2 = jnp.sum(jnp.where(oh2, base, 0.0), axis=-1, keepdims=True)
        dest_ref[...] = jnp.where(lane == 0, d1, jnp.where(lane == 1, d2, 0.0)).astype(I32)
        carry_ref[...] += colsum


def _sort_call(route, n_blocks):
    t = route.shape[0]
    tk = min(SORT_ROWS, t)
    return pl.pallas_call(
        _sort_kernel,
        grid=(2, t // tk),
        in_specs=[pl.BlockSpec((tk, LANES), lambda p, i: (i, 0))],
        out_specs=[pl.BlockSpec((tk, LANES), lambda p, i: (i * p, 0)),
                   pl.BlockSpec((n_blocks, LANES), lambda p, i: (0, 0))],
        out_shape=[jax.ShapeDtypeStruct((t, LANES), I32),
                   jax.ShapeDtypeStruct((n_blocks, LANES), I32)],
        scratch_shapes=[pltpu.VMEM((1, LANES), F32), pltpu.VMEM((1, LANES), F32),
                        pltpu.VMEM((1, LANES), F32)],
        compiler_params=pltpu.CompilerParams(dimension_semantics=("arbitrary", "arbitrary")),
        name="sort",
    )(route)


def _sc_mesh():
    return plsc.VectorSubcoreMesh(core_axis_name="c", subcore_axis_name="s")


def _sc_scatter_rows2(rows, idx_a, idx_b, n_out):
    n, w = rows.shape

    @pl.kernel(out_type=jax.ShapeDtypeStruct((n_out, w), rows.dtype), mesh=_sc_mesh(), scratch_types=[])
    def scatter(x_hbm, ia_hbm, ib_hbm, o_hbm):
        def body(x_vmem, ia_vmem, ib_vmem):
            pltpu.sync_copy(x_vmem, o_hbm.at[ia_vmem.at[0]])
            pltpu.sync_copy(x_vmem, o_hbm.at[ib_vmem.at[0]])

        pltpu.emit_pipeline(
            body,
            grid=(n // SC_WINDOW,),
            in_specs=[pl.BlockSpec((SC_WINDOW, w), lambda i: (i, 0)),
                      pl.BlockSpec((1, SC_WINDOW), lambda i: (0, i)),
                      pl.BlockSpec((1, SC_WINDOW), lambda i: (0, i))],
            out_specs=[],
            core_axis_name=("c", "s"),
            dimension_semantics=(pltpu.PARALLEL,),
        )(x_hbm, ia_hbm, ib_hbm)

    return scatter(rows, idx_a.reshape(1, n), idx_b.reshape(1, n))


def _sc_gather_rows(src, idx):
    n = idx.shape[0]
    w = src.shape[1]

    @pl.kernel(out_type=jax.ShapeDtypeStruct((n, w), src.dtype), mesh=_sc_mesh(), scratch_types=[])
    def gather(x_hbm, i_hbm, o_hbm):
        def body(i_vmem, o_vmem):
            pltpu.sync_copy(x_hbm.at[i_vmem.at[0]], o_vmem)

        pltpu.emit_pipeline(
            body,
            grid=(n // SC_WINDOW,),
            in_specs=[pl.BlockSpec((1, SC_WINDOW), lambda i: (0, i))],
            out_specs=[pl.BlockSpec((SC_WINDOW, w), lambda i: (i, 0))],
            core_axis_name=("c", "s"),
            dimension_semantics=(pltpu.PARALLEL,),
        )(i_hbm, o_hbm)

    return gather(src, idx.reshape(1, n))


def _moe_kernel(n_in, be_ref, nused_ref, *refs):
    xb_refs = refs[:n_in]
    wup_ref, wdn_ref = refs[n_in:n_in + 2]
    y_refs = refs[n_in + 2:-2]
    wupb_ref, wdnb_ref = refs[-2:]
    j = pl.program_id(0)
    e = be_ref[j]
    e_prev = be_ref[jnp.maximum(j - 1, 0)]

    @pl.when((j == 0) | (e != e_prev))
    def _():
        wupb_ref[...] = wup_ref[...].astype(BF16)
        wdnb_ref[...] = wdn_ref[...].astype(BF16)

    @pl.when(j < nused_ref[0])
    def _():
        w = jnp.concatenate([r[...] for r in xb_refs], axis=1)
        xrow = _unpack_bf16_pairs(w).astype(BF16)
        gu = _dot(xrow, wupb_ref[...])
        gate = gu[:, :MOE_HIDDEN]
        act = (gate * _sigmoid(gate) * gu[:, MOE_HIDDEN:]).astype(BF16)
        y = _pack_bf16_pairs(_dot(act, wdnb_ref[...]))
        for i, ref in enumerate(y_refs):
            ref[...] = y[:, i * SC_ROW_WORDS:(i + 1) * SC_ROW_WORDS]


def _moe_call(block_expert, n_used, xb_pieces, w_up, w_down):
    n_rows = xb_pieces[0].shape[0]
    n_in = len(xb_pieces)
    d = 2 * n_in * SC_ROW_WORDS
    n_blocks = n_rows // MOE_ROWS
    hid2 = w_up.shape[2]
    hid = w_down.shape[1]
    last = lambda j, be, nu: (jnp.minimum(j, nu[0] - 1), 0)
    return pl.pallas_call(
        functools.partial(_moe_kernel, n_in),
        grid_spec=pltpu.PrefetchScalarGridSpec(
            num_scalar_prefetch=2,
            grid=(n_blocks,),
            in_specs=[pl.BlockSpec((MOE_ROWS, SC_ROW_WORDS), last)] * n_in + [
                      pl.BlockSpec((None, d, hid2), lambda j, be, nu: (be[j], 0, 0)),
                      pl.BlockSpec((None, hid, d), lambda j, be, nu: (be[j], 0, 0))],
            out_specs=[pl.BlockSpec((MOE_ROWS, SC_ROW_WORDS), last)] * n_in,
            scratch_shapes=[pltpu.VMEM((d, hid2), BF16), pltpu.VMEM((hid, d), BF16)]),
        out_shape=[jax.ShapeDtypeStruct((n_rows, SC_ROW_WORDS), U32)] * n_in,
        compiler_params=pltpu.CompilerParams(
            dimension_semantics=("arbitrary",), vmem_limit_bytes=VMEM_LIMIT),
        name="moe",
    )(block_expert, n_used, *xb_pieces, w_up, w_down)


def _combine_kernel(n_pieces, x1_ref, route_ref, ada_ref, g_ref, b_ref, *refs):
    o_ref = refs[-1]
    ya = _unpack_bf16_pairs(jnp.concatenate([r[...] for r in refs[:n_pieces]], axis=1))
    yb = _unpack_bf16_pairs(jnp.concatenate([r[...] for r in refs[n_pieces:2 * n_pieces]], axis=1))
    r = route_ref[...]
    m = ya * r[:, 2:3] + yb * r[:, 3:4]
    g2 = ada_ref[5:6]
    o_ref[...] = _layernorm(ALPHA * x1_ref[...] + (1.0 + g2) * m, g_ref[...], b_ref[...])


def _combine_call(x1, y_pieces, route, ada, ln2g, ln2b):
    bsz, s, d = x1.shape
    ts = min(OUT_ROWS, s)
    nj = s // ts
    n_pieces = len(y_pieces)
    slot = lambda k: pl.BlockSpec((ts, SC_ROW_WORDS), lambda b, j: (k * bsz * nj + b * nj + j, 0))
    return pl.pallas_call(
        functools.partial(_combine_kernel, n_pieces),
        grid=(bsz, nj),
        in_specs=[pl.BlockSpec((None, ts, d), lambda b, j: (b, j, 0)),
                  pl.BlockSpec((None, ts, LANES), lambda b, j: (b, j, 0)),
                  pl.BlockSpec((None,) + ada.shape[1:], lambda b, j: (b, 0, 0)),
                  pl.BlockSpec((1, d), lambda b, j: (0, 0)),
                  pl.BlockSpec((1, d), lambda b, j: (0, 0))]
                 + [slot(0)] * n_pieces + [slot(1)] * n_pieces,
        out_specs=pl.BlockSpec((None, ts, d), lambda b, j: (b, j, 0)),
        out_shape=jax.ShapeDtypeStruct((bsz, s, d), F32),
        compiler_params=pltpu.CompilerParams(dimension_semantics=("arbitrary", "arbitrary")),
        name="combine",
    )(x1, route, ada, ln2g, ln2b, *y_pieces, *y_pieces)


def kernel(x, c, w_ada, b_ada, w_in, lb_logits, hg_norm_w, sg_ln_g, sg_ln_b, sg_w, sg_b, w_out, ln1_g, ln1_b, router_group_w, router_group_b, router_expert_w, router_expert_b, w_up, w_down, ln2_g, ln2_b):
    assert w_in.shape[0] == DEPTH
    bsz, s, d = x.shape
    t = bsz * s
    l = 0

    ada = _ada_call(c, w_ada[l], b_ada[l])

    wr = jnp.concatenate(
        [router_group_w[l], router_expert_w[l].transpose(1, 0, 2).reshape(d, MOE_EXPERTS)], axis=1)
    wr = jnp.pad(wr, ((0, 0), (0, LANES - wr.shape[1])))
    br = jnp.concatenate([router_group_b[l], router_expert_b[l].reshape(MOE_EXPERTS)])
    br = jnp.pad(br, (0, LANES - br.shape[0])).reshape(1, LANES)
    wrh, wrl = _split_bf16(wr)

    x1, route, *h2p = _mix_call(
        x, ada, w_in[l].astype(BF16), lb_logits, hg_norm_w[l].reshape(1, -1),
        sg_ln_g[l].reshape(1, -1), sg_ln_b[l].reshape(1, -1), sg_w[l], sg_b[l].T,
        w_out[l].astype(BF16), ln1_g[l].reshape(1, d), ln1_b[l].reshape(1, d), wrh, wrl, br)

    n_blocks = -(-(2 * t) // MOE_ROWS) + MOE_EXPERTS
    n_rows = n_blocks * MOE_ROWS
    dest, blk = _sort_call(route.reshape(t, LANES), n_blocks)
    d0, d1 = dest[:, 0], dest[:, 1]

    xb = [_sc_scatter_rows2(p.reshape(t, SC_ROW_WORDS), d0, d1, n_rows) for p in h2p]
    yb = _moe_call(blk[:, 0], blk[0:1, 1].reshape(1), xb, w_up[l], w_down[l])
    d01 = jnp.concatenate([d0, d1])
    y2 = [_sc_gather_rows(p, d01) for p in yb]
    return _combine_call(x1, y2, route, ada, ln2_g[l].reshape(1, d), ln2_b[l].reshape(1, d))
```

```python
import functools

import jax
import jax.numpy as jnp
from jax import lax
from jax.experimental import pallas as pl
from jax.experimental.pallas import tpu as pltpu
from jax.experimental.pallas import tpu_sc as plsc

F32 = jnp.float32
BF16 = jnp.bfloat16
I32 = jnp.int32
U32 = jnp.uint32

HG_HEADS = 4
HG_DK = 128
HG_CHUNK = 64
HG_SUB = 8
SG_GROUPS = 4
SG_CH = 128
SG_CHUNK = 128
MOE_GROUPS = 4
MOE_EPG = 8
MOE_EXPERTS = MOE_GROUPS * MOE_EPG
MOE_HIDDEN = 512
DEPTH = 1
ALPHA = (2.0 * DEPTH) ** 0.25
LN_EPS = 1e-5
RMS_EPS = 1e-6
LOG2E = 1.4426950408889634

LANES = 128
SUBLANES = 8
SC_WINDOW = 128
SC_ROW_WORDS = 256

MIX_ROWS = 512
PROJ_PIECE = 256
SORT_ROWS = 1024
MOE_ROWS = 256
OUT_ROWS = 512
VMEM_LIMIT = 52 * 1024 * 1024


def _dot(a, b):
    return jnp.dot(a, b, preferred_element_type=F32)


def _dot_nt(a, b):
    return lax.dot_general(a, b, (((1,), (1,)), ((), ())), preferred_element_type=F32)


def _dot_tn(a, b):
    return lax.dot_general(a, b, (((0,), (0,)), ((), ())), preferred_element_type=F32)


def _sigmoid(x):
    return jax.nn.sigmoid(x)


def _gelu_exact(x):
    return 0.5 * x * (1.0 + lax.erf(x * (2.0 ** -0.5)))


def _layernorm(x, g, b):
    mu = jnp.mean(x, axis=-1, keepdims=True)
    xc = x - mu
    var = jnp.mean(xc * xc, axis=-1, keepdims=True)
    return xc * lax.rsqrt(var + LN_EPS) * g + b


def _pack_bf16_pairs(x):
    n = x.shape[1]
    bits = lax.bitcast_convert_type(x.astype(BF16).astype(F32), U32)
    return bits[:, n // 2:] | (bits[:, :n // 2] >> 16)


def _unpack_bf16_pairs(w):
    lo = lax.bitcast_convert_type(w << 16, F32)
    hi = lax.bitcast_convert_type(w & jnp.uint32(0xFFFF0000), F32)
    return jnp.concatenate([lo, hi], axis=1)


def _split_bf16(x):
    hi = x.astype(BF16)
    lo = (x - hi.astype(F32)).astype(BF16)
    return hi, lo


def _ada_kernel(c_ref, w_ref, b_ref, o_ref):
    c = c_ref[...]
    ca = c * _sigmoid(c)
    o_ref[...] = jnp.dot(ca, w_ref[...], preferred_element_type=F32,
                         precision=lax.Precision.HIGHEST) + b_ref[...]


def _ada_call(c, w_ada, b_ada):
    bsz, d = c.shape
    n = w_ada.shape[1]
    rows = -(-bsz // SUBLANES) * SUBLANES
    c_pad = jnp.pad(c, ((0, rows - bsz), (0, 0)))
    out = pl.pallas_call(
        _ada_kernel,
        grid=(n // d,),
        in_specs=[pl.BlockSpec((rows, d), lambda i: (0, 0)),
                  pl.BlockSpec((d, d), lambda i: (0, i)),
                  pl.BlockSpec((1, d), lambda i: (0, i))],
        out_specs=pl.BlockSpec((rows, d), lambda i: (0, i)),
        out_shape=jax.ShapeDtypeStruct((rows, n), F32),
        name="ada",
    )(c_pad, w_ada, b_ada.reshape(1, n))
    return out[:bsz].reshape(bsz, n // d, d)


def _hgrn2_chunk(c, proj_ref, y_ref, st_ref, g_ref, h_ref, lb, gnw, tril, overlap):
    hw = HG_HEADS * HG_DK
    cl = HG_CHUNK
    rows = pl.ds(c * cl, cl)
    qz = proj_ref[rows, 0:hw]
    fz = proj_ref[rows, hw:2 * hw]
    v = proj_ref[rows, 2 * hw:3 * hw]
    og = proj_ref[rows, 3 * hw:4 * hw]

    q = qz * _sigmoid(qz)
    f = lb + (1.0 - lb) * _sigmoid(fz)
    lf = jnp.log(f)
    k = 1.0 - f
    lf_hi, lf_lo = _split_bf16(lf)
    g = (_dot(tril, lf_hi) + _dot(tril, lf_lo)) * LOG2E
    glast = g[cl - 1:cl, :]
    qg = (q * jnp.exp2(g)).astype(BF16)
    kd = (k * jnp.exp2(glast - g)).astype(BF16)
    vb = v.astype(BF16)
    g_ref[...] = g
    h_ref[...] = g - jnp.log(k) * LOG2E

    n_sub = cl // HG_SUB
    lane_c = lax.broadcasted_iota(I32, (HG_SUB, cl), 1)
    trow = lax.broadcasted_iota(I32, (HG_SUB, cl), 0)

    heads = [slice(hd * HG_DK, (hd + 1) * HG_DK) for hd in range(HG_HEADS)]

    o_state = []
    acc_all = []
    for hd, cs in enumerate(heads):
        st = st_ref[hd]
        o_state.append(_dot_nt(qg[:, cs], st.astype(BF16)))
        st_ref[hd] = st * jnp.exp2(glast[:, cs]) + _dot_tn(vb[:, cs], kd[:, cs])
        gh = g[:, cs]
        kh = k[:, cs]
        qh = q[:, cs]
        acc = [None] * n_sub
        w = cl // 2
        while w >= HG_SUB:
            for p in range(0, cl, 2 * w):
                ref = g_ref[p + w - 1:p + w, cs]
                qt = (qh[p + w:p + 2 * w] * jnp.exp2(gh[p + w:p + 2 * w] - ref)).astype(BF16)
                kt = (kh[p:p + w] * jnp.exp2(ref - gh[p:p + w])).astype(BF16)
                pieces = [kt]
                if p:
                    pieces.insert(0, jnp.zeros((p, HG_DK), BF16))
                if cl - p - w:
                    pieces.append(jnp.zeros((cl - p - w, HG_DK), BF16))
                blk = _dot_nt(qt, jnp.concatenate(pieces, axis=0))
                for r in range(w // HG_SUB):
                    i = (p + w) // HG_SUB + r
                    part = blk[r * HG_SUB:(r + 1) * HG_SUB]
                    acc[i] = part if acc[i] is None else acc[i] + part
            w //= 2
        acc_all.append(acc)

    overlap()

    for hd, cs in enumerate(heads):
        gh = g[:, cs]
        qh = q[:, cs]
        a_rows = []
        for i in range(n_sub):
            b0 = i * HG_SUB
            gb = gh[b0:b0 + HG_SUB]
            qb = qh[b0:b0 + HG_SUB]
            a = jnp.zeros((HG_SUB, cl), F32) if acc_all[hd][i] is None else acc_all[hd][i]
            for s in range(HG_SUB):
                hs = h_ref[b0 + s:b0 + s + 1, cs]
                col = jnp.sum(qb * jnp.exp2(gb - hs), axis=-1, keepdims=True)
                a = jnp.where(lane_c == b0 + s, col, a)
            a_rows.append(jnp.where(lane_c <= b0 + trow, a, 0.0))
        amat = jnp.concatenate(a_rows, axis=0).astype(BF16)
        o = o_state[hd] + _dot(amat, vb[:, cs])
        ms = jnp.mean(o * o, axis=-1, keepdims=True)
        ogh = og[:, cs]
        ya = o * lax.rsqrt(ms + RMS_EPS) * gnw * (ogh * _sigmoid(ogh))
        y_ref[rows, cs] = ya.astype(BF16)


def _route(logits):
    ts = logits.shape[0]
    lane = lax.broadcasted_iota(I32, (ts, LANES), 1)
    neg = -jnp.inf
    gl = jnp.where(lane < MOE_GROUPS, logits, neg)
    gmax = jnp.max(gl, axis=-1, keepdims=True)
    gidx = jnp.min(jnp.where(gl == gmax, lane, LANES), axis=-1, keepdims=True)
    p_group = 1.0 / jnp.sum(jnp.exp(gl - gmax), axis=-1, keepdims=True)
    e_lo = MOE_GROUPS + gidx * MOE_EPG
    el = jnp.where((lane >= e_lo) & (lane < e_lo + MOE_EPG), logits, neg)
    m1 = jnp.max(el, axis=-1, keepdims=True)
    i1 = jnp.min(jnp.where(el == m1, lane, LANES), axis=-1, keepdims=True)
    el2 = jnp.where(lane == i1, neg, el)
    m2 = jnp.max(el2, axis=-1, keepdims=True)
    i2 = jnp.min(jnp.where(el2 == m2, lane, LANES), axis=-1, keepdims=True)
    esum = jnp.sum(jnp.exp(el - m1), axis=-1, keepdims=True)
    p1 = 1.0 / esum
    p2 = jnp.exp(m2 - m1) / esum
    den = p1 + p2
    w1 = p_group * p1 / den
    w2 = p_group * p2 / den
    e1 = (i1 - MOE_GROUPS).astype(F32)
    e2 = (i2 - MOE_GROUPS).astype(F32)
    return jnp.where(lane == 0, e1, jnp.where(lane == 1, e2,
                     jnp.where(lane == 2, w1, jnp.where(lane == 3, w2, 0.0))))


def _mix_kernel(nt, x_ref, xn_ref, ada_ref, adan_ref, win_ref, lbl_ref, gnw_ref, sglg_ref, sglb_ref,
                sgw_ref, sgbt_ref, wout_ref, ln1g_ref, ln1b_ref, wrh_ref, wrl_ref, br_ref,
                x1_ref, route_ref, *rest):
    h2p_refs = rest[:-6]
    st_ref, proj_ref, projn_ref, y_ref, g_ref, k_ref = rest[-6:]
    ts, d = x_ref.shape
    hw = HG_HEADS * HG_DK
    sgw = SG_GROUPS * SG_CH
    ncol = proj_ref.shape[1]
    step = pl.program_id(0)

    def modulated(xr, adar):
        a = adar[...]
        return (xr[...] * (1.0 + a[1:2]) + a[0:1]).astype(BF16)

    @pl.when(step % nt == 0)
    def _():
        st_ref[...] = jnp.zeros_like(st_ref)

    @pl.when(step == 0)
    def _():
        proj_ref[...] = _dot(modulated(x_ref, ada_ref), win_ref[...])

    x = x_ref[...]
    ada = ada_ref[...]
    g1, sh2, sc2 = ada[2:3], ada[3:4], ada[4:5]
    hn = modulated(xn_ref, adan_ref)

    lbl = lbl_ref[...]
    slots = [lbl[i:i + 1] for i in range(lbl.shape[0])]
    mx = functools.reduce(jnp.maximum, slots)
    ex = [jnp.exp(s - mx) for s in slots]
    lb = ex[0] / functools.reduce(lambda a, b: a + b, ex)

    ci = lax.broadcasted_iota(I32, (HG_CHUNK, HG_CHUNK), 0)
    cj = lax.broadcasted_iota(I32, (HG_CHUNK, HG_CHUNK), 1)
    tril = (ci >= cj).astype(BF16)
    gnw = gnw_ref[...]

    n_chunks = ts // HG_CHUNK
    n_sg = ts // SG_CHUNK
    n_pp = ncol // PROJ_PIECE
    n_tail = max(n_pp - n_chunks, 0)
    slot_of = [p if p < n_pp - n_tail else n_chunks + (2 * (p - (n_pp - n_tail))) // max(n_tail, 1)
               for p in range(n_pp)]

    def next_proj_pieces(slot):
        for p in range(n_pp):
            if min(slot_of[p], n_chunks + 1) == slot:
                cols = slice(p * PROJ_PIECE, (p + 1) * PROJ_PIECE)
                projn_ref[:, cols] = _dot(hn, win_ref[:, cols])

    for c in range(n_chunks):
        _hgrn2_chunk(c, proj_ref, y_ref, st_ref, g_ref, k_ref, lb, gnw, tril,
                     functools.partial(next_proj_pieces, c))

    pi = lax.broadcasted_iota(I32, (SG_CHUNK, SG_CHUNK), 0)
    pj = lax.broadcasted_iota(I32, (SG_CHUNK, SG_CHUNK), 1)
    sglg = sglg_ref[...]
    sglb = sglb_ref[...]
    for p in range(n_sg):
        rows = slice(p * SG_CHUNK, (p + 1) * SG_CHUNK)
        z = _gelu_exact(proj_ref[rows, 4 * hw:4 * hw + 2 * sgw])
        u = z[:, :sgw]
        vn = _layernorm(z[:, sgw:], sglg, sglb).astype(BF16)
        for gi in range(SG_GROUPS):
            cs = slice(gi * SG_CH, (gi + 1) * SG_CH)
            wc = jnp.where(pi >= pj, sgw_ref[gi], 0.0).astype(BF16)
            mixed = _dot(wc, vn[:, cs]) + sgbt_ref[:, gi:gi + 1]
            y_ref[rows, hw + gi * SG_CH:hw + (gi + 1) * SG_CH] = (u[:, cs] * mixed).astype(BF16)

    y = _dot(y_ref[...], wout_ref[...])
    next_proj_pieces(n_chunks)
    x1 = _layernorm(ALPHA * x + (1.0 + g1) * y, ln1g_ref[...], ln1b_ref[...])
    x1_ref[...] = x1
    h2 = x1 * (1.0 + sc2) + sh2

    h2_hi, h2_lo = _split_bf16(h2)
    packed = _pack_bf16_pairs(h2)
    for i, ref in enumerate(h2p_refs):
        ref[...] = packed[:, i * SC_ROW_WORDS:(i + 1) * SC_ROW_WORDS]

    wrh = wrh_ref[...]
    logits = _dot(h2_hi, wrh) + _dot(h2_lo, wrh) + _dot(h2_hi, wrl_ref[...]) + br_ref[...]
    next_proj_pieces(n_chunks + 1)
    route_ref[...] = _route(logits)
    proj_ref[...] = projn_ref[...]


def _mix_call(x, ada, w_in_b, lb_logits, gnw, sglg, sglb, sg_w, sg_bt, w_out_b, ln1g, ln1b, wrh, wrl, br):
    bsz, s, d = x.shape
    ts = min(MIX_ROWS, s)
    nt = s // ts
    n_steps = bsz * nt
    ncol = w_in_b.shape[1]
    hw = HG_HEADS * HG_DK
    n_pieces = d // 2 // SC_ROW_WORDS
    const = lambda a: pl.BlockSpec(a.shape, lambda i: (0,) * a.ndim, pipeline_mode=pl.Buffered(1))
    cur = lambda i: (i // nt, i % nt, 0)
    nxt = lambda i: cur(jnp.minimum(i + 1, n_steps - 1))
    return pl.pallas_call(
        functools.partial(_mix_kernel, nt),
        grid=(n_steps,),
        in_specs=[pl.BlockSpec((None, ts, d), cur),
                  pl.BlockSpec((None, ts, d), nxt),
                  pl.BlockSpec((None,) + ada.shape[1:], lambda i: (cur(i)[0], 0, 0)),
                  pl.BlockSpec((None,) + ada.shape[1:], lambda i: (nxt(i)[0], 0, 0)),
                  const(w_in_b), const(lb_logits), const(gnw), const(sglg), const(sglb), const(sg_w),
                  const(sg_bt), const(w_out_b), const(ln1g), const(ln1b), const(wrh), const(wrl),
                  const(br)],
        out_specs=[pl.BlockSpec((None, ts, d), cur),
                   pl.BlockSpec((None, ts, LANES), cur)]
                  + [pl.BlockSpec((None, ts, SC_ROW_WORDS), cur)] * n_pieces,
        out_shape=[jax.ShapeDtypeStruct((bsz, s, d), F32),
                   jax.ShapeDtypeStruct((bsz, s, LANES), F32)]
                  + [jax.ShapeDtypeStruct((bsz, s, SC_ROW_WORDS), U32)] * n_pieces,
        scratch_shapes=[pltpu.VMEM((HG_HEADS, HG_DK, HG_DK), F32),
                        pltpu.VMEM((ts, ncol), F32),
                        pltpu.VMEM((ts, ncol), F32),
                        pltpu.VMEM((ts, d), BF16),
                        pltpu.VMEM((HG_CHUNK, hw), F32),
                        pltpu.VMEM((HG_CHUNK, hw), F32)],
        compiler_params=pltpu.CompilerParams(
            dimension_semantics=("arbitrary",), vmem_limit_bytes=VMEM_LIMIT),
        name="mix",
    )(x, x, ada, ada, w_in_b, lb_logits, gnw, sglg, sglb, sg_w, sg_bt, w_out_b, ln1g, ln1b, wrh, wrl, br)


def _sort_kernel(route_ref, dest_ref, blk_ref, cnt_ref, pst_ref, carry_ref):
    ph = pl.program_id(0)
    i = pl.program_id(1)
    tk = route_ref.shape[0]
    lane = lax.broadcasted_iota(I32, (tk, LANES), 1)
    r = route_ref[...]
    oh1 = lane == r[:, 0:1].astype(I32)
    oh2 = lane == r[:, 1:2].astype(I32)
    hot = jnp.where(oh1 | oh2, 1.0, 0.0)
    colsum = jnp.sum(hot, axis=0, keepdims=True)

    @pl.when((ph == 0) & (i == 0))
    def _():
        cnt_ref[...] = jnp.zeros_like(cnt_ref)

    @pl.when(ph == 0)
    def _():
        cnt_ref[...] += colsum

    @pl.when((ph == 1) & (i == 0))
    def _():
        nblk = jnp.floor((cnt_ref[...] + (MOE_ROWS - 1)) * (1.0 / MOE_ROWS))
        ei = lax.broadcasted_iota(I32, (LANES, LANES), 0)
        ej = lax.broadcasted_iota(I32, (LANES, LANES), 1)
        upper = (ei < ej).astype(BF16)
        nblk8 = jnp.broadcast_to(nblk, (SUBLANES, LANES)).astype(BF16)
        pstart = _dot(nblk8, upper)[0:1]
        pst_ref[...] = pstart * float(MOE_ROWS)
        carry_ref[...] = jnp.zeros_like(carry_ref)
        row_b = lax.broadcasted_iota(I32, blk_ref.shape, 0)
        blk_ref[...] = jnp.where(row_b == 0, pstart, jnp.where(row_b == 1, nblk, 0.0)).astype(I32)

    @pl.when(ph == 1)
    def _():
        ti = lax.broadcasted_iota(I32, (tk, tk), 0)
        tj = lax.broadcasted_iota(I32, (tk, tk), 1)
        before = (ti > tj).astype(BF16)
        base = _dot(before, hot.astype(BF16)) + carry_ref[...] + pst_ref[...]
        d1 = jnp.sum(jnp.where(oh1, base, 0.0), axis=-1, keepdims=True)
        d2 = jnp.sum(jnp.where(oh2, base, 0.0), axis=-1, keepdims=True)
        dest_ref[...] = jnp.where(lane == 0, d1, jnp.where(lane == 1, d2, 0.0)).astype(I32)
        carry_ref[...] += colsum


def _sort_call(route):
    t = route.shape[0]
    tk = min(SORT_ROWS, t)
    return pl.pallas_call(
        _sort_kernel,
        grid=(2, t // tk),
        in_specs=[pl.BlockSpec((tk, LANES), lambda p, i: (i, 0))],
        out_specs=[pl.BlockSpec((tk, LANES), lambda p, i: (i * p, 0)),
                   pl.BlockSpec((SUBLANES, LANES), lambda p, i: (0, 0))],
        out_shape=[jax.ShapeDtypeStruct((t, LANES), I32),
                   jax.ShapeDtypeStruct((SUBLANES, LANES), I32)],
        scratch_shapes=[pltpu.VMEM((1, LANES), F32), pltpu.VMEM((1, LANES), F32),
                        pltpu.VMEM((1, LANES), F32)],
        compiler_params=pltpu.CompilerParams(dimension_semantics=("arbitrary", "arbitrary")),
        name="sort",
    )(route)


def _sc_mesh():
    return plsc.VectorSubcoreMesh(core_axis_name="c", subcore_axis_name="s")


def _sc_scatter_rows2(rows, idx_a, idx_b, n_out):
    n, w = rows.shape

    @pl.kernel(out_type=jax.ShapeDtypeStruct((n_out, w), rows.dtype), mesh=_sc_mesh(), scratch_types=[])
    def scatter(x_hbm, ia_hbm, ib_hbm, o_hbm):
        def body(x_vmem, ia_vmem, ib_vmem):
            pltpu.sync_copy(x_vmem, o_hbm.at[ia_vmem.at[0]])
            pltpu.sync_copy(x_vmem, o_hbm.at[ib_vmem.at[0]])

        pltpu.emit_pipeline(
            body,
            grid=(n // SC_WINDOW,),
            in_specs=[pl.BlockSpec((SC_WINDOW, w), lambda i: (i, 0)),
                      pl.BlockSpec((1, SC_WINDOW), lambda i: (0, i)),
                      pl.BlockSpec((1, SC_WINDOW), lambda i: (0, i))],
            out_specs=[],
            core_axis_name=("c", "s"),
            dimension_semantics=(pltpu.PARALLEL,),
        )(x_hbm, ia_hbm, ib_hbm)

    return scatter(rows, idx_a.reshape(1, n), idx_b.reshape(1, n))


def _sc_gather_rows(src, idx):
    n = idx.shape[0]
    w = src.shape[1]

    @pl.kernel(out_type=jax.ShapeDtypeStruct((n, w), src.dtype), mesh=_sc_mesh(), scratch_types=[])
    def gather(x_hbm, i_hbm, o_hbm):
        def body(i_vmem, o_vmem):
            pltpu.sync_copy(x_hbm.at[i_vmem.at[0]], o_vmem)

        pltpu.emit_pipeline(
            body,
            grid=(n // SC_WINDOW,),
            in_specs=[pl.BlockSpec((1, SC_WINDOW), lambda i: (0, i))],
            out_specs=[pl.BlockSpec((SC_WINDOW, w), lambda i: (i, 0))],
            core_axis_name=("c", "s"),
            dimension_semantics=(pltpu.PARALLEL,),
        )(i_hbm, o_hbm)

    return gather(src, idx.reshape(1, n))


def _moe_kernel(n_in, first_ref, count_ref, *refs):
    xb_hbm = refs[:n_in]
    wup_ref, wdn_ref = refs[n_in:n_in + 2]
    y_hbm = refs[n_in + 2:2 * n_in + 2]
    wupb_ref, wdnb_ref, xbuf, ybuf, in_sem, out_sem = refs[2 * n_in + 2:]
    e = pl.program_id(0)
    first = first_ref[e]
    count = count_ref[e]

    def rows_of(b):
        return pl.ds(pl.multiple_of((first + b) * MOE_ROWS, MOE_ROWS), MOE_ROWS)

    def in_copy(b, slot, i):
        return pltpu.make_async_copy(xb_hbm[i].at[rows_of(b)], xbuf.at[slot, i], in_sem.at[slot, i])

    def out_copy(b, slot, i):
        return pltpu.make_async_copy(ybuf.at[slot, i], y_hbm[i].at[rows_of(b)], out_sem.at[slot, i])

    @pl.when(count > 0)
    def _():
        for i in range(n_in):
            in_copy(0, 0, i).start()
        wupb_ref[...] = wup_ref[...].astype(BF16)
        wdnb_ref[...] = wdn_ref[...].astype(BF16)

        def block(b, carry):
            slot = b % 2
            for i in range(n_in):
                in_copy(b, slot, i).wait()

            @pl.when(b + 1 < count)
            def _():
                for i in range(n_in):
                    in_copy(b + 1, 1 - slot, i).start()

            @pl.when(b >= 2)
            def _():
                for i in range(n_in):
                    out_copy(b - 2, slot, i).wait()

            w = jnp.concatenate([xbuf[slot, i] for i in range(n_in)], axis=1)
            xrow = _unpack_bf16_pairs(w).astype(BF16)
            gu = _dot(xrow, wupb_ref[...])
            gate = gu[:, :MOE_HIDDEN]
            act = (gate * _sigmoid(gate) * gu[:, MOE_HIDDEN:]).astype(BF16)
            y = _pack_bf16_pairs(_dot(act, wdnb_ref[...]))
            for i in range(n_in):
                ybuf[slot, i] = y[:, i * SC_ROW_WORDS:(i + 1) * SC_ROW_WORDS]
                out_copy(b, slot, i).start()
            return carry

        lax.fori_loop(0, count, block, 0)

        @pl.when(count >= 2)
        def _():
            for i in range(n_in):
                out_copy(count - 2, count % 2, i).wait()

        for i in range(n_in):
            out_copy(count - 1, (count - 1) % 2, i).wait()


def _moe_call(first_block, block_count, xb_pieces, w_up, w_down):
    n_rows = xb_pieces[0].shape[0]
    n_in = len(xb_pieces)
    d = 2 * n_in * SC_ROW_WORDS
    n_exp, _, hid2 = w_up.shape
    hid = w_down.shape[1]
    hbm = pl.BlockSpec(memory_space=pl.ANY)
    buf = pltpu.VMEM((2, n_in, MOE_ROWS, SC_ROW_WORDS), U32)
    return pl.pallas_call(
        functools.partial(_moe_kernel, n_in),
        grid_spec=pltpu.PrefetchScalarGridSpec(
            num_scalar_prefetch=2,
            grid=(n_exp,),
            in_specs=[hbm] * n_in + [
                      pl.BlockSpec((None, d, hid2), lambda e, fb, bc: (e, 0, 0)),
                      pl.BlockSpec((None, hid, d), lambda e, fb, bc: (e, 0, 0))],
            out_specs=[hbm] * n_in,
            scratch_shapes=[pltpu.VMEM((d, hid2), BF16), pltpu.VMEM((hid, d), BF16), buf, buf,
                            pltpu.SemaphoreType.DMA((2, n_in)), pltpu.SemaphoreType.DMA((2, n_in))]),
        out_shape=[jax.ShapeDtypeStruct((n_rows, SC_ROW_WORDS), U32)] * n_in,
        compiler_params=pltpu.CompilerParams(
            dimension_semantics=("arbitrary",), vmem_limit_bytes=VMEM_LIMIT),
        name="moe",
    )(first_block, block_count, *xb_pieces, w_up, w_down)


def _combine_kernel(n_pieces, x1_ref, route_ref, ada_ref, g_ref, b_ref, *refs):
    o_ref = refs[-1]
    ya = _unpack_bf16_pairs(jnp.concatenate([r[...] for r in refs[:n_pieces]], axis=1))
    yb = _unpack_bf16_pairs(jnp.concatenate([r[...] for r in refs[n_pieces:2 * n_pieces]], axis=1))
    r = route_ref[...]
    m = ya * r[:, 2:3] + yb * r[:, 3:4]
    g2 = ada_ref[5:6]
    o_ref[...] = _layernorm(ALPHA * x1_ref[...] + (1.0 + g2) * m, g_ref[...], b_ref[...])


def _combine_call(x1, y_pieces, route, ada, ln2g, ln2b):
    bsz, s, d = x1.shape
    ts = min(OUT_ROWS, s)
    nj = s // ts
    n_pieces = len(y_pieces)
    slot = lambda k: pl.BlockSpec((ts, SC_ROW_WORDS), lambda b, j: (k * bsz * nj + b * nj + j, 0))
    return pl.pallas_call(
        functools.partial(_combine_kernel, n_pieces),
        grid=(bsz, nj),
        in_specs=[pl.BlockSpec((None, ts, d), lambda b, j: (b, j, 0)),
                  pl.BlockSpec((None, ts, LANES), lambda b, j: (b, j, 0)),
                  pl.BlockSpec((None,) + ada.shape[1:], lambda b, j: (b, 0, 0)),
                  pl.BlockSpec((1, d), lambda b, j: (0, 0)),
                  pl.BlockSpec((1, d), lambda b, j: (0, 0))]
                 + [slot(0)] * n_pieces + [slot(1)] * n_pieces,
        out_specs=pl.BlockSpec((None, ts, d), lambda b, j: (b, j, 0)),
        out_shape=jax.ShapeDtypeStruct((bsz, s, d), F32),
        compiler_params=pltpu.CompilerParams(dimension_semantics=("arbitrary", "arbitrary")),
        name="combine",
    )(x1, route, ada, ln2g, ln2b, *y_pieces, *y_pieces)


def kernel(x, c, w_ada, b_ada, w_in, lb_logits, hg_norm_w, sg_ln_g, sg_ln_b, sg_w, sg_b, w_out, ln1_g, ln1_b, router_group_w, router_group_b, router_expert_w, router_expert_b, w_up, w_down, ln2_g, ln2_b):
    assert w_in.shape[0] == DEPTH
    bsz, s, d = x.shape
    t = bsz * s
    l = 0

    ada = _ada_call(c, w_ada[l], b_ada[l])

    wr = jnp.concatenate(
        [router_group_w[l], router_expert_w[l].transpose(1, 0, 2).reshape(d, MOE_EXPERTS)], axis=1)
    wr = jnp.pad(wr, ((0, 0), (0, LANES - wr.shape[1])))
    br = jnp.concatenate([router_group_b[l], router_expert_b[l].reshape(MOE_EXPERTS)])
    br = jnp.pad(br, (0, LANES - br.shape[0])).reshape(1, LANES)
    wrh, wrl = _split_bf16(wr)

    x1, route, *h2p = _mix_call(
        x, ada, w_in[l].astype(BF16), lb_logits, hg_norm_w[l].reshape(1, -1),
        sg_ln_g[l].reshape(1, -1), sg_ln_b[l].reshape(1, -1), sg_w[l], sg_b[l].T,
        w_out[l].astype(BF16), ln1_g[l].reshape(1, d), ln1_b[l].reshape(1, d), wrh, wrl, br)

    n_blocks = -(-(2 * t) // MOE_ROWS) + MOE_EXPERTS
    n_rows = n_blocks * MOE_ROWS
    dest, blk = _sort_call(route.reshape(t, LANES))
    d0, d1 = dest[:, 0], dest[:, 1]

    xb = [_sc_scatter_rows2(p.reshape(t, SC_ROW_WORDS), d0, d1, n_rows) for p in h2p]
    yb = _moe_call(blk[0, :MOE_EXPERTS], blk[1, :MOE_EXPERTS], xb, w_up[l], w_down[l])
    d01 = jnp.concatenate([d0, d1])
    y2 = [_sc_gather_rows(p, d01) for p in yb]
    return _combine_call(x1, y2, route, ada, ln2_g[l].reshape(1, d), ln2_b[l].reshape(1, d))
```

```python
import functools

import jax
import jax.numpy as jnp
from jax import lax
from jax.experimental import pallas as pl
from jax.experimental.pallas import tpu as pltpu
from jax.experimental.pallas import tpu_sc as plsc

F32 = jnp.float32
BF16 = jnp.bfloat16
I32 = jnp.int32
U32 = jnp.uint32

HG_HEADS = 4
HG_DK = 128
HG_CHUNK = 64
HG_SUB = 8
SG_GROUPS = 4
SG_CH = 128
SG_CHUNK = 128
MOE_GROUPS = 4
MOE_EPG = 8
MOE_EXPERTS = MOE_GROUPS * MOE_EPG
MOE_HIDDEN = 512
DEPTH = 1
ALPHA = (2.0 * DEPTH) ** 0.25
LN_EPS = 1e-5
RMS_EPS = 1e-6
LOG2E = 1.4426950408889634

LANES = 128
SUBLANES = 8
SC_WINDOW = 128
SC_ROW_WORDS = 256

MIX_ROWS = 512
PROJ_PIECE = 256
SORT_ROWS = 1024
MOE_ROWS = 256
MOE_IN_BUFS = 4
MOE_OUT_BUFS = 2
OUT_ROWS = 512
VMEM_LIMIT = 52 * 1024 * 1024


def _dot(a, b):
    return jnp.dot(a, b, preferred_element_type=F32)


def _dot_nt(a, b):
    return lax.dot_general(a, b, (((1,), (1,)), ((), ())), preferred_element_type=F32)


def _dot_tn(a, b):
    return lax.dot_general(a, b, (((0,), (0,)), ((), ())), preferred_element_type=F32)


def _sigmoid(x):
    return jax.nn.sigmoid(x)


def _gelu_exact(x):
    return 0.5 * x * (1.0 + lax.erf(x * (2.0 ** -0.5)))


def _layernorm(x, g, b):
    mu = jnp.mean(x, axis=-1, keepdims=True)
    xc = x - mu
    var = jnp.mean(xc * xc, axis=-1, keepdims=True)
    return xc * lax.rsqrt(var + LN_EPS) * g + b


def _pack_bf16_pairs(x):
    n = x.shape[1]
    bits = lax.bitcast_convert_type(x.astype(BF16).astype(F32), U32)
    return bits[:, n // 2:] | (bits[:, :n // 2] >> 16)


def _unpack_bf16_pairs(w):
    lo = lax.bitcast_convert_type(w << 16, F32)
    hi = lax.bitcast_convert_type(w & jnp.uint32(0xFFFF0000), F32)
    return jnp.concatenate([lo, hi], axis=1)


def _split_bf16(x):
    hi = x.astype(BF16)
    lo = (x - hi.astype(F32)).astype(BF16)
    return hi, lo


def _ada_kernel(c_ref, w_ref, b_ref, o_ref):
    c = c_ref[...]
    ca = c * _sigmoid(c)
    o_ref[...] = jnp.dot(ca, w_ref[...], preferred_element_type=F32,
                         precision=lax.Precision.HIGHEST) + b_ref[...]


def _ada_call(c, w_ada, b_ada):
    bsz, d = c.shape
    n = w_ada.shape[1]
    rows = -(-bsz // SUBLANES) * SUBLANES
    c_pad = jnp.pad(c, ((0, rows - bsz), (0, 0)))
    out = pl.pallas_call(
        _ada_kernel,
        grid=(n // d,),
        in_specs=[pl.BlockSpec((rows, d), lambda i: (0, 0)),
                  pl.BlockSpec((d, d), lambda i: (0, i)),
                  pl.BlockSpec((1, d), lambda i: (0, i))],
        out_specs=pl.BlockSpec((rows, d), lambda i: (0, i)),
        out_shape=jax.ShapeDtypeStruct((rows, n), F32),
        name="ada",
    )(c_pad, w_ada, b_ada.reshape(1, n))
    return out[:bsz].reshape(bsz, n // d, d)


def _hgrn2_chunk(c, proj_ref, y_ref, st_ref, g_ref, h_ref, lb, gnw, tril, overlap):
    hw = HG_HEADS * HG_DK
    cl = HG_CHUNK
    rows = pl.ds(c * cl, cl)
    qz = proj_ref[rows, 0:hw]
    fz = proj_ref[rows, hw:2 * hw]
    v = proj_ref[rows, 2 * hw:3 * hw]
    og = proj_ref[rows, 3 * hw:4 * hw]

    q = qz * _sigmoid(qz)
    f = lb + (1.0 - lb) * _sigmoid(fz)
    lf = jnp.log(f)
    k = 1.0 - f
    lf_hi, lf_lo = _split_bf16(lf)
    g = (_dot(tril, lf_hi) + _dot(tril, lf_lo)) * LOG2E
    glast = g[cl - 1:cl, :]
    qg = (q * jnp.exp2(g)).astype(BF16)
    kd = (k * jnp.exp2(glast - g)).astype(BF16)
    vb = v.astype(BF16)
    g_ref[...] = g
    h_ref[...] = g - jnp.log(k) * LOG2E

    n_sub = cl // HG_SUB
    lane_c = lax.broadcasted_iota(I32, (HG_SUB, cl), 1)
    trow = lax.broadcasted_iota(I32, (HG_SUB, cl), 0)

    heads = [slice(hd * HG_DK, (hd + 1) * HG_DK) for hd in range(HG_HEADS)]

    o_state = []
    acc_all = []
    for hd, cs in enumerate(heads):
        st = st_ref[hd]
        o_state.append(_dot_nt(qg[:, cs], st.astype(BF16)))
        st_ref[hd] = st * jnp.exp2(glast[:, cs]) + _dot_tn(vb[:, cs], kd[:, cs])
        gh = g[:, cs]
        kh = k[:, cs]
        qh = q[:, cs]
        acc = [None] * n_sub
        w = cl // 2
        while w >= HG_SUB:
            for p in range(0, cl, 2 * w):
                ref = g_ref[p + w - 1:p + w, cs]
                qt = (qh[p + w:p + 2 * w] * jnp.exp2(gh[p + w:p + 2 * w] - ref)).astype(BF16)
                kt = (kh[p:p + w] * jnp.exp2(ref - gh[p:p + w])).astype(BF16)
                pieces = [kt]
                if p:
                    pieces.insert(0, jnp.zeros((p, HG_DK), BF16))
                if cl - p - w:
                    pieces.append(jnp.zeros((cl - p - w, HG_DK), BF16))
                blk = _dot_nt(qt, jnp.concatenate(pieces, axis=0))
                for r in range(w // HG_SUB):
                    i = (p + w) // HG_SUB + r
                    part = blk[r * HG_SUB:(r + 1) * HG_SUB]
                    acc[i] = part if acc[i] is None else acc[i] + part
            w //= 2
        acc_all.append(acc)

    overlap()

    for hd, cs in enumerate(heads):
        gh = g[:, cs]
        qh = q[:, cs]
        a_rows = []
        for i in range(n_sub):
            b0 = i * HG_SUB
            gb = gh[b0:b0 + HG_SUB]
            qb = qh[b0:b0 + HG_SUB]
            a = jnp.zeros((HG_SUB, cl), F32) if acc_all[hd][i] is None else acc_all[hd][i]
            for s in range(HG_SUB):
                hs = h_ref[b0 + s:b0 + s + 1, cs]
                col = jnp.sum(qb * jnp.exp2(gb - hs), axis=-1, keepdims=True)
                a = jnp.where(lane_c == b0 + s, col, a)
            a_rows.append(jnp.where(lane_c <= b0 + trow, a, 0.0))
        amat = jnp.concatenate(a_rows, axis=0).astype(BF16)
        o = o_state[hd] + _dot(amat, vb[:, cs])
        ms = jnp.mean(o * o, axis=-1, keepdims=True)
        ogh = og[:, cs]
        ya = o * lax.rsqrt(ms + RMS_EPS) * gnw * (ogh * _sigmoid(ogh))
        y_ref[rows, cs] = ya.astype(BF16)


def _route(logits):
    ts = logits.shape[0]
    lane = lax.broadcasted_iota(I32, (ts, LANES), 1)
    neg = -jnp.inf
    gl = jnp.where(lane < MOE_GROUPS, logits, neg)
    gmax = jnp.max(gl, axis=-1, keepdims=True)
    gidx = jnp.min(jnp.where(gl == gmax, lane, LANES), axis=-1, keepdims=True)
    p_group = 1.0 / jnp.sum(jnp.exp(gl - gmax), axis=-1, keepdims=True)
    e_lo = MOE_GROUPS + gidx * MOE_EPG
    el = jnp.where((lane >= e_lo) & (lane < e_lo + MOE_EPG), logits, neg)
    m1 = jnp.max(el, axis=-1, keepdims=True)
    i1 = jnp.min(jnp.where(el == m1, lane, LANES), axis=-1, keepdims=True)
    el2 = jnp.where(lane == i1, neg, el)
    m2 = jnp.max(el2, axis=-1, keepdims=True)
    i2 = jnp.min(jnp.where(el2 == m2, lane, LANES), axis=-1, keepdims=True)
    esum = jnp.sum(jnp.exp(el - m1), axis=-1, keepdims=True)
    p1 = 1.0 / esum
    p2 = jnp.exp(m2 - m1) / esum
    den = p1 + p2
    w1 = p_group * p1 / den
    w2 = p_group * p2 / den
    e1 = (i1 - MOE_GROUPS).astype(F32)
    e2 = (i2 - MOE_GROUPS).astype(F32)
    return jnp.where(lane == 0, e1, jnp.where(lane == 1, e2,
                     jnp.where(lane == 2, w1, jnp.where(lane == 3, w2, 0.0))))


def _mix_kernel(nt, x_ref, xn_ref, ada_ref, adan_ref, win_ref, lbl_ref, gnw_ref, sglg_ref, sglb_ref,
                sgw_ref, sgbt_ref, wout_ref, ln1g_ref, ln1b_ref, wrh_ref, wrl_ref, br_ref,
                x1_ref, route_ref, *rest):
    h2p_refs = rest[:-6]
    st_ref, proj_ref, projn_ref, y_ref, g_ref, k_ref = rest[-6:]
    ts, d = x_ref.shape
    hw = HG_HEADS * HG_DK
    sgw = SG_GROUPS * SG_CH
    ncol = proj_ref.shape[1]
    step = pl.program_id(0)

    def modulated(xr, adar):
        a = adar[...]
        return (xr[...] * (1.0 + a[1:2]) + a[0:1]).astype(BF16)

    @pl.when(step % nt == 0)
    def _():
        st_ref[...] = jnp.zeros_like(st_ref)

    @pl.when(step == 0)
    def _():
        proj_ref[...] = _dot(modulated(x_ref, ada_ref), win_ref[...])

    x = x_ref[...]
    ada = ada_ref[...]
    g1, sh2, sc2 = ada[2:3], ada[3:4], ada[4:5]
    hn = modulated(xn_ref, adan_ref)

    lbl = lbl_ref[...]
    slots = [lbl[i:i + 1] for i in range(lbl.shape[0])]
    mx = functools.reduce(jnp.maximum, slots)
    ex = [jnp.exp(s - mx) for s in slots]
    lb = ex[0] / functools.reduce(lambda a, b: a + b, ex)

    ci = lax.broadcasted_iota(I32, (HG_CHUNK, HG_CHUNK), 0)
    cj = lax.broadcasted_iota(I32, (HG_CHUNK, HG_CHUNK), 1)
    tril = (ci >= cj).astype(BF16)
    gnw = gnw_ref[...]

    n_chunks = ts // HG_CHUNK
    n_sg = ts // SG_CHUNK
    n_pp = ncol // PROJ_PIECE
    n_tail = max(n_pp - n_chunks, 0)
    slot_of = [p if p < n_pp - n_tail else n_chunks + (2 * (p - (n_pp - n_tail))) // max(n_tail, 1)
               for p in range(n_pp)]

    def next_proj_pieces(slot):
        for p in range(n_pp):
            if min(slot_of[p], n_chunks + 1) == slot:
                cols = slice(p * PROJ_PIECE, (p + 1) * PROJ_PIECE)
                projn_ref[:, cols] = _dot(hn, win_ref[:, cols])

    for c in range(n_chunks):
        _hgrn2_chunk(c, proj_ref, y_ref, st_ref, g_ref, k_ref, lb, gnw, tril,
                     functools.partial(next_proj_pieces, c))

    pi = lax.broadcasted_iota(I32, (SG_CHUNK, SG_CHUNK), 0)
    pj = lax.broadcasted_iota(I32, (SG_CHUNK, SG_CHUNK), 1)
    sglg = sglg_ref[...]
    sglb = sglb_ref[...]
    for p in range(n_sg):
        rows = slice(p * SG_CHUNK, (p + 1) * SG_CHUNK)
        z = _gelu_exact(proj_ref[rows, 4 * hw:4 * hw + 2 * sgw])
        u = z[:, :sgw]
        vn = _layernorm(z[:, sgw:], sglg, sglb).astype(BF16)
        for gi in range(SG_GROUPS):
            cs = slice(gi * SG_CH, (gi + 1) * SG_CH)
            wc = jnp.where(pi >= pj, sgw_ref[gi], 0.0).astype(BF16)
            mixed = _dot(wc, vn[:, cs]) + sgbt_ref[:, gi:gi + 1]
            y_ref[rows, hw + gi * SG_CH:hw + (gi + 1) * SG_CH] = (u[:, cs] * mixed).astype(BF16)

    y = _dot(y_ref[...], wout_ref[...])
    next_proj_pieces(n_chunks)
    x1 = _layernorm(ALPHA * x + (1.0 + g1) * y, ln1g_ref[...], ln1b_ref[...])
    x1_ref[...] = x1
    h2 = x1 * (1.0 + sc2) + sh2

    h2_hi, h2_lo = _split_bf16(h2)
    packed = _pack_bf16_pairs(h2)
    for i, ref in enumerate(h2p_refs):
        ref[...] = packed[:, i * SC_ROW_WORDS:(i + 1) * SC_ROW_WORDS]

    wrh = wrh_ref[...]
    logits = _dot(h2_hi, wrh) + _dot(h2_lo, wrh) + _dot(h2_hi, wrl_ref[...]) + br_ref[...]
    next_proj_pieces(n_chunks + 1)
    route_ref[...] = _route(logits)
    proj_ref[...] = projn_ref[...]


def _mix_call(x, ada, w_in_b, lb_logits, gnw, sglg, sglb, sg_w, sg_bt, w_out_b, ln1g, ln1b, wrh, wrl, br):
    bsz, s, d = x.shape
    ts = min(MIX_ROWS, s)
    nt = s // ts
    n_steps = bsz * nt
    ncol = w_in_b.shape[1]
    hw = HG_HEADS * HG_DK
    n_pieces = d // 2 // SC_ROW_WORDS
    const = lambda a: pl.BlockSpec(a.shape, lambda i: (0,) * a.ndim, pipeline_mode=pl.Buffered(1))
    cur = lambda i: (i // nt, i % nt, 0)
    nxt = lambda i: cur(jnp.minimum(i + 1, n_steps - 1))
    return pl.pallas_call(
        functools.partial(_mix_kernel, nt),
        grid=(n_steps,),
        in_specs=[pl.BlockSpec((None, ts, d), cur),
                  pl.BlockSpec((None, ts, d), nxt),
                  pl.BlockSpec((None,) + ada.shape[1:], lambda i: (cur(i)[0], 0, 0)),
                  pl.BlockSpec((None,) + ada.shape[1:], lambda i: (nxt(i)[0], 0, 0)),
                  const(w_in_b), const(lb_logits), const(gnw), const(sglg), const(sglb), const(sg_w),
                  const(sg_bt), const(w_out_b), const(ln1g), const(ln1b), const(wrh), const(wrl),
                  const(br)],
        out_specs=[pl.BlockSpec((None, ts, d), cur),
                   pl.BlockSpec((None, ts, LANES), cur)]
                  + [pl.BlockSpec((None, ts, SC_ROW_WORDS), cur)] * n_pieces,
        out_shape=[jax.ShapeDtypeStruct((bsz, s, d), F32),
                   jax.ShapeDtypeStruct((bsz, s, LANES), F32)]
                  + [jax.ShapeDtypeStruct((bsz, s, SC_ROW_WORDS), U32)] * n_pieces,
        scratch_shapes=[pltpu.VMEM((HG_HEADS, HG_DK, HG_DK), F32),
                        pltpu.VMEM((ts, ncol), F32),
                        pltpu.VMEM((ts, ncol), F32),
                        pltpu.VMEM((ts, d), BF16),
                        pltpu.VMEM((HG_CHUNK, hw), F32),
                        pltpu.VMEM((HG_CHUNK, hw), F32)],
        compiler_params=pltpu.CompilerParams(
            dimension_semantics=("arbitrary",), vmem_limit_bytes=VMEM_LIMIT),
        name="mix",
    )(x, x, ada, ada, w_in_b, lb_logits, gnw, sglg, sglb, sg_w, sg_bt, w_out_b, ln1g, ln1b, wrh, wrl, br)


def _sort_kernel(route_ref, dest_ref, blk_ref, cnt_ref, pst_ref, carry_ref):
    ph = pl.program_id(0)
    i = pl.program_id(1)
    tk = route_ref.shape[0]
    lane = lax.broadcasted_iota(I32, (tk, LANES), 1)
    r = route_ref[...]
    oh1 = lane == r[:, 0:1].astype(I32)
    oh2 = lane == r[:, 1:2].astype(I32)
    hot = jnp.where(oh1 | oh2, 1.0, 0.0)
    colsum = jnp.sum(hot, axis=0, keepdims=True)

    @pl.when((ph == 0) & (i == 0))
    def _():
        cnt_ref[...] = jnp.zeros_like(cnt_ref)

    @pl.when(ph == 0)
    def _():
        cnt_ref[...] += colsum

    @pl.when((ph == 1) & (i == 0))
    def _():
        nblk = jnp.floor((cnt_ref[...] + (MOE_ROWS - 1)) * (1.0 / MOE_ROWS))
        ei = lax.broadcasted_iota(I32, (LANES, LANES), 0)
        ej = lax.broadcasted_iota(I32, (LANES, LANES), 1)
        upper = (ei < ej).astype(BF16)
        nblk8 = jnp.broadcast_to(nblk, (SUBLANES, LANES)).astype(BF16)
        pstart = _dot(nblk8, upper)[0:1]
        pst_ref[...] = pstart * float(MOE_ROWS)
        carry_ref[...] = jnp.zeros_like(carry_ref)
        row_b = lax.broadcasted_iota(I32, blk_ref.shape, 0)
        blk_ref[...] = jnp.where(row_b == 0, pstart, jnp.where(row_b == 1, nblk, 0.0)).astype(I32)

    @pl.when(ph == 1)
    def _():
        ti = lax.broadcasted_iota(I32, (tk, tk), 0)
        tj = lax.broadcasted_iota(I32, (tk, tk), 1)
        before = (ti > tj).astype(BF16)
        base = _dot(before, hot.astype(BF16)) + carry_ref[...] + pst_ref[...]
        d1 = jnp.sum(jnp.where(oh1, base, 0.0), axis=-1, keepdims=True)
        d2 = jnp.sum(jnp.where(oh2, base, 0.0), axis=-1, keepdims=True)
        dest_ref[...] = jnp.where(lane == 0, d1, jnp.where(lane == 1, d2, 0.0)).astype(I32)
        carry_ref[...] += colsum


def _sort_call(route):
    t = route.shape[0]
    tk = min(SORT_ROWS, t)
    return pl.pallas_call(
        _sort_kernel,
        grid=(2, t // tk),
        in_specs=[pl.BlockSpec((tk, LANES), lambda p, i: (i, 0))],
        out_specs=[pl.BlockSpec((tk, LANES), lambda p, i: (i * p, 0)),
                   pl.BlockSpec((SUBLANES, LANES), lambda p, i: (0, 0))],
        out_shape=[jax.ShapeDtypeStruct((t, LANES), I32),
                   jax.ShapeDtypeStruct((SUBLANES, LANES), I32)],
        scratch_shapes=[pltpu.VMEM((1, LANES), F32), pltpu.VMEM((1, LANES), F32),
                        pltpu.VMEM((1, LANES), F32)],
        compiler_params=pltpu.CompilerParams(dimension_semantics=("arbitrary", "arbitrary")),
        name="sort",
    )(route)


def _sc_mesh():
    return plsc.VectorSubcoreMesh(core_axis_name="c", subcore_axis_name="s")


def _sc_scatter_rows2(rows, idx_a, idx_b, n_out):
    n, w = rows.shape

    @pl.kernel(out_type=jax.ShapeDtypeStruct((n_out, w), rows.dtype), mesh=_sc_mesh(), scratch_types=[])
    def scatter(x_hbm, ia_hbm, ib_hbm, o_hbm):
        def body(x_vmem, ia_vmem, ib_vmem):
            pltpu.sync_copy(x_vmem, o_hbm.at[ia_vmem.at[0]])
            pltpu.sync_copy(x_vmem, o_hbm.at[ib_vmem.at[0]])

        pltpu.emit_pipeline(
            body,
            grid=(n // SC_WINDOW,),
            in_specs=[pl.BlockSpec((SC_WINDOW, w), lambda i: (i, 0)),
                      pl.BlockSpec((1, SC_WINDOW), lambda i: (0, i)),
                      pl.BlockSpec((1, SC_WINDOW), lambda i: (0, i))],
            out_specs=[],
            core_axis_name=("c", "s"),
            dimension_semantics=(pltpu.PARALLEL,),
        )(x_hbm, ia_hbm, ib_hbm)

    return scatter(rows, idx_a.reshape(1, n), idx_b.reshape(1, n))


def _sc_gather_rows(src, idx):
    n = idx.shape[0]
    w = src.shape[1]

    @pl.kernel(out_type=jax.ShapeDtypeStruct((n, w), src.dtype), mesh=_sc_mesh(), scratch_types=[])
    def gather(x_hbm, i_hbm, o_hbm):
        def body(i_vmem, o_vmem):
            pltpu.sync_copy(x_hbm.at[i_vmem.at[0]], o_vmem)

        pltpu.emit_pipeline(
            body,
            grid=(n // SC_WINDOW,),
            in_specs=[pl.BlockSpec((1, SC_WINDOW), lambda i: (0, i))],
            out_specs=[pl.BlockSpec((SC_WINDOW, w), lambda i: (i, 0))],
            core_axis_name=("c", "s"),
            dimension_semantics=(pltpu.PARALLEL,),
        )(i_hbm, o_hbm)

    return gather(src, idx.reshape(1, n))


def _moe_kernel(n_in, first_ref, count_ref, *refs):
    xb_hbm = refs[:n_in]
    wup_ref, wdn_ref = refs[n_in:n_in + 2]
    y_hbm = refs[n_in + 2:2 * n_in + 2]
    wupb_ref, wdnb_ref, xbuf, ybuf, in_sem, out_sem = refs[2 * n_in + 2:]
    e = pl.program_id(0)
    n_exp = pl.num_programs(0)
    first = first_ref[e]
    count = count_ref[e]
    total = first_ref[n_exp - 1] + count_ref[n_exp - 1]
    ahead = MOE_IN_BUFS - 1

    def rows_of(g):
        return pl.ds(pl.multiple_of(g * MOE_ROWS, MOE_ROWS), MOE_ROWS)

    def in_copy(g, i):
        slot = g % MOE_IN_BUFS
        return pltpu.make_async_copy(xb_hbm[i].at[rows_of(g)], xbuf.at[slot, i], in_sem.at[slot, i])

    def out_copy(g, i):
        slot = g % MOE_OUT_BUFS
        return pltpu.make_async_copy(ybuf.at[slot, i], y_hbm[i].at[rows_of(g)], out_sem.at[slot, i])

    @pl.when(e == 0)
    def _():
        for g in range(ahead):
            @pl.when(g < total)
            def _():
                for i in range(n_in):
                    in_copy(g, i).start()

    @pl.when(count > 0)
    def _():
        wupb_ref[...] = wup_ref[...].astype(BF16)
        wdnb_ref[...] = wdn_ref[...].astype(BF16)

        def block(b, carry):
            g = first + b
            for i in range(n_in):
                in_copy(g, i).wait()

            @pl.when(g + ahead < total)
            def _():
                for i in range(n_in):
                    in_copy(g + ahead, i).start()

            @pl.when(g >= MOE_OUT_BUFS)
            def _():
                for i in range(n_in):
                    out_copy(g - MOE_OUT_BUFS, i).wait()

            w = jnp.concatenate([xbuf[g % MOE_IN_BUFS, i] for i in range(n_in)], axis=1)
            xrow = _unpack_bf16_pairs(w).astype(BF16)
            gu = _dot(xrow, wupb_ref[...])
            gate = gu[:, :MOE_HIDDEN]
            act = (gate * _sigmoid(gate) * gu[:, MOE_HIDDEN:]).astype(BF16)
            y = _pack_bf16_pairs(_dot(act, wdnb_ref[...]))
            for i in range(n_in):
                ybuf[g % MOE_OUT_BUFS, i] = y[:, i * SC_ROW_WORDS:(i + 1) * SC_ROW_WORDS]
                out_copy(g, i).start()
            return carry

        lax.fori_loop(0, count, block, 0)

    @pl.when(e == n_exp - 1)
    def _():
        for k in range(MOE_OUT_BUFS):
            @pl.when(total - 1 - k >= 0)
            def _():
                for i in range(n_in):
                    out_copy(total - 1 - k, i).wait()


def _moe_call(first_block, block_count, xb_pieces, w_up, w_down):
    n_rows = xb_pieces[0].shape[0]
    n_in = len(xb_pieces)
    d = 2 * n_in * SC_ROW_WORDS
    n_exp, _, hid2 = w_up.shape
    hid = w_down.shape[1]
    hbm = pl.BlockSpec(memory_space=pl.ANY)
    buf = lambda n: pltpu.VMEM((n, n_in, MOE_ROWS, SC_ROW_WORDS), U32)
    return pl.pallas_call(
        functools.partial(_moe_kernel, n_in),
        grid_spec=pltpu.PrefetchScalarGridSpec(
            num_scalar_prefetch=2,
            grid=(n_exp,),
            in_specs=[hbm] * n_in + [
                      pl.BlockSpec((None, d, hid2), lambda e, fb, bc: (e, 0, 0)),
                      pl.BlockSpec((None, hid, d), lambda e, fb, bc: (e, 0, 0))],
            out_specs=[hbm] * n_in,
            scratch_shapes=[pltpu.VMEM((d, hid2), BF16), pltpu.VMEM((hid, d), BF16),
                            buf(MOE_IN_BUFS), buf(MOE_OUT_BUFS),
                            pltpu.SemaphoreType.DMA((MOE_IN_BUFS, n_in)),
                            pltpu.SemaphoreType.DMA((MOE_OUT_BUFS, n_in))]),
        out_shape=[jax.ShapeDtypeStruct((n_rows, SC_ROW_WORDS), U32)] * n_in,
        compiler_params=pltpu.CompilerParams(
            dimension_semantics=("arbitrary",), vmem_limit_bytes=VMEM_LIMIT),
        name="moe",
    )(first_block, block_count, *xb_pieces, w_up, w_down)


def _combine_kernel(n_pieces, x1_ref, route_ref, ada_ref, g_ref, b_ref, *refs):
    o_ref = refs[-1]
    ya = _unpack_bf16_pairs(jnp.concatenate([r[...] for r in refs[:n_pieces]], axis=1))
    yb = _unpack_bf16_pairs(jnp.concatenate([r[...] for r in refs[n_pieces:2 * n_pieces]], axis=1))
    r = route_ref[...]
    m = ya * r[:, 2:3] + yb * r[:, 3:4]
    g2 = ada_ref[5:6]
    o_ref[...] = _layernorm(ALPHA * x1_ref[...] + (1.0 + g2) * m, g_ref[...], b_ref[...])


def _combine_call(x1, y_pieces, route, ada, ln2g, ln2b):
    bsz, s, d = x1.shape
    ts = min(OUT_ROWS, s)
    nj = s // ts
    n_pieces = len(y_pieces)
    slot = lambda k: pl.BlockSpec((ts, SC_ROW_WORDS), lambda b, j: (k * bsz * nj + b * nj + j, 0))
    return pl.pallas_call(
        functools.partial(_combine_kernel, n_pieces),
        grid=(bsz, nj),
        in_specs=[pl.BlockSpec((None, ts, d), lambda b, j: (b, j, 0)),
                  pl.BlockSpec((None, ts, LANES), lambda b, j: (b, j, 0)),
                  pl.BlockSpec((None,) + ada.shape[1:], lambda b, j: (b, 0, 0)),
                  pl.BlockSpec((1, d), lambda b, j: (0, 0)),
                  pl.BlockSpec((1, d), lambda b, j: (0, 0))]
                 + [slot(0)] * n_pieces + [slot(1)] * n_pieces,
        out_specs=pl.BlockSpec((None, ts, d), lambda b, j: (b, j, 0)),
        out_shape=jax.ShapeDtypeStruct((bsz, s, d), F32),
        compiler_params=pltpu.CompilerParams(dimension_semantics=("arbitrary", "arbitrary")),
        name="combine",
    )(x1, route, ada, ln2g, ln2b, *y_pieces, *y_pieces)


def kernel(x, c, w_ada, b_ada, w_in, lb_logits, hg_norm_w, sg_ln_g, sg_ln_b, sg_w, sg_b, w_out, ln1_g, ln1_b, router_group_w, router_group_b, router_expert_w, router_expert_b, w_up, w_down, ln2_g, ln2_b):
    assert w_in.shape[0] == DEPTH
    bsz, s, d = x.shape
    t = bsz * s
    l = 0

    ada = _ada_call(c, w_ada[l], b_ada[l])

    wr = jnp.concatenate(
        [router_group_w[l], router_expert_w[l].transpose(1, 0, 2).reshape(d, MOE_EXPERTS)], axis=1)
    wr = jnp.pad(wr, ((0, 0), (0, LANES - wr.shape[1])))
    br = jnp.concatenate([router_group_b[l], router_expert_b[l].reshape(MOE_EXPERTS)])
    br = jnp.pad(br, (0, LANES - br.shape[0])).reshape(1, LANES)
    wrh, wrl = _split_bf16(wr)

    x1, route, *h2p = _mix_call(
        x, ada, w_in[l].astype(BF16), lb_logits, hg_norm_w[l].reshape(1, -1),
        sg_ln_g[l].reshape(1, -1), sg_ln_b[l].reshape(1, -1), sg_w[l], sg_b[l].T,
        w_out[l].astype(BF16), ln1_g[l].reshape(1, d), ln1_b[l].reshape(1, d), wrh, wrl, br)

    n_blocks = -(-(2 * t) // MOE_ROWS) + MOE_EXPERTS
    n_rows = n_blocks * MOE_ROWS
    dest, blk = _sort_call(route.reshape(t, LANES))
    d0, d1 = dest[:, 0], dest[:, 1]

    xb = [_sc_scatter_rows2(p.reshape(t, SC_ROW_WORDS), d0, d1, n_rows) for p in h2p]
    yb = _moe_call(blk[0, :MOE_EXPERTS], blk[1, :MOE_EXPERTS], xb, w_up[l], w_down[l])
    d01 = jnp.concatenate([d0, d1])
    y2 = [_sc_gather_rows(p, d01) for p in yb]
    return _combine_call(x1, y2, route, ada, ln2_g[l].reshape(1, d), ln2_b[l].reshape(1, d))
```

```python
import functools

import jax
import jax.numpy as jnp
from jax import lax
from jax.experimental import pallas as pl
from jax.experimental.pallas import tpu as pltpu
from jax.experimental.pallas import tpu_sc as plsc

F32 = jnp.float32
BF16 = jnp.bfloat16
I32 = jnp.int32
U32 = jnp.uint32

HG_HEADS = 4
HG_DK = 128
HG_CHUNK = 64
HG_SUB = 8
SG_GROUPS = 4
SG_CH = 128
SG_CHUNK = 128
MOE_GROUPS = 4
MOE_EPG = 8
MOE_EXPERTS = MOE_GROUPS * MOE_EPG
MOE_HIDDEN = 512
DEPTH = 1
ALPHA = (2.0 * DEPTH) ** 0.25
LN_EPS = 1e-5
RMS_EPS = 1e-6
LOG2E = 1.4426950408889634

LANES = 128
SUBLANES = 8
SC_WINDOW = 128
SC_ROW_WORDS = 256

MIX_ROWS = 512
PROJ_PIECE = 256
SORT_ROWS = 1024
SORT_BLOCK = 256
ROUTE_ROWS = 40
MOE_ROWS = 256
MOE_GROUP = 1
MOE_AHEAD = 3
MOE_IN_BUFS = MOE_AHEAD + MOE_GROUP
MOE_OUT_BUFS = 2
OUT_ROWS = 512
VMEM_LIMIT = 52 * 1024 * 1024


def _dot(a, b):
    return jnp.dot(a, b, preferred_element_type=F32)


def _dot_nt(a, b):
    return lax.dot_general(a, b, (((1,), (1,)), ((), ())), preferred_element_type=F32)


def _dot_tn(a, b):
    return lax.dot_general(a, b, (((0,), (0,)), ((), ())), preferred_element_type=F32)


def _sigmoid(x):
    return jax.nn.sigmoid(x)


def _gelu_exact(x):
    return 0.5 * x * (1.0 + lax.erf(x * (2.0 ** -0.5)))


def _layernorm(x, g, b):
    mu = jnp.mean(x, axis=-1, keepdims=True)
    xc = x - mu
    var = jnp.mean(xc * xc, axis=-1, keepdims=True)
    return xc * lax.rsqrt(var + LN_EPS) * g + b


def _pack_bf16_pairs(x):
    n = x.shape[1]
    bits = lax.bitcast_convert_type(x.astype(BF16).astype(F32), U32)
    return bits[:, n // 2:] | (bits[:, :n // 2] >> 16)


def _unpack_bf16_pairs(w):
    lo = lax.bitcast_convert_type(w << 16, F32)
    hi = lax.bitcast_convert_type(w & jnp.uint32(0xFFFF0000), F32)
    return jnp.concatenate([lo, hi], axis=1)


def _split_bf16(x):
    hi = x.astype(BF16)
    lo = (x - hi.astype(F32)).astype(BF16)
    return hi, lo


def _ada_kernel(c_ref, w_ref, b_ref, o_ref):
    c = c_ref[...]
    ca = c * _sigmoid(c)
    o_ref[...] = jnp.dot(ca, w_ref[...], preferred_element_type=F32,
                         precision=lax.Precision.HIGHEST) + b_ref[...]


def _ada_call(c, w_ada, b_ada):
    bsz, d = c.shape
    n = w_ada.shape[1]
    rows = -(-bsz // SUBLANES) * SUBLANES
    c_pad = jnp.pad(c, ((0, rows - bsz), (0, 0)))
    out = pl.pallas_call(
        _ada_kernel,
        grid=(n // d,),
        in_specs=[pl.BlockSpec((rows, d), lambda i: (0, 0)),
                  pl.BlockSpec((d, d), lambda i: (0, i)),
                  pl.BlockSpec((1, d), lambda i: (0, i))],
        out_specs=pl.BlockSpec((rows, d), lambda i: (0, i)),
        out_shape=jax.ShapeDtypeStruct((rows, n), F32),
        name="ada",
    )(c_pad, w_ada, b_ada.reshape(1, n))
    return out[:bsz].reshape(bsz, n // d, d)


def _hgrn2_chunk(c, proj_ref, y_ref, st_ref, g_ref, h_ref, lb, gnw, tril, overlap):
    hw = HG_HEADS * HG_DK
    cl = HG_CHUNK
    rows = pl.ds(c * cl, cl)
    qz = proj_ref[rows, 0:hw]
    fz = proj_ref[rows, hw:2 * hw]
    v = proj_ref[rows, 2 * hw:3 * hw]
    og = proj_ref[rows, 3 * hw:4 * hw]

    q = qz * _sigmoid(qz)
    f = lb + (1.0 - lb) * _sigmoid(fz)
    lf = jnp.log(f)
    k = 1.0 - f
    lf_hi, lf_lo = _split_bf16(lf)
    g = (_dot(tril, lf_hi) + _dot(tril, lf_lo)) * LOG2E
    glast = g[cl - 1:cl, :]
    qg = (q * jnp.exp2(g)).astype(BF16)
    kd = (k * jnp.exp2(glast - g)).astype(BF16)
    vb = v.astype(BF16)
    g_ref[...] = g
    h_ref[...] = g - jnp.log(k) * LOG2E

    n_sub = cl // HG_SUB
    lane_c = lax.broadcasted_iota(I32, (HG_SUB, cl), 1)
    trow = lax.broadcasted_iota(I32, (HG_SUB, cl), 0)

    heads = [slice(hd * HG_DK, (hd + 1) * HG_DK) for hd in range(HG_HEADS)]

    o_state = []
    acc_all = []
    for hd, cs in enumerate(heads):
        st = st_ref[hd]
        o_state.append(_dot_nt(qg[:, cs], st.astype(BF16)))
        st_ref[hd] = st * jnp.exp2(glast[:, cs]) + _dot_tn(vb[:, cs], kd[:, cs])
        gh = g[:, cs]
        kh = k[:, cs]
        qh = q[:, cs]
        acc = [None] * n_sub
        w = cl // 2
        while w >= HG_SUB:
            for p in range(0, cl, 2 * w):
                ref = g_ref[p + w - 1:p + w, cs]
                qt = (qh[p + w:p + 2 * w] * jnp.exp2(gh[p + w:p + 2 * w] - ref)).astype(BF16)
                kt = (kh[p:p + w] * jnp.exp2(ref - gh[p:p + w])).astype(BF16)
                pieces = [kt]
                if p:
                    pieces.insert(0, jnp.zeros((p, HG_DK), BF16))
                if cl - p - w:
                    pieces.append(jnp.zeros((cl - p - w, HG_DK), BF16))
                blk = _dot_nt(qt, jnp.concatenate(pieces, axis=0))
                for r in range(w // HG_SUB):
                    i = (p + w) // HG_SUB + r
                    part = blk[r * HG_SUB:(r + 1) * HG_SUB]
                    acc[i] = part if acc[i] is None else acc[i] + part
            w //= 2
        acc_all.append(acc)

    overlap()

    for hd, cs in enumerate(heads):
        gh = g[:, cs]
        qh = q[:, cs]
        a_rows = []
        for i in range(n_sub):
            b0 = i * HG_SUB
            gb = gh[b0:b0 + HG_SUB]
            qb = qh[b0:b0 + HG_SUB]
            a = jnp.zeros((HG_SUB, cl), F32) if acc_all[hd][i] is None else acc_all[hd][i]
            for s in range(HG_SUB):
                hs = h_ref[b0 + s:b0 + s + 1, cs]
                col = jnp.sum(qb * jnp.exp2(gb - hs), axis=-1, keepdims=True)
                a = jnp.where(lane_c == b0 + s, col, a)
            a_rows.append(jnp.where(lane_c <= b0 + trow, a, 0.0))
        amat = jnp.concatenate(a_rows, axis=0).astype(BF16)
        o = o_state[hd] + _dot(amat, vb[:, cs])
        ms = jnp.mean(o * o, axis=-1, keepdims=True)
        ogh = og[:, cs]
        ya = o * lax.rsqrt(ms + RMS_EPS) * gnw * (ogh * _sigmoid(ogh))
        y_ref[rows, cs] = ya.astype(BF16)


def _route_t(logits):
    nr, ts = logits.shape
    row = lax.broadcasted_iota(I32, (nr, ts), 0)
    neg = -jnp.inf
    gl = jnp.where(row < MOE_GROUPS, logits, neg)
    gmax = jnp.max(gl, axis=0, keepdims=True)
    gidx = jnp.min(jnp.where(gl == gmax, row, nr), axis=0, keepdims=True)
    p_group = 1.0 / jnp.sum(jnp.exp(gl - gmax), axis=0, keepdims=True)
    e_lo = MOE_GROUPS + gidx * MOE_EPG
    el = jnp.where((row >= e_lo) & (row < e_lo + MOE_EPG), logits, neg)
    m1 = jnp.max(el, axis=0, keepdims=True)
    i1 = jnp.min(jnp.where(el == m1, row, nr), axis=0, keepdims=True)
    el2 = jnp.where(row == i1, neg, el)
    m2 = jnp.max(el2, axis=0, keepdims=True)
    i2 = jnp.min(jnp.where(el2 == m2, row, nr), axis=0, keepdims=True)
    esum = jnp.sum(jnp.exp(el - m1), axis=0, keepdims=True)
    p1 = 1.0 / esum
    p2 = jnp.exp(m2 - m1) / esum
    den = p1 + p2
    w1 = p_group * p1 / den
    w2 = p_group * p2 / den
    e1 = (i1 - MOE_GROUPS).astype(F32)
    e2 = (i2 - MOE_GROUPS).astype(F32)
    out_row = lax.broadcasted_iota(I32, (SUBLANES, ts), 0)
    return jnp.where(out_row == 0, e1, jnp.where(out_row == 1, e2,
                     jnp.where(out_row == 2, w1, jnp.where(out_row == 3, w2, 0.0))))


def _mix_kernel(nt, x_ref, xn_ref, ada_ref, adan_ref, win_ref, lbl_ref, gnw_ref, sglg_ref, sglb_ref,
                sgw_ref, sgbt_ref, wout_ref, ln1g_ref, ln1b_ref, wrs_ref, br_ref,
                x1_ref, route_ref, routet_ref, *rest):
    h2p_refs = rest[:-6]
    st_ref, proj_ref, projn_ref, y_ref, g_ref, k_ref = rest[-6:]
    ts, d = x_ref.shape
    hw = HG_HEADS * HG_DK
    sgw = SG_GROUPS * SG_CH
    ncol = proj_ref.shape[1]
    step = pl.program_id(0)

    def modulated(xr, adar):
        a = adar[...]
        return (xr[...] * (1.0 + a[1:2]) + a[0:1]).astype(BF16)

    @pl.when(step % nt == 0)
    def _():
        st_ref[...] = jnp.zeros_like(st_ref)

    @pl.when(step == 0)
    def _():
        proj_ref[...] = _dot(modulated(x_ref, ada_ref), win_ref[...])

    x = x_ref[...]
    ada = ada_ref[...]
    g1, sh2, sc2 = ada[2:3], ada[3:4], ada[4:5]
    hn = modulated(xn_ref, adan_ref)

    lbl = lbl_ref[...]
    slots = [lbl[i:i + 1] for i in range(lbl.shape[0])]
    mx = functools.reduce(jnp.maximum, slots)
    ex = [jnp.exp(s - mx) for s in slots]
    lb = ex[0] / functools.reduce(lambda a, b: a + b, ex)

    ci = lax.broadcasted_iota(I32, (HG_CHUNK, HG_CHUNK), 0)
    cj = lax.broadcasted_iota(I32, (HG_CHUNK, HG_CHUNK), 1)
    tril = (ci >= cj).astype(BF16)
    gnw = gnw_ref[...]

    n_chunks = ts // HG_CHUNK
    n_sg = ts // SG_CHUNK
    n_pp = ncol // PROJ_PIECE
    n_tail = max(n_pp - n_chunks, 0)
    slot_of = [p if p < n_pp - n_tail else n_chunks + (2 * (p - (n_pp - n_tail))) // max(n_tail, 1)
               for p in range(n_pp)]

    def next_proj_pieces(slot):
        for p in range(n_pp):
            if min(slot_of[p], n_chunks + 1) == slot:
                cols = slice(p * PROJ_PIECE, (p + 1) * PROJ_PIECE)
                projn_ref[:, cols] = _dot(hn, win_ref[:, cols])

    for c in range(n_chunks):
        _hgrn2_chunk(c, proj_ref, y_ref, st_ref, g_ref, k_ref, lb, gnw, tril,
                     functools.partial(next_proj_pieces, c))

    pi = lax.broadcasted_iota(I32, (SG_CHUNK, SG_CHUNK), 0)
    pj = lax.broadcasted_iota(I32, (SG_CHUNK, SG_CHUNK), 1)
    sglg = sglg_ref[...]
    sglb = sglb_ref[...]
    for p in range(n_sg):
        rows = slice(p * SG_CHUNK, (p + 1) * SG_CHUNK)
        z = _gelu_exact(proj_ref[rows, 4 * hw:4 * hw + 2 * sgw])
        u = z[:, :sgw]
        vn = _layernorm(z[:, sgw:], sglg, sglb).astype(BF16)
        for gi in range(SG_GROUPS):
            cs = slice(gi * SG_CH, (gi + 1) * SG_CH)
            wc = jnp.where(pi >= pj, sgw_ref[gi], 0.0).astype(BF16)
            mixed = _dot(wc, vn[:, cs]) + sgbt_ref[:, gi:gi + 1]
            y_ref[rows, hw + gi * SG_CH:hw + (gi + 1) * SG_CH] = (u[:, cs] * mixed).astype(BF16)

    y = _dot(y_ref[...], wout_ref[...])
    next_proj_pieces(n_chunks)
    x1 = _layernorm(ALPHA * x + (1.0 + g1) * y, ln1g_ref[...], ln1b_ref[...])
    x1_ref[...] = x1
    h2 = x1 * (1.0 + sc2) + sh2

    h2_hi, h2_lo = _split_bf16(h2)
    packed = _pack_bf16_pairs(h2)
    for i, ref in enumerate(h2p_refs):
        ref[...] = packed[:, i * SC_ROW_WORDS:(i + 1) * SC_ROW_WORDS]

    nr = br_ref.shape[0]
    part = _dot_nt(wrs_ref[...], h2_hi)
    logits = part[:nr] + part[nr:] + _dot_nt(wrs_ref[:nr, :], h2_lo) + br_ref[:, 0:1]
    next_proj_pieces(n_chunks + 1)
    rt = _route_t(logits)
    routet_ref[...] = rt
    route_ref[...] = jnp.concatenate([rt, jnp.zeros((LANES - SUBLANES, ts), F32)], axis=0).T
    proj_ref[...] = projn_ref[...]


def _mix_call(x, ada, w_in_b, lb_logits, gnw, sglg, sglb, sg_w, sg_bt, w_out_b, ln1g, ln1b, wrs, br):
    bsz, s, d = x.shape
    ts = min(MIX_ROWS, s)
    nt = s // ts
    n_steps = bsz * nt
    ncol = w_in_b.shape[1]
    hw = HG_HEADS * HG_DK
    n_pieces = d // 2 // SC_ROW_WORDS
    const = lambda a: pl.BlockSpec(a.shape, lambda i: (0,) * a.ndim, pipeline_mode=pl.Buffered(1))
    cur = lambda i: (i // nt, i % nt, 0)
    nxt = lambda i: cur(jnp.minimum(i + 1, n_steps - 1))
    return pl.pallas_call(
        functools.partial(_mix_kernel, nt),
        grid=(n_steps,),
        in_specs=[pl.BlockSpec((None, ts, d), cur),
                  pl.BlockSpec((None, ts, d), nxt),
                  pl.BlockSpec((None,) + ada.shape[1:], lambda i: (cur(i)[0], 0, 0)),
                  pl.BlockSpec((None,) + ada.shape[1:], lambda i: (nxt(i)[0], 0, 0)),
                  const(w_in_b), const(lb_logits), const(gnw), const(sglg), const(sglb), const(sg_w),
                  const(sg_bt), const(w_out_b), const(ln1g), const(ln1b), const(wrs), const(br)],
        out_specs=[pl.BlockSpec((None, ts, d), cur),
                   pl.BlockSpec((None, ts, LANES), cur),
                   pl.BlockSpec((None, SUBLANES, ts), lambda i: (i // nt, 0, i % nt))]
                  + [pl.BlockSpec((None, ts, SC_ROW_WORDS), cur)] * n_pieces,
        out_shape=[jax.ShapeDtypeStruct((bsz, s, d), F32),
                   jax.ShapeDtypeStruct((bsz, s, LANES), F32),
                   jax.ShapeDtypeStruct((bsz, SUBLANES, s), F32)]
                  + [jax.ShapeDtypeStruct((bsz, s, SC_ROW_WORDS), U32)] * n_pieces,
        scratch_shapes=[pltpu.VMEM((HG_HEADS, HG_DK, HG_DK), F32),
                        pltpu.VMEM((ts, ncol), F32),
                        pltpu.VMEM((ts, ncol), F32),
                        pltpu.VMEM((ts, d), BF16),
                        pltpu.VMEM((HG_CHUNK, hw), F32),
                        pltpu.VMEM((HG_CHUNK, hw), F32)],
        compiler_params=pltpu.CompilerParams(
            dimension_semantics=("arbitrary",), vmem_limit_bytes=VMEM_LIMIT),
        name="mix",
    )(x, x, ada, ada, w_in_b, lb_logits, gnw, sglg, sglb, sg_w, sg_bt, w_out_b, ln1g, ln1b, wrs, br)


def _sort_kernel(nb, rt_ref, dest_ref, blk_ref, cnt_ref, pst_ref, carry_ref):
    ph = pl.program_id(0)
    first_step = (pl.program_id(1) == 0) & (pl.program_id(2) == 0)
    tk = rt_ref.shape[1]
    r = rt_ref[...]
    row = lax.broadcasted_iota(I32, (MOE_EXPERTS, tk), 0)
    oh1 = row == r[0:1].astype(I32)
    oh2 = row == r[1:2].astype(I32)
    hot = jnp.where(oh1 | oh2, 1.0, 0.0)

    @pl.when((ph == 0) & first_step)
    def _():
        cnt_ref[...] = jnp.zeros_like(cnt_ref)

    @pl.when(ph == 0)
    def _():
        cnt_ref[...] += jnp.sum(hot, axis=1, keepdims=True)

    @pl.when((ph == 1) & first_step)
    def _():
        nblk = jnp.floor((cnt_ref[...] + (MOE_ROWS - 1)) * (1.0 / MOE_ROWS))
        ei = lax.broadcasted_iota(I32, (MOE_EXPERTS, MOE_EXPERTS), 0)
        ej = lax.broadcasted_iota(I32, (MOE_EXPERTS, MOE_EXPERTS), 1)
        pstart = _dot((ej < ei).astype(BF16), nblk.astype(BF16))
        pst_ref[...] = pstart * float(MOE_ROWS)
        carry_ref[...] = jnp.zeros_like(carry_ref)
        lane_b = lax.broadcasted_iota(I32, blk_ref.shape, 1)
        blk_ref[...] = jnp.where(lane_b == 0, pstart, jnp.where(lane_b == 1, nblk, 0.0)).astype(I32)

    @pl.when(ph == 1)
    def _():
        ti = lax.broadcasted_iota(I32, (nb, nb), 0)
        tj = lax.broadcasted_iota(I32, (nb, nb), 1)
        before = (ti < tj).astype(BF16)
        carry = carry_ref[...] + pst_ref[...]
        rows1 = []
        rows2 = []
        for j in range(tk // nb):
            cs = slice(j * nb, (j + 1) * nb)
            hb = hot[:, cs]
            base = _dot(hb.astype(BF16), before) + carry[:, 0:1]
            rows1.append(jnp.sum(jnp.where(oh1[:, cs], base, 0.0), axis=0, keepdims=True))
            rows2.append(jnp.sum(jnp.where(oh2[:, cs], base, 0.0), axis=0, keepdims=True))
            carry = carry + jnp.sum(hb, axis=1, keepdims=True)
        d1 = jnp.concatenate(rows1, axis=1)
        d2 = jnp.concatenate(rows2, axis=1)
        out_row = lax.broadcasted_iota(I32, (SUBLANES, tk), 0)
        dest_ref[...] = jnp.where(out_row == 0, d1, jnp.where(out_row == 1, d2, 0.0)).astype(I32)
        carry_ref[...] = carry - pst_ref[...]


def _sort_call(route_t):
    bsz, _, s = route_t.shape
    tk = min(SORT_ROWS, s)
    nj = s // tk
    return pl.pallas_call(
        functools.partial(_sort_kernel, min(SORT_BLOCK, tk)),
        grid=(2, bsz, nj),
        in_specs=[pl.BlockSpec((None, SUBLANES, tk), lambda p, b, j: (b, 0, j))],
        out_specs=[pl.BlockSpec((SUBLANES, tk), lambda p, b, j: (0, (b * nj + j) * p)),
                   pl.BlockSpec((MOE_EXPERTS, LANES), lambda p, b, j: (0, 0))],
        out_shape=[jax.ShapeDtypeStruct((SUBLANES, bsz * s), I32),
                   jax.ShapeDtypeStruct((MOE_EXPERTS, LANES), I32)],
        scratch_shapes=[pltpu.VMEM((MOE_EXPERTS, LANES), F32), pltpu.VMEM((MOE_EXPERTS, LANES), F32),
                        pltpu.VMEM((MOE_EXPERTS, LANES), F32)],
        compiler_params=pltpu.CompilerParams(
            dimension_semantics=("arbitrary", "arbitrary", "arbitrary")),
        name="sort",
    )(route_t)


def _sc_mesh():
    return plsc.VectorSubcoreMesh(core_axis_name="c", subcore_axis_name="s")


def _sc_scatter_rows2(rows, idx_a, idx_b, n_out):
    n, w = rows.shape

    @pl.kernel(out_type=jax.ShapeDtypeStruct((n_out, w), rows.dtype), mesh=_sc_mesh(), scratch_types=[])
    def scatter(x_hbm, ia_hbm, ib_hbm, o_hbm):
        def body(x_vmem, ia_vmem, ib_vmem):
            pltpu.sync_copy(x_vmem, o_hbm.at[ia_vmem.at[0]])
            pltpu.sync_copy(x_vmem, o_hbm.at[ib_vmem.at[0]])

        pltpu.emit_pipeline(
            body,
            grid=(n // SC_WINDOW,),
            in_specs=[pl.BlockSpec((SC_WINDOW, w), lambda i: (i, 0)),
                      pl.BlockSpec((1, SC_WINDOW), lambda i: (0, i)),
                      pl.BlockSpec((1, SC_WINDOW), lambda i: (0, i))],
            out_specs=[],
            core_axis_name=("c", "s"),
            dimension_semantics=(pltpu.PARALLEL,),
        )(x_hbm, ia_hbm, ib_hbm)

    return scatter(rows, idx_a.reshape(1, n), idx_b.reshape(1, n))


def _sc_gather_rows(src, idx):
    n = idx.shape[0]
    w = src.shape[1]

    @pl.kernel(out_type=jax.ShapeDtypeStruct((n, w), src.dtype), mesh=_sc_mesh(), scratch_types=[])
    def gather(x_hbm, i_hbm, o_hbm):
        def body(i_vmem, o_vmem):
            pltpu.sync_copy(x_hbm.at[i_vmem.at[0]], o_vmem)

        pltpu.emit_pipeline(
            body,
            grid=(n // SC_WINDOW,),
            in_specs=[pl.BlockSpec((1, SC_WINDOW), lambda i: (0, i))],
            out_specs=[pl.BlockSpec((SC_WINDOW, w), lambda i: (i, 0))],
            core_axis_name=("c", "s"),
            dimension_semantics=(pltpu.PARALLEL,),
        )(i_hbm, o_hbm)

    return gather(src, idx.reshape(1, n))


def _moe_kernel(n_in, first_ref, count_ref, *refs):
    xb_hbm = refs[:n_in]
    wup_ref, wdn_ref = refs[n_in:n_in + 2]
    y_hbm = refs[n_in + 2:2 * n_in + 2]
    wupb_ref, wdnb_ref, xbuf, ybuf, in_sem, out_sem = refs[2 * n_in + 2:]
    e = pl.program_id(0)
    n_exp = pl.num_programs(0)
    first = first_ref[e]
    count = count_ref[e]
    total = first_ref[n_exp - 1] + count_ref[n_exp - 1]

    def rows_of(g):
        return pl.ds(pl.multiple_of(g * MOE_ROWS, MOE_ROWS), MOE_ROWS)

    def in_copy(g, i):
        slot = g % MOE_IN_BUFS
        return pltpu.make_async_copy(xb_hbm[i].at[rows_of(g)], xbuf.at[slot, i], in_sem.at[slot, i])

    def out_copy(g, i):
        slot = g % MOE_OUT_BUFS
        return pltpu.make_async_copy(ybuf.at[slot, i], y_hbm[i].at[rows_of(g)], out_sem.at[slot, i])

    @pl.when(e == 0)
    def _():
        for g in range(MOE_AHEAD):
            @pl.when(g < total)
            def _():
                for i in range(n_in):
                    in_copy(g, i).start()

    @pl.when(count > 0)
    def _():
        wupb_ref[...] = wup_ref[...].astype(BF16)
        wdnb_ref[...] = wdn_ref[...].astype(BF16)

        def run_blocks(gs):
            for g in gs:
                for i in range(n_in):
                    in_copy(g, i).wait()
            for g in gs:
                @pl.when(g + MOE_AHEAD < total)
                def _():
                    for i in range(n_in):
                        in_copy(g + MOE_AHEAD, i).start()
            for g in gs:
                @pl.when(g >= MOE_OUT_BUFS)
                def _():
                    for i in range(n_in):
                        out_copy(g - MOE_OUT_BUFS, i).wait()
            gus = []
            for g in gs:
                w = jnp.concatenate([xbuf[g % MOE_IN_BUFS, i] for i in range(n_in)], axis=1)
                gus.append(_dot(_unpack_bf16_pairs(w).astype(BF16), wupb_ref[...]))
            ys = []
            for gu in gus:
                gate = gu[:, :MOE_HIDDEN]
                act = (gate * _sigmoid(gate) * gu[:, MOE_HIDDEN:]).astype(BF16)
                ys.append(_pack_bf16_pairs(_dot(act, wdnb_ref[...])))
            for g, y in zip(gs, ys):
                for i in range(n_in):
                    ybuf[g % MOE_OUT_BUFS, i] = y[:, i * SC_ROW_WORDS:(i + 1) * SC_ROW_WORDS]
                    out_copy(g, i).start()

        def pair(p, carry):
            g = first + MOE_GROUP * p
            run_blocks([g + k for k in range(MOE_GROUP)])
            return carry

        lax.fori_loop(0, count // MOE_GROUP, pair, 0)
        for k in range(MOE_GROUP - 1):
            @pl.when(count % MOE_GROUP > k)
            def _():
                run_blocks([first + (count // MOE_GROUP) * MOE_GROUP + k])

    @pl.when(e == n_exp - 1)
    def _():
        for k in range(MOE_OUT_BUFS):
            @pl.when(total - 1 - k >= 0)
            def _():
                for i in range(n_in):
                    out_copy(total - 1 - k, i).wait()


def _moe_call(first_block, block_count, xb_pieces, w_up, w_down):
    n_rows = xb_pieces[0].shape[0]
    n_in = len(xb_pieces)
    d = 2 * n_in * SC_ROW_WORDS
    n_exp, _, hid2 = w_up.shape
    hid = w_down.shape[1]
    hbm = pl.BlockSpec(memory_space=pl.ANY)
    buf = lambda n: pltpu.VMEM((n, n_in, MOE_ROWS, SC_ROW_WORDS), U32)
    return pl.pallas_call(
        functools.partial(_moe_kernel, n_in),
        grid_spec=pltpu.PrefetchScalarGridSpec(
            num_scalar_prefetch=2,
            grid=(n_exp,),
            in_specs=[hbm] * n_in + [
                      pl.BlockSpec((None, d, hid2), lambda e, fb, bc: (e, 0, 0)),
                      pl.BlockSpec((None, hid, d), lambda e, fb, bc: (e, 0, 0))],
            out_specs=[hbm] * n_in,
            scratch_shapes=[pltpu.VMEM((d, hid2), BF16), pltpu.VMEM((hid, d), BF16),
                            buf(MOE_IN_BUFS), buf(MOE_OUT_BUFS),
                            pltpu.SemaphoreType.DMA((MOE_IN_BUFS, n_in)),
                            pltpu.SemaphoreType.DMA((MOE_OUT_BUFS, n_in))]),
        out_shape=[jax.ShapeDtypeStruct((n_rows, SC_ROW_WORDS), U32)] * n_in,
        compiler_params=pltpu.CompilerParams(
            dimension_semantics=("arbitrary",), vmem_limit_bytes=VMEM_LIMIT),
        name="moe",
    )(first_block, block_count, *xb_pieces, w_up, w_down)


def _combine_kernel(n_pieces, x1_ref, route_ref, ada_ref, g_ref, b_ref, *refs):
    o_ref = refs[-1]
    ya = _unpack_bf16_pairs(jnp.concatenate([r[...] for r in refs[:n_pieces]], axis=1))
    yb = _unpack_bf16_pairs(jnp.concatenate([r[...] for r in refs[n_pieces:2 * n_pieces]], axis=1))
    r = route_ref[...]
    m = ya * r[:, 2:3] + yb * r[:, 3:4]
    g2 = ada_ref[5:6]
    o_ref[...] = _layernorm(ALPHA * x1_ref[...] + (1.0 + g2) * m, g_ref[...], b_ref[...])


def _combine_call(x1, y_pieces, route, ada, ln2g, ln2b):
    bsz, s, d = x1.shape
    ts = min(OUT_ROWS, s)
    nj = s // ts
    n_pieces = len(y_pieces)
    slot = lambda k: pl.BlockSpec((ts, SC_ROW_WORDS), lambda b, j: (k * bsz * nj + b * nj + j, 0))
    return pl.pallas_call(
        functools.partial(_combine_kernel, n_pieces),
        grid=(bsz, nj),
        in_specs=[pl.BlockSpec((None, ts, d), lambda b, j: (b, j, 0)),
                  pl.BlockSpec((None, ts, LANES), lambda b, j: (b, j, 0)),
                  pl.BlockSpec((None,) + ada.shape[1:], lambda b, j: (b, 0, 0)),
                  pl.BlockSpec((1, d), lambda b, j: (0, 0)),
                  pl.BlockSpec((1, d), lambda b, j: (0, 0))]
                 + [slot(0)] * n_pieces + [slot(1)] * n_pieces,
        out_specs=pl.BlockSpec((None, ts, d), lambda b, j: (b, j, 0)),
        out_shape=jax.ShapeDtypeStruct((bsz, s, d), F32),
        compiler_params=pltpu.CompilerParams(dimension_semantics=("arbitrary", "arbitrary")),
        name="combine",
    )(x1, route, ada, ln2g, ln2b, *y_pieces, *y_pieces)


def kernel(x, c, w_ada, b_ada, w_in, lb_logits, hg_norm_w, sg_ln_g, sg_ln_b, sg_w, sg_b, w_out, ln1_g, ln1_b, router_group_w, router_group_b, router_expert_w, router_expert_b, w_up, w_down, ln2_g, ln2_b):
    assert w_in.shape[0] == DEPTH
    bsz, s, d = x.shape
    t = bsz * s
    l = 0

    ada = _ada_call(c, w_ada[l], b_ada[l])

    wr = jnp.concatenate(
        [router_group_w[l].T, router_expert_w[l].transpose(0, 2, 1).reshape(MOE_EXPERTS, d)], axis=0)
    wr = jnp.pad(wr, ((0, ROUTE_ROWS - wr.shape[0]), (0, 0)))
    br = jnp.concatenate([router_group_b[l], router_expert_b[l].reshape(MOE_EXPERTS)])
    br = jnp.broadcast_to(jnp.pad(br, (0, ROUTE_ROWS - br.shape[0]))[:, None], (ROUTE_ROWS, LANES))
    wrs = jnp.concatenate(_split_bf16(wr), axis=0)

    x1, route, route_t, *h2p = _mix_call(
        x, ada, w_in[l].astype(BF16), lb_logits, hg_norm_w[l].reshape(1, -1),
        sg_ln_g[l].reshape(1, -1), sg_ln_b[l].reshape(1, -1), sg_w[l], sg_b[l].T,
        w_out[l].astype(BF16), ln1_g[l].reshape(1, d), ln1_b[l].reshape(1, d), wrs, br)

    n_blocks = -(-(2 * t) // MOE_ROWS) + MOE_EXPERTS
    n_rows = n_blocks * MOE_ROWS
    dest, blk = _sort_call(route_t)
    d0, d1 = dest[0], dest[1]

    xb = [_sc_scatter_rows2(p.reshape(t, SC_ROW_WORDS), d0, d1, n_rows) for p in h2p]
    yb = _moe_call(blk[:, 0], blk[:, 1], xb, w_up[l], w_down[l])
    d01 = jnp.concatenate([d0, d1])
    y2 = [_sc_gather_rows(p, d01) for p in yb]
    return _combine_call(x1, y2, route, ada, ln2_g[l].reshape(1, d), ln2_b[l].reshape(1, d))
```

```python
import functools

import jax
import jax.numpy as jnp
from jax import lax
from jax.experimental import pallas as pl
from jax.experimental.pallas import tpu as pltpu
from jax.experimental.pallas import tpu_sc as plsc

F32 = jnp.float32
BF16 = jnp.bfloat16
I32 = jnp.int32
U32 = jnp.uint32

HG_HEADS = 4
HG_DK = 128
HG_CHUNK = 64
HG_SUB = 8
SG_GROUPS = 4
SG_CH = 128
SG_CHUNK = 128
MOE_GROUPS = 4
MOE_EPG = 8
MOE_EXPERTS = MOE_GROUPS * MOE_EPG
MOE_HIDDEN = 512
DEPTH = 1
ALPHA = (2.0 * DEPTH) ** 0.25
LN_EPS = 1e-5
RMS_EPS = 1e-6
LOG2E = 1.4426950408889634

LANES = 128
SUBLANES = 8
SC_WINDOW = 128
SC_ROW_WORDS = 256

MIX_ROWS = 512
PROJ_PIECE = 256
SORT_ROWS = 1024
SORT_BLOCK = 256
ROUTE_ROWS = 40
MOE_ROWS = 256
MOE_TALL = 4
MOE_AHEAD = 4
MOE_IN_BUFS = MOE_AHEAD + MOE_TALL
MOE_OUT_BUFS = 2 * MOE_TALL
OUT_ROWS = 512
VMEM_LIMIT = 52 * 1024 * 1024


def _dot(a, b):
    return jnp.dot(a, b, preferred_element_type=F32)


def _dot_nt(a, b):
    return lax.dot_general(a, b, (((1,), (1,)), ((), ())), preferred_element_type=F32)


def _dot_tn(a, b):
    return lax.dot_general(a, b, (((0,), (0,)), ((), ())), preferred_element_type=F32)


def _sigmoid(x):
    return jax.nn.sigmoid(x)


def _gelu_exact(x):
    return 0.5 * x * (1.0 + lax.erf(x * (2.0 ** -0.5)))


def _layernorm(x, g, b):
    mu = jnp.mean(x, axis=-1, keepdims=True)
    xc = x - mu
    var = jnp.mean(xc * xc, axis=-1, keepdims=True)
    return xc * lax.rsqrt(var + LN_EPS) * g + b


def _pack_bf16_pairs(x):
    n = x.shape[1]
    bits = lax.bitcast_convert_type(x.astype(BF16).astype(F32), U32)
    return bits[:, n // 2:] | (bits[:, :n // 2] >> 16)


def _unpack_bf16_pairs(w):
    lo = lax.bitcast_convert_type(w << 16, F32)
    hi = lax.bitcast_convert_type(w & jnp.uint32(0xFFFF0000), F32)
    return jnp.concatenate([lo, hi], axis=1)


def _split_bf16(x):
    hi = x.astype(BF16)
    lo = (x - hi.astype(F32)).astype(BF16)
    return hi, lo


def _ada_kernel(c_ref, w_ref, b_ref, o_ref):
    c = c_ref[...]
    ca = c * _sigmoid(c)
    o_ref[...] = jnp.dot(ca, w_ref[...], preferred_element_type=F32,
                         precision=lax.Precision.HIGHEST) + b_ref[...]


def _ada_call(c, w_ada, b_ada):
    bsz, d = c.shape
    n = w_ada.shape[1]
    rows = -(-bsz // SUBLANES) * SUBLANES
    c_pad = jnp.pad(c, ((0, rows - bsz), (0, 0)))
    out = pl.pallas_call(
        _ada_kernel,
        grid=(n // d,),
        in_specs=[pl.BlockSpec((rows, d), lambda i: (0, 0)),
                  pl.BlockSpec((d, d), lambda i: (0, i)),
                  pl.BlockSpec((1, d), lambda i: (0, i))],
        out_specs=pl.BlockSpec((rows, d), lambda i: (0, i)),
        out_shape=jax.ShapeDtypeStruct((rows, n), F32),
        name="ada",
    )(c_pad, w_ada, b_ada.reshape(1, n))
    return out[:bsz].reshape(bsz, n // d, d)


def _hgrn2_matmul_phase(c, proj_ref, st_ref, g_ref, h_ref, lb, tril):
    hw = HG_HEADS * HG_DK
    cl = HG_CHUNK
    rows = pl.ds(c * cl, cl)
    qz = proj_ref[rows, 0:hw]
    fz = proj_ref[rows, hw:2 * hw]
    v = proj_ref[rows, 2 * hw:3 * hw]

    q = qz * _sigmoid(qz)
    f = lb + (1.0 - lb) * _sigmoid(fz)
    lf = jnp.log(f)
    k = 1.0 - f
    lf_hi, lf_lo = _split_bf16(lf)
    g = (_dot(tril, lf_hi) + _dot(tril, lf_lo)) * LOG2E
    glast = g[cl - 1:cl, :]
    qg = (q * jnp.exp2(g)).astype(BF16)
    kd = (k * jnp.exp2(glast - g)).astype(BF16)
    vb = v.astype(BF16)
    g_ref[...] = g
    h_ref[...] = g - jnp.log(k) * LOG2E

    n_sub = cl // HG_SUB
    o_state = []
    acc_all = []
    for hd in range(HG_HEADS):
        cs = slice(hd * HG_DK, (hd + 1) * HG_DK)
        st = st_ref[hd]
        o_state.append(_dot_nt(qg[:, cs], st.astype(BF16)))
        st_ref[hd] = st * jnp.exp2(glast[:, cs]) + _dot_tn(vb[:, cs], kd[:, cs])
        gh = g[:, cs]
        kh = k[:, cs]
        qh = q[:, cs]
        acc = [None] * n_sub
        w = cl // 2
        while w >= HG_SUB:
            for p in range(0, cl, 2 * w):
                ref = g_ref[p + w - 1:p + w, cs]
                qt = (qh[p + w:p + 2 * w] * jnp.exp2(gh[p + w:p + 2 * w] - ref)).astype(BF16)
                kt = (kh[p:p + w] * jnp.exp2(ref - gh[p:p + w])).astype(BF16)
                pieces = [kt]
                if p:
                    pieces.insert(0, jnp.zeros((p, HG_DK), BF16))
                if cl - p - w:
                    pieces.append(jnp.zeros((cl - p - w, HG_DK), BF16))
                blk = _dot_nt(qt, jnp.concatenate(pieces, axis=0))
                for r in range(w // HG_SUB):
                    i = (p + w) // HG_SUB + r
                    part = blk[r * HG_SUB:(r + 1) * HG_SUB]
                    acc[i] = part if acc[i] is None else acc[i] + part
            w //= 2
        acc_all.append(acc)
    return q, g, vb, o_state, acc_all


def _hgrn2_diag_phase(c, phase1, proj_ref, y_ref, h_ref, gnw):
    q, g, vb, o_state, acc_all = phase1
    hw = HG_HEADS * HG_DK
    cl = HG_CHUNK
    rows = pl.ds(c * cl, cl)
    og = proj_ref[rows, 3 * hw:4 * hw]
    n_sub = cl // HG_SUB
    lane_c = lax.broadcasted_iota(I32, (HG_SUB, cl), 1)
    trow = lax.broadcasted_iota(I32, (HG_SUB, cl), 0)
    for hd in range(HG_HEADS):
        cs = slice(hd * HG_DK, (hd + 1) * HG_DK)
        gh = g[:, cs]
        qh = q[:, cs]
        a_rows = []
        for i in range(n_sub):
            b0 = i * HG_SUB
            gb = gh[b0:b0 + HG_SUB]
            qb = qh[b0:b0 + HG_SUB]
            a = jnp.zeros((HG_SUB, cl), F32) if acc_all[hd][i] is None else acc_all[hd][i]
            for s in range(HG_SUB):
                hs = h_ref[b0 + s:b0 + s + 1, cs]
                col = jnp.sum(qb * jnp.exp2(gb - hs), axis=-1, keepdims=True)
                a = jnp.where(lane_c == b0 + s, col, a)
            a_rows.append(jnp.where(lane_c <= b0 + trow, a, 0.0))
        amat = jnp.concatenate(a_rows, axis=0).astype(BF16)
        o = o_state[hd] + _dot(amat, vb[:, cs])
        ms = jnp.mean(o * o, axis=-1, keepdims=True)
        ogh = og[:, cs]
        ya = o * lax.rsqrt(ms + RMS_EPS) * gnw * (ogh * _sigmoid(ogh))
        y_ref[rows, cs] = ya.astype(BF16)


def _route_t(logits):
    nr, ts = logits.shape
    row = lax.broadcasted_iota(I32, (nr, ts), 0)
    neg = -jnp.inf
    gl = jnp.where(row < MOE_GROUPS, logits, neg)
    gmax = jnp.max(gl, axis=0, keepdims=True)
    gidx = jnp.min(jnp.where(gl == gmax, row, nr), axis=0, keepdims=True)
    p_group = 1.0 / jnp.sum(jnp.exp(gl - gmax), axis=0, keepdims=True)
    e_lo = MOE_GROUPS + gidx * MOE_EPG
    el = jnp.where((row >= e_lo) & (row < e_lo + MOE_EPG), logits, neg)
    m1 = jnp.max(el, axis=0, keepdims=True)
    i1 = jnp.min(jnp.where(el == m1, row, nr), axis=0, keepdims=True)
    el2 = jnp.where(row == i1, neg, el)
    m2 = jnp.max(el2, axis=0, keepdims=True)
    i2 = jnp.min(jnp.where(el2 == m2, row, nr), axis=0, keepdims=True)
    esum = jnp.sum(jnp.exp(el - m1), axis=0, keepdims=True)
    p1 = 1.0 / esum
    p2 = jnp.exp(m2 - m1) / esum
    den = p1 + p2
    w1 = p_group * p1 / den
    w2 = p_group * p2 / den
    e1 = (i1 - MOE_GROUPS).astype(F32)
    e2 = (i2 - MOE_GROUPS).astype(F32)
    out_row = lax.broadcasted_iota(I32, (SUBLANES, ts), 0)
    return jnp.where(out_row == 0, e1, jnp.where(out_row == 1, e2,
                     jnp.where(out_row == 2, w1, jnp.where(out_row == 3, w2, 0.0))))


def _mix_kernel(nt, x_ref, xn_ref, ada_ref, adan_ref, win_ref, lbl_ref, gnw_ref, sglg_ref, sglb_ref,
                sgw_ref, sgbt_ref, wout_ref, ln1g_ref, ln1b_ref, wrs_ref, br_ref,
                x1_ref, route_ref, routet_ref, *rest):
    h2p_refs = rest[:-6]
    st_ref, proj_ref, projn_ref, y_ref, g_ref, h_ref = rest[-6:]
    ts, d = x_ref.shape
    hw = HG_HEADS * HG_DK
    sgw = SG_GROUPS * SG_CH
    ncol = proj_ref.shape[1]
    step = pl.program_id(0)

    def modulated(xr, adar):
        a = adar[...]
        return (xr[...] * (1.0 + a[1:2]) + a[0:1]).astype(BF16)

    @pl.when(step % nt == 0)
    def _():
        st_ref[...] = jnp.zeros_like(st_ref)

    @pl.when(step == 0)
    def _():
        proj_ref[...] = _dot(modulated(x_ref, ada_ref), win_ref[...])

    x = x_ref[...]
    ada = ada_ref[...]
    g1, sh2, sc2 = ada[2:3], ada[3:4], ada[4:5]
    hn = modulated(xn_ref, adan_ref)

    lbl = lbl_ref[...]
    slots = [lbl[i:i + 1] for i in range(lbl.shape[0])]
    mx = functools.reduce(jnp.maximum, slots)
    ex = [jnp.exp(s - mx) for s in slots]
    lb = ex[0] / functools.reduce(lambda a, b: a + b, ex)

    ci = lax.broadcasted_iota(I32, (HG_CHUNK, HG_CHUNK), 0)
    cj = lax.broadcasted_iota(I32, (HG_CHUNK, HG_CHUNK), 1)
    tril = (ci >= cj).astype(BF16)
    gnw = gnw_ref[...]

    n_chunks = ts // HG_CHUNK
    n_sg = ts // SG_CHUNK
    n_pp = ncol // PROJ_PIECE
    n_tail = max(n_pp - n_chunks, 0)
    slot_of = [p if p < n_pp - n_tail else n_chunks + (2 * (p - (n_pp - n_tail))) // max(n_tail, 1)
               for p in range(n_pp)]

    def next_proj_pieces(slot):
        for p in range(n_pp):
            if min(slot_of[p], n_chunks + 1) == slot:
                cols = slice(p * PROJ_PIECE, (p + 1) * PROJ_PIECE)
                projn_ref[:, cols] = _dot(hn, win_ref[:, cols])

    phase1 = _hgrn2_matmul_phase(0, proj_ref, st_ref, g_ref.at[0], h_ref.at[0], lb, tril)
    for c in range(n_chunks):
        next_proj_pieces(c)
        if c + 1 < n_chunks:
            nxt = (c + 1) % 2
            phase1_next = _hgrn2_matmul_phase(c + 1, proj_ref, st_ref, g_ref.at[nxt], h_ref.at[nxt],
                                              lb, tril)
        _hgrn2_diag_phase(c, phase1, proj_ref, y_ref, h_ref.at[c % 2], gnw)
        phase1 = phase1_next

    pi = lax.broadcasted_iota(I32, (SG_CHUNK, SG_CHUNK), 0)
    pj = lax.broadcasted_iota(I32, (SG_CHUNK, SG_CHUNK), 1)
    sglg = sglg_ref[...]
    sglb = sglb_ref[...]
    for p in range(n_sg):
        rows = slice(p * SG_CHUNK, (p + 1) * SG_CHUNK)
        z = _gelu_exact(proj_ref[rows, 4 * hw:4 * hw + 2 * sgw])
        u = z[:, :sgw]
        vn = _layernorm(z[:, sgw:], sglg, sglb).astype(BF16)
        for gi in range(SG_GROUPS):
            cs = slice(gi * SG_CH, (gi + 1) * SG_CH)
            wc = jnp.where(pi >= pj, sgw_ref[gi], 0.0).astype(BF16)
            mixed = _dot(wc, vn[:, cs]) + sgbt_ref[:, gi:gi + 1]
            y_ref[rows, hw + gi * SG_CH:hw + (gi + 1) * SG_CH] = (u[:, cs] * mixed).astype(BF16)

    y = _dot(y_ref[...], wout_ref[...])
    next_proj_pieces(n_chunks)
    x1 = _layernorm(ALPHA * x + (1.0 + g1) * y, ln1g_ref[...], ln1b_ref[...])
    x1_ref[...] = x1
    h2 = x1 * (1.0 + sc2) + sh2

    h2_hi, h2_lo = _split_bf16(h2)
    packed = _pack_bf16_pairs(h2)
    for i, ref in enumerate(h2p_refs):
        ref[...] = packed[:, i * SC_ROW_WORDS:(i + 1) * SC_ROW_WORDS]

    nr = br_ref.shape[0]
    part = _dot_nt(wrs_ref[...], h2_hi)
    logits = part[:nr] + part[nr:] + _dot_nt(wrs_ref[:nr, :], h2_lo) + br_ref[:, 0:1]
    next_proj_pieces(n_chunks + 1)
    rt = _route_t(logits)
    routet_ref[...] = rt
    route_ref[...] = jnp.concatenate([rt, jnp.zeros((LANES - SUBLANES, ts), F32)], axis=0).T
    proj_ref[...] = projn_ref[...]


def _mix_call(x, ada, w_in_b, lb_logits, gnw, sglg, sglb, sg_w, sg_bt, w_out_b, ln1g, ln1b, wrs, br):
    bsz, s, d = x.shape
    ts = min(MIX_ROWS, s)
    nt = s // ts
    n_steps = bsz * nt
    ncol = w_in_b.shape[1]
    hw = HG_HEADS * HG_DK
    n_pieces = d // 2 // SC_ROW_WORDS
    const = lambda a: pl.BlockSpec(a.shape, lambda i: (0,) * a.ndim, pipeline_mode=pl.Buffered(1))
    cur = lambda i: (i // nt, i % nt, 0)
    nxt = lambda i: cur(jnp.minimum(i + 1, n_steps - 1))
    return pl.pallas_call(
        functools.partial(_mix_kernel, nt),
        grid=(n_steps,),
        in_specs=[pl.BlockSpec((None, ts, d), cur),
                  pl.BlockSpec((None, ts, d), nxt),
                  pl.BlockSpec((None,) + ada.shape[1:], lambda i: (cur(i)[0], 0, 0)),
                  pl.BlockSpec((None,) + ada.shape[1:], lambda i: (nxt(i)[0], 0, 0)),
                  const(w_in_b), const(lb_logits), const(gnw), const(sglg), const(sglb), const(sg_w),
                  const(sg_bt), const(w_out_b), const(ln1g), const(ln1b), const(wrs), const(br)],
        out_specs=[pl.BlockSpec((None, ts, d), cur),
                   pl.BlockSpec((None, ts, LANES), cur),
                   pl.BlockSpec((None, SUBLANES, ts), lambda i: (i // nt, 0, i % nt))]
                  + [pl.BlockSpec((None, ts, SC_ROW_WORDS), cur)] * n_pieces,
        out_shape=[jax.ShapeDtypeStruct((bsz, s, d), F32),
                   jax.ShapeDtypeStruct((bsz, s, LANES), F32),
                   jax.ShapeDtypeStruct((bsz, SUBLANES, s), F32)]
                  + [jax.ShapeDtypeStruct((bsz, s, SC_ROW_WORDS), U32)] * n_pieces,
        scratch_shapes=[pltpu.VMEM((HG_HEADS, HG_DK, HG_DK), F32),
                        pltpu.VMEM((ts, ncol), F32),
                        pltpu.VMEM((ts, ncol), F32),
                        pltpu.VMEM((ts, d), BF16),
                        pltpu.VMEM((2, HG_CHUNK, hw), F32),
                        pltpu.VMEM((2, HG_CHUNK, hw), F32)],
        compiler_params=pltpu.CompilerParams(
            dimension_semantics=("arbitrary",), vmem_limit_bytes=VMEM_LIMIT),
        name="mix",
    )(x, x, ada, ada, w_in_b, lb_logits, gnw, sglg, sglb, sg_w, sg_bt, w_out_b, ln1g, ln1b, wrs, br)


def _sort_kernel(nb, rt_ref, dest_ref, blk_ref, cnt_ref, pst_ref, carry_ref):
    ph = pl.program_id(0)
    first_step = (pl.program_id(1) == 0) & (pl.program_id(2) == 0)
    tk = rt_ref.shape[1]
    r = rt_ref[...]
    row = lax.broadcasted_iota(I32, (MOE_EXPERTS, tk), 0)
    oh1 = row == r[0:1].astype(I32)
    oh2 = row == r[1:2].astype(I32)
    hot = jnp.where(oh1 | oh2, 1.0, 0.0)

    @pl.when((ph == 0) & first_step)
    def _():
        cnt_ref[...] = jnp.zeros_like(cnt_ref)

    @pl.when(ph == 0)
    def _():
        cnt_ref[...] += jnp.sum(hot, axis=1, keepdims=True)

    @pl.when((ph == 1) & first_step)
    def _():
        nblk = jnp.floor((cnt_ref[...] + (MOE_ROWS - 1)) * (1.0 / MOE_ROWS))
        ei = lax.broadcasted_iota(I32, (MOE_EXPERTS, MOE_EXPERTS), 0)
        ej = lax.broadcasted_iota(I32, (MOE_EXPERTS, MOE_EXPERTS), 1)
        pstart = _dot((ej < ei).astype(BF16), nblk.astype(BF16))
        pst_ref[...] = pstart * float(MOE_ROWS)
        carry_ref[...] = jnp.zeros_like(carry_ref)
        lane_b = lax.broadcasted_iota(I32, blk_ref.shape, 1)
        blk_ref[...] = jnp.where(lane_b == 0, pstart, jnp.where(lane_b == 1, nblk, 0.0)).astype(I32)

    @pl.when(ph == 1)
    def _():
        ti = lax.broadcasted_iota(I32, (nb, nb), 0)
        tj = lax.broadcasted_iota(I32, (nb, nb), 1)
        before = (ti < tj).astype(BF16)
        carry = carry_ref[...] + pst_ref[...]
        rows1 = []
        rows2 = []
        for j in range(tk // nb):
            cs = slice(j * nb, (j + 1) * nb)
            hb = hot[:, cs]
            base = _dot(hb.astype(BF16), before) + carry[:, 0:1]
            rows1.append(jnp.sum(jnp.where(oh1[:, cs], base, 0.0), axis=0, keepdims=True))
            rows2.append(jnp.sum(jnp.where(oh2[:, cs], base, 0.0), axis=0, keepdims=True))
            carry = carry + jnp.sum(hb, axis=1, keepdims=True)
        d1 = jnp.concatenate(rows1, axis=1)
        d2 = jnp.concatenate(rows2, axis=1)
        out_row = lax.broadcasted_iota(I32, (SUBLANES, tk), 0)
        dest_ref[...] = jnp.where(out_row == 0, d1, jnp.where(out_row == 1, d2, 0.0)).astype(I32)
        carry_ref[...] = carry - pst_ref[...]


def _sort_call(route_t):
    bsz, _, s = route_t.shape
    tk = min(SORT_ROWS, s)
    nj = s // tk
    return pl.pallas_call(
        functools.partial(_sort_kernel, min(SORT_BLOCK, tk)),
        grid=(2, bsz, nj),
        in_specs=[pl.BlockSpec((None, SUBLANES, tk), lambda p, b, j: (b, 0, j))],
        out_specs=[pl.BlockSpec((SUBLANES, tk), lambda p, b, j: (0, (b * nj + j) * p)),
                   pl.BlockSpec((MOE_EXPERTS, LANES), lambda p, b, j: (0, 0))],
        out_shape=[jax.ShapeDtypeStruct((SUBLANES, bsz * s), I32),
                   jax.ShapeDtypeStruct((MOE_EXPERTS, LANES), I32)],
        scratch_shapes=[pltpu.VMEM((MOE_EXPERTS, LANES), F32), pltpu.VMEM((MOE_EXPERTS, LANES), F32),
                        pltpu.VMEM((MOE_EXPERTS, LANES), F32)],
        compiler_params=pltpu.CompilerParams(
            dimension_semantics=("arbitrary", "arbitrary", "arbitrary")),
        name="sort",
    )(route_t)


def _sc_mesh():
    return plsc.VectorSubcoreMesh(core_axis_name="c", subcore_axis_name="s")


def _sc_scatter_rows2(rows, idx_a, idx_b, n_out):
    n, w = rows.shape

    @pl.kernel(out_type=jax.ShapeDtypeStruct((n_out, w), rows.dtype), mesh=_sc_mesh(), scratch_types=[])
    def scatter(x_hbm, ia_hbm, ib_hbm, o_hbm):
        def body(x_vmem, ia_vmem, ib_vmem):
            pltpu.sync_copy(x_vmem, o_hbm.at[ia_vmem.at[0]])
            pltpu.sync_copy(x_vmem, o_hbm.at[ib_vmem.at[0]])

        pltpu.emit_pipeline(
            body,
            grid=(n // SC_WINDOW,),
            in_specs=[pl.BlockSpec((SC_WINDOW, w), lambda i: (i, 0)),
                      pl.BlockSpec((1, SC_WINDOW), lambda i: (0, i)),
                      pl.BlockSpec((1, SC_WINDOW), lambda i: (0, i))],
            out_specs=[],
            core_axis_name=("c", "s"),
            dimension_semantics=(pltpu.PARALLEL,),
        )(x_hbm, ia_hbm, ib_hbm)

    return scatter(rows, idx_a.reshape(1, n), idx_b.reshape(1, n))


def _sc_gather_rows(src, idx):
    n = idx.shape[0]
    w = src.shape[1]

    @pl.kernel(out_type=jax.ShapeDtypeStruct((n, w), src.dtype), mesh=_sc_mesh(), scratch_types=[])
    def gather(x_hbm, i_hbm, o_hbm):
        def body(i_vmem, o_vmem):
            pltpu.sync_copy(x_hbm.at[i_vmem.at[0]], o_vmem)

        pltpu.emit_pipeline(
            body,
            grid=(n // SC_WINDOW,),
            in_specs=[pl.BlockSpec((1, SC_WINDOW), lambda i: (0, i))],
            out_specs=[pl.BlockSpec((SC_WINDOW, w), lambda i: (i, 0))],
            core_axis_name=("c", "s"),
            dimension_semantics=(pltpu.PARALLEL,),
        )(i_hbm, o_hbm)

    return gather(src, idx.reshape(1, n))


def _moe_kernel(n_in, first_ref, count_ref, *refs):
    xb_hbm = refs[:n_in]
    wup_ref, wdn_ref = refs[n_in:n_in + 2]
    y_hbm = refs[n_in + 2:2 * n_in + 2]
    wupb_ref, wdnb_ref, xbuf, ybuf, in_sem, out_sem = refs[2 * n_in + 2:]
    e = pl.program_id(0)
    n_exp = pl.num_programs(0)
    first = first_ref[e]
    count = count_ref[e]
    total = first_ref[n_exp - 1] + count_ref[n_exp - 1]

    def rows_of(g):
        return pl.ds(pl.multiple_of(g * MOE_ROWS, MOE_ROWS), MOE_ROWS)

    def in_copy(g, i):
        slot = g % MOE_IN_BUFS
        return pltpu.make_async_copy(xb_hbm[i].at[rows_of(g)], xbuf.at[slot, i], in_sem.at[slot, i])

    def out_copy(g, i):
        slot = g % MOE_OUT_BUFS
        return pltpu.make_async_copy(ybuf.at[slot, i], y_hbm[i].at[rows_of(g)], out_sem.at[slot, i])

    @pl.when(e == 0)
    def _():
        for g in range(MOE_AHEAD):
            @pl.when(g < total)
            def _():
                for i in range(n_in):
                    in_copy(g, i).start()

    @pl.when(count > 0)
    def _():
        wupb_ref[...] = wup_ref[...].astype(BF16)
        wdnb_ref[...] = wdn_ref[...].astype(BF16)

        def run_blocks(gs):
            for g in gs:
                for i in range(n_in):
                    in_copy(g, i).wait()
            for g in gs:
                @pl.when(g + MOE_AHEAD < total)
                def _():
                    for i in range(n_in):
                        in_copy(g + MOE_AHEAD, i).start()
            for g in gs:
                @pl.when(g >= MOE_OUT_BUFS)
                def _():
                    for i in range(n_in):
                        out_copy(g - MOE_OUT_BUFS, i).wait()
            w = jnp.concatenate(
                [jnp.concatenate([xbuf[g % MOE_IN_BUFS, i] for i in range(n_in)], axis=1)
                 for g in gs], axis=0)
            gu = _dot(_unpack_bf16_pairs(w).astype(BF16), wupb_ref[...])
            gate = gu[:, :MOE_HIDDEN]
            act = (gate * _sigmoid(gate) * gu[:, MOE_HIDDEN:]).astype(BF16)
            y = _pack_bf16_pairs(_dot(act, wdnb_ref[...]))
            for k, g in enumerate(gs):
                for i in range(n_in):
                    ybuf[g % MOE_OUT_BUFS, i] = y[k * MOE_ROWS:(k + 1) * MOE_ROWS,
                                                  i * SC_ROW_WORDS:(i + 1) * SC_ROW_WORDS]
                    out_copy(g, i).start()

        def full_group(p, carry):
            g = first + MOE_TALL * p
            run_blocks([g + k for k in range(MOE_TALL)])
            return carry

        n_full = count // MOE_TALL
        lax.fori_loop(0, n_full, full_group, 0)
        done = n_full * MOE_TALL
        size = MOE_TALL // 2
        while size >= 1:
            start = first + done + ((count - done) // (2 * size)) * (2 * size)

            @pl.when(((count - done) // size) % 2 == 1)
            def _(start=start, size=size):
                run_blocks([start + k for k in range(size)])

            size //= 2

    @pl.when(e == n_exp - 1)
    def _():
        for k in range(MOE_OUT_BUFS):
            @pl.when(total - 1 - k >= 0)
            def _():
                for i in range(n_in):
                    out_copy(total - 1 - k, i).wait()


def _moe_call(first_block, block_count, xb_pieces, w_up, w_down):
    n_rows = xb_pieces[0].shape[0]
    n_in = len(xb_pieces)
    d = 2 * n_in * SC_ROW_WORDS
    n_exp, _, hid2 = w_up.shape
    hid = w_down.shape[1]
    hbm = pl.BlockSpec(memory_space=pl.ANY)
    buf = lambda n: pltpu.VMEM((n, n_in, MOE_ROWS, SC_ROW_WORDS), U32)
    return pl.pallas_call(
        functools.partial(_moe_kernel, n_in),
        grid_spec=pltpu.PrefetchScalarGridSpec(
            num_scalar_prefetch=2,
            grid=(n_exp,),
            in_specs=[hbm] * n_in + [
                      pl.BlockSpec((None, d, hid2), lambda e, fb, bc: (e, 0, 0)),
                      pl.BlockSpec((None, hid, d), lambda e, fb, bc: (e, 0, 0))],
            out_specs=[hbm] * n_in,
            scratch_shapes=[pltpu.VMEM((d, hid2), BF16), pltpu.VMEM((hid, d), BF16),
                            buf(MOE_IN_BUFS), buf(MOE_OUT_BUFS),
                            pltpu.SemaphoreType.DMA((MOE_IN_BUFS, n_in)),
                            pltpu.SemaphoreType.DMA((MOE_OUT_BUFS, n_in))]),
        out_shape=[jax.ShapeDtypeStruct((n_rows, SC_ROW_WORDS), U32)] * n_in,
        compiler_params=pltpu.CompilerParams(
            dimension_semantics=("arbitrary",), vmem_limit_bytes=VMEM_LIMIT),
        name="moe",
    )(first_block, block_count, *xb_pieces, w_up, w_down)


def _combine_kernel(n_pieces, x1_ref, route_ref, ada_ref, g_ref, b_ref, *refs):
    o_ref = refs[-1]
    ya = _unpack_bf16_pairs(jnp.concatenate([r[...] for r in refs[:n_pieces]], axis=1))
    yb = _unpack_bf16_pairs(jnp.concatenate([r[...] for r in refs[n_pieces:2 * n_pieces]], axis=1))
    r = route_ref[...]
    m = ya * r[:, 2:3] + yb * r[:, 3:4]
    g2 = ada_ref[5:6]
    o_ref[...] = _layernorm(ALPHA * x1_ref[...] + (1.0 + g2) * m, g_ref[...], b_ref[...])


def _combine_call(b, out_prev, x1, y_pieces, route, ada, ln2g, ln2b):
    bsz, s, d = x1.shape
    ts = min(OUT_ROWS, s)
    nj = s // ts
    n_pieces = len(y_pieces)
    slot = lambda k: pl.BlockSpec((ts, SC_ROW_WORDS), lambda j: (k * nj + j, 0))
    in_specs = ([pl.BlockSpec((None, ts, d), lambda j: (b, j, 0)),
                 pl.BlockSpec((None, ts, LANES), lambda j: (b, j, 0)),
                 pl.BlockSpec((None,) + ada.shape[1:], lambda j: (b, 0, 0)),
                 pl.BlockSpec((1, d), lambda j: (0, 0)),
                 pl.BlockSpec((1, d), lambda j: (0, 0))]
                + [slot(0)] * n_pieces + [slot(1)] * n_pieces)
    args = [x1, route, ada, ln2g, ln2b, *y_pieces, *y_pieces]
    aliases = {}
    if out_prev is not None:
        in_specs.append(pl.BlockSpec(memory_space=pl.ANY))
        aliases = {len(args): 0}
        args.append(out_prev)
    n_extra = len(args) - 5 - 2 * n_pieces

    def body(*refs):
        _combine_kernel(n_pieces, *refs[:5 + 2 * n_pieces], *refs[5 + 2 * n_pieces + n_extra:])

    return pl.pallas_call(
        body,
        grid=(nj,),
        in_specs=in_specs,
        out_specs=pl.BlockSpec((None, ts, d), lambda j: (b, j, 0)),
        out_shape=jax.ShapeDtypeStruct((bsz, s, d), F32),
        input_output_aliases=aliases,
        compiler_params=pltpu.CompilerParams(dimension_semantics=("arbitrary",)),
        name="combine",
    )(*args)


def kernel(x, c, w_ada, b_ada, w_in, lb_logits, hg_norm_w, sg_ln_g, sg_ln_b, sg_w, sg_b, w_out, ln1_g, ln1_b, router_group_w, router_group_b, router_expert_w, router_expert_b, w_up, w_down, ln2_g, ln2_b):
    assert w_in.shape[0] == DEPTH
    bsz, s, d = x.shape
    t = bsz * s
    l = 0

    ada = _ada_call(c, w_ada[l], b_ada[l])

    wr = jnp.concatenate(
        [router_group_w[l].T, router_expert_w[l].transpose(0, 2, 1).reshape(MOE_EXPERTS, d)], axis=0)
    wr = jnp.pad(wr, ((0, ROUTE_ROWS - wr.shape[0]), (0, 0)))
    br = jnp.concatenate([router_group_b[l], router_expert_b[l].reshape(MOE_EXPERTS)])
    br = jnp.broadcast_to(jnp.pad(br, (0, ROUTE_ROWS - br.shape[0]))[:, None], (ROUTE_ROWS, LANES))
    wrs = jnp.concatenate(_split_bf16(wr), axis=0)

    x1, route, route_t, *h2p = _mix_call(
        x, ada, w_in[l].astype(BF16), lb_logits, hg_norm_w[l].reshape(1, -1),
        sg_ln_g[l].reshape(1, -1), sg_ln_b[l].reshape(1, -1), sg_w[l], sg_b[l].T,
        w_out[l].astype(BF16), ln1_g[l].reshape(1, d), ln1_b[l].reshape(1, d), wrs, br)

    n_blocks = -(-(2 * t) // MOE_ROWS) + MOE_EXPERTS
    n_rows = n_blocks * MOE_ROWS
    dest, blk = _sort_call(route_t)
    d0, d1 = dest[0], dest[1]

    xb = [_sc_scatter_rows2(p.reshape(t, SC_ROW_WORDS), d0, d1, n_rows) for p in h2p]
    yb = _moe_call(blk[:, 0], blk[:, 1], xb, w_up[l], w_down[l])
    out = None
    for b in range(bsz):
        tok = slice(b * s, (b + 1) * s)
        idx = jnp.concatenate([d0[tok], d1[tok]])
        y2 = [_sc_gather_rows(p, idx) for p in yb]
        out = _combine_call(b, out, x1, y2, route, ada, ln2_g[l].reshape(1, d), ln2_b[l].reshape(1, d))
    return out
```

```python
import functools

import jax
import jax.numpy as jnp
from jax import lax
from jax.experimental import pallas as pl
from jax.experimental.pallas import tpu as pltpu
from jax.experimental.pallas import tpu_sc as plsc

F32 = jnp.float32
BF16 = jnp.bfloat16
I32 = jnp.int32
U32 = jnp.uint32

HG_HEADS = 4
HG_DK = 128
HG_CHUNK = 64
HG_SUB = 8
SG_GROUPS = 4
SG_CH = 128
SG_CHUNK = 128
MOE_GROUPS = 4
MOE_EPG = 8
MOE_EXPERTS = MOE_GROUPS * MOE_EPG
MOE_HIDDEN = 512
DEPTH = 1
ALPHA = (2.0 * DEPTH) ** 0.25
LN_EPS = 1e-5
RMS_EPS = 1e-6
LOG2E = 1.4426950408889634

LANES = 128
SUBLANES = 8
SC_WINDOW = 128
SC_ROW_WORDS = 256

MIX_ROWS = 512
PROJ_PIECE = 256
MIX_TAIL_PARTS = 1
SORT_ROWS = 4096
SORT_BLOCK = 256
ROUTE_ROWS = 40
MOE_ROWS = 256
MOE_TALL = 4
MOE_AHEAD = 4
MOE_IN_BUFS = MOE_AHEAD + MOE_TALL
MOE_OUT_BUFS = 2 * MOE_TALL
OUT_ROWS = 512
VMEM_LIMIT = 52 * 1024 * 1024


def _dot(a, b):
    return jnp.dot(a, b, preferred_element_type=F32)


def _dot_nt(a, b):
    return lax.dot_general(a, b, (((1,), (1,)), ((), ())), preferred_element_type=F32)


def _dot_tn(a, b):
    return lax.dot_general(a, b, (((0,), (0,)), ((), ())), preferred_element_type=F32)


def _sigmoid(x):
    return jax.nn.sigmoid(x)


def _gelu_exact(x):
    return 0.5 * x * (1.0 + lax.erf(x * (2.0 ** -0.5)))


def _layernorm(x, g, b):
    mu = jnp.mean(x, axis=-1, keepdims=True)
    xc = x - mu
    var = jnp.mean(xc * xc, axis=-1, keepdims=True)
    return xc * lax.rsqrt(var + LN_EPS) * g + b


def _pack_bf16_pairs(x):
    n = x.shape[1]
    bits = lax.bitcast_convert_type(x.astype(BF16).astype(F32), U32)
    return bits[:, n // 2:] | (bits[:, :n // 2] >> 16)


def _unpack_bf16_pairs(w):
    lo = lax.bitcast_convert_type(w << 16, F32)
    hi = lax.bitcast_convert_type(w & jnp.uint32(0xFFFF0000), F32)
    return jnp.concatenate([lo, hi], axis=1)


def _split_bf16(x):
    hi = x.astype(BF16)
    lo = (x - hi.astype(F32)).astype(BF16)
    return hi, lo


def _ada_kernel(ct_ref, w_ref, b_ref, o_ref):
    ct = ct_ref[...]
    ca = ct * _sigmoid(ct)
    w = w_ref[...]
    rows = [jnp.sum(ca[:, b:b + 1] * w, axis=0, keepdims=True) for b in range(ct.shape[1])]
    pad = o_ref.shape[0] - len(rows)
    if pad:
        rows.append(jnp.zeros((pad, w.shape[1]), F32))
    o_ref[...] = jnp.concatenate(rows, axis=0) + b_ref[...]


def _ada_call(c, w_ada, b_ada):
    bsz, d = c.shape
    n = w_ada.shape[1]
    rows = -(-bsz // SUBLANES) * SUBLANES
    out = pl.pallas_call(
        _ada_kernel,
        grid=(n // d,),
        in_specs=[pl.BlockSpec((d, bsz), lambda i: (0, 0)),
                  pl.BlockSpec((d, d), lambda i: (0, i)),
                  pl.BlockSpec((1, d), lambda i: (0, i))],
        out_specs=pl.BlockSpec((rows, d), lambda i: (0, i)),
        out_shape=jax.ShapeDtypeStruct((rows, n), F32),
        name="ada",
    )(c.T, w_ada, b_ada.reshape(1, n))
    return out[:bsz].reshape(bsz, n // d, d)


def _hgrn2_matmul_phase(c, proj_ref, st_ref, g_ref, h_ref, lb, tril):
    hw = HG_HEADS * HG_DK
    cl = HG_CHUNK
    rows = pl.ds(c * cl, cl)
    qz = proj_ref[rows, 0:hw]
    fz = proj_ref[rows, hw:2 * hw]
    v = proj_ref[rows, 2 * hw:3 * hw]

    q = qz * _sigmoid(qz)
    f = lb + (1.0 - lb) * _sigmoid(fz)
    lf = jnp.log(f)
    k = 1.0 - f
    lf_hi, lf_lo = _split_bf16(lf)
    g = (_dot(tril, lf_hi) + _dot(tril, lf_lo)) * LOG2E
    glast = g[cl - 1:cl, :]
    qg = (q * jnp.exp2(g)).astype(BF16)
    kd = (k * jnp.exp2(glast - g)).astype(BF16)
    vb = v.astype(BF16)
    g_ref[...] = g
    h_ref[...] = g - jnp.log(k) * LOG2E

    n_sub = cl // HG_SUB
    o_state = []
    acc_all = []
    for hd in range(HG_HEADS):
        cs = slice(hd * HG_DK, (hd + 1) * HG_DK)
        st = st_ref[hd]
        o_state.append(_dot_nt(qg[:, cs], st.astype(BF16)))
        st_ref[hd] = st * jnp.exp2(glast[:, cs]) + _dot_tn(vb[:, cs], kd[:, cs])
        gh = g[:, cs]
        kh = k[:, cs]
        qh = q[:, cs]
        acc = [None] * n_sub
        w = cl // 2
        while w >= HG_SUB:
            for p in range(0, cl, 2 * w):
                ref = g_ref[p + w - 1:p + w, cs]
                qt = (qh[p + w:p + 2 * w] * jnp.exp2(gh[p + w:p + 2 * w] - ref)).astype(BF16)
                kt = (kh[p:p + w] * jnp.exp2(ref - gh[p:p + w])).astype(BF16)
                pieces = [kt]
                if p:
                    pieces.insert(0, jnp.zeros((p, HG_DK), BF16))
                if cl - p - w:
                    pieces.append(jnp.zeros((cl - p - w, HG_DK), BF16))
                blk = _dot_nt(qt, jnp.concatenate(pieces, axis=0))
                for r in range(w // HG_SUB):
                    i = (p + w) // HG_SUB + r
                    part = blk[r * HG_SUB:(r + 1) * HG_SUB]
                    acc[i] = part if acc[i] is None else acc[i] + part
            w //= 2
        acc_all.append(acc)
    return q, g, vb, o_state, acc_all


def _hgrn2_diag_phase(c, phase1, proj_ref, y_ref, h_ref, gnw):
    q, g, vb, o_state, acc_all = phase1
    hw = HG_HEADS * HG_DK
    cl = HG_CHUNK
    rows = pl.ds(c * cl, cl)
    og = proj_ref[rows, 3 * hw:4 * hw]
    n_sub = cl // HG_SUB
    lane_c = lax.broadcasted_iota(I32, (HG_SUB, cl), 1)
    trow = lax.broadcasted_iota(I32, (HG_SUB, cl), 0)
    for hd in range(HG_HEADS):
        cs = slice(hd * HG_DK, (hd + 1) * HG_DK)
        gh = g[:, cs]
        qh = q[:, cs]
        a_rows = []
        for i in range(n_sub):
            b0 = i * HG_SUB
            gb = gh[b0:b0 + HG_SUB]
            qb = qh[b0:b0 + HG_SUB]
            a = jnp.zeros((HG_SUB, cl), F32) if acc_all[hd][i] is None else acc_all[hd][i]
            for s in range(HG_SUB):
                hs = h_ref[b0 + s:b0 + s + 1, cs]
                col = jnp.sum(qb * jnp.exp2(gb - hs), axis=-1, keepdims=True)
                a = jnp.where(lane_c == b0 + s, col, a)
            a_rows.append(jnp.where(lane_c <= b0 + trow, a, 0.0))
        amat = jnp.concatenate(a_rows, axis=0).astype(BF16)
        o = o_state[hd] + _dot(amat, vb[:, cs])
        ms = jnp.mean(o * o, axis=-1, keepdims=True)
        ogh = og[:, cs]
        ya = o * lax.rsqrt(ms + RMS_EPS) * gnw * (ogh * _sigmoid(ogh))
        y_ref[rows, cs] = ya.astype(BF16)


def _route_t(logits):
    nr, ts = logits.shape
    row = lax.broadcasted_iota(I32, (nr, ts), 0)
    neg = -jnp.inf
    gl = jnp.where(row < MOE_GROUPS, logits, neg)
    gmax = jnp.max(gl, axis=0, keepdims=True)
    gidx = jnp.min(jnp.where(gl == gmax, row, nr), axis=0, keepdims=True)
    p_group = 1.0 / jnp.sum(jnp.exp(gl - gmax), axis=0, keepdims=True)
    e_lo = MOE_GROUPS + gidx * MOE_EPG
    el = jnp.where((row >= e_lo) & (row < e_lo + MOE_EPG), logits, neg)
    m1 = jnp.max(el, axis=0, keepdims=True)
    i1 = jnp.min(jnp.where(el == m1, row, nr), axis=0, keepdims=True)
    el2 = jnp.where(row == i1, neg, el)
    m2 = jnp.max(el2, axis=0, keepdims=True)
    i2 = jnp.min(jnp.where(el2 == m2, row, nr), axis=0, keepdims=True)
    esum = jnp.sum(jnp.exp(el - m1), axis=0, keepdims=True)
    p1 = 1.0 / esum
    p2 = jnp.exp(m2 - m1) / esum
    den = p1 + p2
    w1 = p_group * p1 / den
    w2 = p_group * p2 / den
    e1 = (i1 - MOE_GROUPS).astype(F32)
    e2 = (i2 - MOE_GROUPS).astype(F32)
    out_row = lax.broadcasted_iota(I32, (SUBLANES, ts), 0)
    return jnp.where(out_row == 0, e1, jnp.where(out_row == 1, e2,
                     jnp.where(out_row == 2, w1, jnp.where(out_row == 3, w2, 0.0))))


def _mix_kernel(nt, x_ref, xn_ref, ada_ref, adan_ref, win_ref, lbl_ref, gnw_ref, sglg_ref, sglb_ref,
                sgw_ref, sgbt_ref, wout_ref, ln1g_ref, ln1b_ref, wrs_ref, br_ref,
                x1_ref, route_ref, routet_ref, *rest):
    h2p_refs = rest[:-6]
    st_ref, proj_ref, projn_ref, y_ref, g_ref, h_ref = rest[-6:]
    ts, d = x_ref.shape
    hw = HG_HEADS * HG_DK
    sgw = SG_GROUPS * SG_CH
    ncol = proj_ref.shape[1]
    step = pl.program_id(0)

    def modulated(xr, adar):
        a = adar[...]
        return (xr[...] * (1.0 + a[1:2]) + a[0:1]).astype(BF16)

    @pl.when(step % nt == 0)
    def _():
        st_ref[...] = jnp.zeros_like(st_ref)

    @pl.when(step == 0)
    def _():
        proj_ref[...] = _dot(modulated(x_ref, ada_ref), win_ref[...])

    x = x_ref[...]
    ada = ada_ref[...]
    g1, sh2, sc2 = ada[2:3], ada[3:4], ada[4:5]
    hn = modulated(xn_ref, adan_ref)

    lbl = lbl_ref[...]
    slots = [lbl[i:i + 1] for i in range(lbl.shape[0])]
    mx = functools.reduce(jnp.maximum, slots)
    ex = [jnp.exp(s - mx) for s in slots]
    lb = ex[0] / functools.reduce(lambda a, b: a + b, ex)

    ci = lax.broadcasted_iota(I32, (HG_CHUNK, HG_CHUNK), 0)
    cj = lax.broadcasted_iota(I32, (HG_CHUNK, HG_CHUNK), 1)
    tril = (ci >= cj).astype(BF16)
    gnw = gnw_ref[...]

    n_chunks = ts // HG_CHUNK
    n_sg = ts // SG_CHUNK
    n_pp = ncol // PROJ_PIECE
    n_tail = max(n_pp - n_chunks, 0)
    slot_of = [p if p < n_pp - n_tail else n_chunks + (2 * (p - (n_pp - n_tail))) // max(n_tail, 1)
               for p in range(n_pp)]

    def next_proj_pieces(slot):
        for p in range(n_pp):
            if min(slot_of[p], n_chunks + 1) == slot:
                cols = slice(p * PROJ_PIECE, (p + 1) * PROJ_PIECE)
                projn_ref[:, cols] = _dot(hn, win_ref[:, cols])

    pi = lax.broadcasted_iota(I32, (SG_CHUNK, SG_CHUNK), 0)
    pj = lax.broadcasted_iota(I32, (SG_CHUNK, SG_CHUNK), 1)
    sglg = sglg_ref[...]
    sglb = sglb_ref[...]

    def spatial_gating(p):
        rows = slice(p * SG_CHUNK, (p + 1) * SG_CHUNK)
        z = _gelu_exact(proj_ref[rows, 4 * hw:4 * hw + 2 * sgw])
        u = z[:, :sgw]
        vn = _layernorm(z[:, sgw:], sglg, sglb).astype(BF16)
        for gi in range(SG_GROUPS):
            cs = slice(gi * SG_CH, (gi + 1) * SG_CH)
            wc = jnp.where(pi >= pj, sgw_ref[gi], 0.0).astype(BF16)
            mixed = _dot(wc, vn[:, cs]) + sgbt_ref[:, gi:gi + 1]
            y_ref[rows, hw + gi * SG_CH:hw + (gi + 1) * SG_CH] = (u[:, cs] * mixed).astype(BF16)

    phase1 = _hgrn2_matmul_phase(0, proj_ref, st_ref, g_ref.at[0], h_ref.at[0], lb, tril)
    for c in range(n_chunks):
        next_proj_pieces(c)
        if c + 1 < n_chunks:
            nxt = (c + 1) % 2
            phase1_next = _hgrn2_matmul_phase(c + 1, proj_ref, st_ref, g_ref.at[nxt], h_ref.at[nxt],
                                              lb, tril)
        _hgrn2_diag_phase(c, phase1, proj_ref, y_ref, h_ref.at[c % 2], gnw)
        phase1 = phase1_next
    for p in range(n_sg):
        spatial_gating(p)

    nr = br_ref.shape[0]
    ln1g = ln1g_ref[...]
    ln1b = ln1b_ref[...]
    part_rows = ts // MIX_TAIL_PARTS
    for r in range(MIX_TAIL_PARTS):
        rows = slice(r * part_rows, (r + 1) * part_rows)
        y = _dot(y_ref[rows, :], wout_ref[...])
        if r == 0:
            next_proj_pieces(n_chunks)
        x1 = _layernorm(ALPHA * x[rows] + (1.0 + g1) * y, ln1g, ln1b)
        x1_ref[rows, :] = x1
        h2 = x1 * (1.0 + sc2) + sh2
        h2_hi, h2_lo = _split_bf16(h2)
        packed = _pack_bf16_pairs(h2)
        for i, ref in enumerate(h2p_refs):
            ref[rows, :] = packed[:, i * SC_ROW_WORDS:(i + 1) * SC_ROW_WORDS]
        part = _dot_nt(wrs_ref[...], h2_hi)
        logits = part[:nr] + part[nr:] + _dot_nt(wrs_ref[:nr, :], h2_lo) + br_ref[:, 0:1]
        if r == 0:
            next_proj_pieces(n_chunks + 1)
        rt = _route_t(logits)
        routet_ref[:, rows] = rt
        route_ref[rows, :] = jnp.concatenate(
            [rt, jnp.zeros((LANES - SUBLANES, part_rows), F32)], axis=0).T
    proj_ref[...] = projn_ref[...]


def _mix_call(x, ada, w_in_b, lb_logits, gnw, sglg, sglb, sg_w, sg_bt, w_out_b, ln1g, ln1b, wrs, br):
    bsz, s, d = x.shape
    ts = min(MIX_ROWS, s)
    nt = s // ts
    n_steps = bsz * nt
    ncol = w_in_b.shape[1]
    hw = HG_HEADS * HG_DK
    n_pieces = d // 2 // SC_ROW_WORDS
    const = lambda a: pl.BlockSpec(a.shape, lambda i: (0,) * a.ndim, pipeline_mode=pl.Buffered(1))
    cur = lambda i: (i // nt, i % nt, 0)
    nxt = lambda i: cur(jnp.minimum(i + 1, n_steps - 1))
    return pl.pallas_call(
        functools.partial(_mix_kernel, nt),
        grid=(n_steps,),
        in_specs=[pl.BlockSpec((None, ts, d), cur),
                  pl.BlockSpec((None, ts, d), nxt),
                  pl.BlockSpec((None,) + ada.shape[1:], lambda i: (cur(i)[0], 0, 0)),
                  pl.BlockSpec((None,) + ada.shape[1:], lambda i: (nxt(i)[0], 0, 0)),
                  const(w_in_b), const(lb_logits), const(gnw), const(sglg), const(sglb), const(sg_w),
                  const(sg_bt), const(w_out_b), const(ln1g), const(ln1b), const(wrs), const(br)],
        out_specs=[pl.BlockSpec((None, ts, d), cur),
                   pl.BlockSpec((None, ts, LANES), cur),
                   pl.BlockSpec((None, SUBLANES, ts), lambda i: (i // nt, 0, i % nt))]
                  + [pl.BlockSpec((None, ts, SC_ROW_WORDS), cur)] * n_pieces,
        out_shape=[jax.ShapeDtypeStruct((bsz, s, d), F32),
                   jax.ShapeDtypeStruct((bsz, s, LANES), F32),
                   jax.ShapeDtypeStruct((bsz, SUBLANES, s), F32)]
                  + [jax.ShapeDtypeStruct((bsz, s, SC_ROW_WORDS), U32)] * n_pieces,
        scratch_shapes=[pltpu.VMEM((HG_HEADS, HG_DK, HG_DK), F32),
                        pltpu.VMEM((ts, ncol), F32),
                        pltpu.VMEM((ts, ncol), F32),
                        pltpu.VMEM((ts, d), BF16),
                        pltpu.VMEM((2, HG_CHUNK, hw), F32),
                        pltpu.VMEM((2, HG_CHUNK, hw), F32)],
        compiler_params=pltpu.CompilerParams(
            dimension_semantics=("arbitrary",), vmem_limit_bytes=VMEM_LIMIT),
        name="mix",
    )(x, x, ada, ada, w_in_b, lb_logits, gnw, sglg, sglb, sg_w, sg_bt, w_out_b, ln1g, ln1b, wrs, br)


def _sort_kernel(nb, rt_ref, dest_ref, blk_ref, cnt_ref, pst_ref, carry_ref):
    ph = pl.program_id(0)
    first_step = (pl.program_id(1) == 0) & (pl.program_id(2) == 0)
    tk = rt_ref.shape[1]
    r = rt_ref[...]
    row = lax.broadcasted_iota(I32, (MOE_EXPERTS, tk), 0)
    oh1 = row == r[0:1].astype(I32)
    oh2 = row == r[1:2].astype(I32)
    hot = jnp.where(oh1 | oh2, 1.0, 0.0)

    @pl.when((ph == 0) & first_step)
    def _():
        cnt_ref[...] = jnp.zeros_like(cnt_ref)

    @pl.when(ph == 0)
    def _():
        cnt_ref[...] += jnp.sum(hot, axis=1, keepdims=True)

    @pl.when((ph == 1) & first_step)
    def _():
        nblk = jnp.floor((cnt_ref[...] + (MOE_ROWS - 1)) * (1.0 / MOE_ROWS))
        ei = lax.broadcasted_iota(I32, (MOE_EXPERTS, MOE_EXPERTS), 0)
        ej = lax.broadcasted_iota(I32, (MOE_EXPERTS, MOE_EXPERTS), 1)
        pstart = _dot((ej < ei).astype(BF16), nblk.astype(BF16))
        pst_ref[...] = pstart * float(MOE_ROWS)
        carry_ref[...] = jnp.zeros_like(carry_ref)
        lane_b = lax.broadcasted_iota(I32, blk_ref.shape, 1)
        blk_ref[...] = jnp.where(lane_b == 0, pstart, jnp.where(lane_b == 1, nblk, 0.0)).astype(I32)

    @pl.when(ph == 1)
    def _():
        ti = lax.broadcasted_iota(I32, (nb, nb), 0)
        tj = lax.broadcasted_iota(I32, (nb, nb), 1)
        before = (ti < tj).astype(BF16)
        carry = carry_ref[...] + pst_ref[...]
        rows1 = []
        rows2 = []
        for j in range(tk // nb):
            cs = slice(j * nb, (j + 1) * nb)
            hb = hot[:, cs]
            base = _dot(hb.astype(BF16), before) + carry[:, 0:1]
            rows1.append(jnp.sum(jnp.where(oh1[:, cs], base, 0.0), axis=0, keepdims=True))
            rows2.append(jnp.sum(jnp.where(oh2[:, cs], base, 0.0), axis=0, keepdims=True))
            carry = carry + jnp.sum(hb, axis=1, keepdims=True)
        d1 = jnp.concatenate(rows1, axis=1)
        d2 = jnp.concatenate(rows2, axis=1)
        out_row = lax.broadcasted_iota(I32, (SUBLANES, tk), 0)
        dest_ref[...] = jnp.where(out_row == 0, d1, jnp.where(out_row == 1, d2, 0.0)).astype(I32)
        carry_ref[...] = carry - pst_ref[...]


def _sort_call(route_t):
    bsz, _, s = route_t.shape
    tk = min(SORT_ROWS, s)
    nj = s // tk
    return pl.pallas_call(
        functools.partial(_sort_kernel, min(SORT_BLOCK, tk)),
        grid=(2, bsz, nj),
        in_specs=[pl.BlockSpec((None, SUBLANES, tk), lambda p, b, j: (b, 0, j))],
        out_specs=[pl.BlockSpec((SUBLANES, tk), lambda p, b, j: (0, (b * nj + j) * p)),
                   pl.BlockSpec((MOE_EXPERTS, LANES), lambda p, b, j: (0, 0))],
        out_shape=[jax.ShapeDtypeStruct((SUBLANES, bsz * s), I32),
                   jax.ShapeDtypeStruct((MOE_EXPERTS, LANES), I32)],
        scratch_shapes=[pltpu.VMEM((MOE_EXPERTS, LANES), F32), pltpu.VMEM((MOE_EXPERTS, LANES), F32),
                        pltpu.VMEM((MOE_EXPERTS, LANES), F32)],
        compiler_params=pltpu.CompilerParams(
            dimension_semantics=("arbitrary", "arbitrary", "arbitrary")),
        name="sort",
    )(route_t)


def _sc_mesh():
    return plsc.VectorSubcoreMesh(core_axis_name="c", subcore_axis_name="s")


def _sc_scatter_rows2(rows, idx_a, idx_b, n_out):
    n, w = rows.shape

    @pl.kernel(out_type=jax.ShapeDtypeStruct((n_out, w), rows.dtype), mesh=_sc_mesh(), scratch_types=[])
    def scatter(x_hbm, ia_hbm, ib_hbm, o_hbm):
        def body(x_vmem, ia_vmem, ib_vmem):
            pltpu.sync_copy(x_vmem, o_hbm.at[ia_vmem.at[0]])
            pltpu.sync_copy(x_vmem, o_hbm.at[ib_vmem.at[0]])

        pltpu.emit_pipeline(
            body,
            grid=(n // SC_WINDOW,),
            in_specs=[pl.BlockSpec((SC_WINDOW, w), lambda i: (i, 0)),
                      pl.BlockSpec((1, SC_WINDOW), lambda i: (0, i)),
                      pl.BlockSpec((1, SC_WINDOW), lambda i: (0, i))],
            out_specs=[],
            core_axis_name=("c", "s"),
            dimension_semantics=(pltpu.PARALLEL,),
        )(x_hbm, ia_hbm, ib_hbm)

    return scatter(rows, idx_a.reshape(1, n), idx_b.reshape(1, n))


def _sc_gather_rows(src, idx):
    n = idx.shape[0]
    w = src.shape[1]

    @pl.kernel(out_type=jax.ShapeDtypeStruct((n, w), src.dtype), mesh=_sc_mesh(), scratch_types=[])
    def gather(x_hbm, i_hbm, o_hbm):
        def body(i_vmem, o_vmem):
            pltpu.sync_copy(x_hbm.at[i_vmem.at[0]], o_vmem)

        pltpu.emit_pipeline(
            body,
            grid=(n // SC_WINDOW,),
            in_specs=[pl.BlockSpec((1, SC_WINDOW), lambda i: (0, i))],
            out_specs=[pl.BlockSpec((SC_WINDOW, w), lambda i: (i, 0))],
            core_axis_name=("c", "s"),
            dimension_semantics=(pltpu.PARALLEL,),
        )(i_hbm, o_hbm)

    return gather(src, idx.reshape(1, n))


def _moe_kernel(n_in, first_ref, count_ref, *refs):
    xb_hbm = refs[:n_in]
    wup_ref, wdn_ref = refs[n_in:n_in + 2]
    y_hbm = refs[n_in + 2:2 * n_in + 2]
    wupb_ref, wdnb_ref, xbuf, ybuf, in_sem, out_sem = refs[2 * n_in + 2:]
    e = pl.program_id(0)
    n_exp = pl.num_programs(0)
    first = first_ref[e]
    count = count_ref[e]
    total = first_ref[n_exp - 1] + count_ref[n_exp - 1]

    def rows_of(g):
        return pl.ds(pl.multiple_of(g * MOE_ROWS, MOE_ROWS), MOE_ROWS)

    def in_copy(g, i):
        slot = g % MOE_IN_BUFS
        return pltpu.make_async_copy(xb_hbm[i].at[rows_of(g)], xbuf.at[slot, i], in_sem.at[slot, i])

    def out_copy(g, i):
        slot = g % MOE_OUT_BUFS
        return pltpu.make_async_copy(ybuf.at[slot, i], y_hbm[i].at[rows_of(g)], out_sem.at[slot, i])

    @pl.when(e == 0)
    def _():
        for g in range(MOE_AHEAD):
            @pl.when(g < total)
            def _():
                for i in range(n_in):
                    in_copy(g, i).start()

    @pl.when(count > 0)
    def _():
        wupb_ref[...] = wup_ref[...].astype(BF16)
        wdnb_ref[...] = wdn_ref[...].astype(BF16)

        def run_blocks(gs):
            for g in gs:
                for i in range(n_in):
                    in_copy(g, i).wait()
            for g in gs:
                @pl.when(g + MOE_AHEAD < total)
                def _():
                    for i in range(n_in):
                        in_copy(g + MOE_AHEAD, i).start()
            for g in gs:
                @pl.when(g >= MOE_OUT_BUFS)
                def _():
                    for i in range(n_in):
                        out_copy(g - MOE_OUT_BUFS, i).wait()
            w = jnp.concatenate(
                [jnp.concatenate([xbuf[g % MOE_IN_BUFS, i] for i in range(n_in)], axis=1)
                 for g in gs], axis=0)
            gu = _dot(_unpack_bf16_pairs(w).astype(BF16), wupb_ref[...])
            gate = gu[:, :MOE_HIDDEN]
            act = (gate * _sigmoid(gate) * gu[:, MOE_HIDDEN:]).astype(BF16)
            y = _pack_bf16_pairs(_dot(act, wdnb_ref[...]))
            for k, g in enumerate(gs):
                for i in range(n_in):
                    ybuf[g % MOE_OUT_BUFS, i] = y[k * MOE_ROWS:(k + 1) * MOE_ROWS,
                                                  i * SC_ROW_WORDS:(i + 1) * SC_ROW_WORDS]
                    out_copy(g, i).start()

        def full_group(p, carry):
            g = first + MOE_TALL * p
            run_blocks([g + k for k in range(MOE_TALL)])
            return carry

        n_full = count // MOE_TALL
        lax.fori_loop(0, n_full, full_group, 0)
        done = n_full * MOE_TALL
        size = MOE_TALL // 2
        while size >= 1:
            start = first + done + ((count - done) // (2 * size)) * (2 * size)

            @pl.when(((count - done) // size) % 2 == 1)
            def _(start=start, size=size):
                run_blocks([start + k for k in range(size)])

            size //= 2

    @pl.when(e == n_exp - 1)
    def _():
        for k in range(MOE_OUT_BUFS):
            @pl.when(total - 1 - k >= 0)
            def _():
                for i in range(n_in):
                    out_copy(total - 1 - k, i).wait()


def _moe_call(first_block, block_count, xb_pieces, w_up, w_down):
    n_rows = xb_pieces[0].shape[0]
    n_in = len(xb_pieces)
    d = 2 * n_in * SC_ROW_WORDS
    n_exp, _, hid2 = w_up.shape
    hid = w_down.shape[1]
    hbm = pl.BlockSpec(memory_space=pl.ANY)
    buf = lambda n: pltpu.VMEM((n, n_in, MOE_ROWS, SC_ROW_WORDS), U32)
    return pl.pallas_call(
        functools.partial(_moe_kernel, n_in),
        grid_spec=pltpu.PrefetchScalarGridSpec(
            num_scalar_prefetch=2,
            grid=(n_exp,),
            in_specs=[hbm] * n_in + [
                      pl.BlockSpec((None, d, hid2), lambda e, fb, bc: (e, 0, 0)),
                      pl.BlockSpec((None, hid, d), lambda e, fb, bc: (e, 0, 0))],
            out_specs=[hbm] * n_in,
            scratch_shapes=[pltpu.VMEM((d, hid2), BF16), pltpu.VMEM((hid, d), BF16),
                            buf(MOE_IN_BUFS), buf(MOE_OUT_BUFS),
                            pltpu.SemaphoreType.DMA((MOE_IN_BUFS, n_in)),
                            pltpu.SemaphoreType.DMA((MOE_OUT_BUFS, n_in))]),
        out_shape=[jax.ShapeDtypeStruct((n_rows, SC_ROW_WORDS), U32)] * n_in,
        compiler_params=pltpu.CompilerParams(
            dimension_semantics=("arbitrary",), vmem_limit_bytes=VMEM_LIMIT),
        name="moe",
    )(first_block, block_count, *xb_pieces, w_up, w_down)


def _combine_kernel(n_pieces, x1_ref, route_ref, ada_ref, g_ref, b_ref, *refs):
    o_ref = refs[-1]
    ya = _unpack_bf16_pairs(jnp.concatenate([r[...] for r in refs[:n_pieces]], axis=1))
    yb = _unpack_bf16_pairs(jnp.concatenate([r[...] for r in refs[n_pieces:2 * n_pieces]], axis=1))
    r = route_ref[...]
    m = ya * r[:, 2:3] + yb * r[:, 3:4]
    g2 = ada_ref[5:6]
    o_ref[...] = _layernorm(ALPHA * x1_ref[...] + (1.0 + g2) * m, g_ref[...], b_ref[...])


def _combine_call(b, out_prev, x1, y_pieces, route, ada, ln2g, ln2b):
    bsz, s, d = x1.shape
    ts = min(OUT_ROWS, s)
    nj = s // ts
    n_pieces = len(y_pieces)
    slot = lambda k: pl.BlockSpec((ts, SC_ROW_WORDS), lambda j: (k * nj + j, 0))
    in_specs = ([pl.BlockSpec((None, ts, d), lambda j: (b, j, 0)),
                 pl.BlockSpec((None, ts, LANES), lambda j: (b, j, 0)),
                 pl.BlockSpec((None,) + ada.shape[1:], lambda j: (b, 0, 0)),
                 pl.BlockSpec((1, d), lambda j: (0, 0)),
                 pl.BlockSpec((1, d), lambda j: (0, 0))]
                + [slot(0)] * n_pieces + [slot(1)] * n_pieces)
    args = [x1, route, ada, ln2g, ln2b, *y_pieces, *y_pieces]
    aliases = {}
    if out_prev is not None:
        in_specs.append(pl.BlockSpec(memory_space=pl.ANY))
        aliases = {len(args): 0}
        args.append(out_prev)
    n_extra = len(args) - 5 - 2 * n_pieces

    def body(*refs):
        _combine_kernel(n_pieces, *refs[:5 + 2 * n_pieces], *refs[5 + 2 * n_pieces + n_extra:])

    return pl.pallas_call(
        body,
        grid=(nj,),
        in_specs=in_specs,
        out_specs=pl.BlockSpec((None, ts, d), lambda j: (b, j, 0)),
        out_shape=jax.ShapeDtypeStruct((bsz, s, d), F32),
        input_output_aliases=aliases,
        compiler_params=pltpu.CompilerParams(dimension_semantics=("arbitrary",)),
        name="combine",
    )(*args)


def kernel(x, c, w_ada, b_ada, w_in, lb_logits, hg_norm_w, sg_ln_g, sg_ln_b, sg_w, sg_b, w_out, ln1_g, ln1_b, router_group_w, router_group_b, router_expert_w, router_expert_b, w_up, w_down, ln2_g, ln2_b):
    assert w_in.shape[0] == DEPTH
    bsz, s, d = x.shape
    t = bsz * s
    l = 0

    ada = _ada_call(c, w_ada[l], b_ada[l])

    wr = jnp.concatenate(
        [router_group_w[l].T, router_expert_w[l].transpose(0, 2, 1).reshape(MOE_EXPERTS, d)], axis=0)
    wr = jnp.pad(wr, ((0, ROUTE_ROWS - wr.shape[0]), (0, 0)))
    br = jnp.concatenate([router_group_b[l], router_expert_b[l].reshape(MOE_EXPERTS)])
    br = jnp.broadcast_to(jnp.pad(br, (0, ROUTE_ROWS - br.shape[0]))[:, None], (ROUTE_ROWS, LANES))
    wrs = jnp.concatenate(_split_bf16(wr), axis=0)

    x1, route, route_t, *h2p = _mix_call(
        x, ada, w_in[l].astype(BF16), lb_logits, hg_norm_w[l].reshape(1, -1),
        sg_ln_g[l].reshape(1, -1), sg_ln_b[l].reshape(1, -1), sg_w[l], sg_b[l].T,
        w_out[l].astype(BF16), ln1_g[l].reshape(1, d), ln1_b[l].reshape(1, d), wrs, br)

    n_blocks = -(-(2 * t) // MOE_ROWS) + MOE_EXPERTS
    n_rows = n_blocks * MOE_ROWS
    dest, blk = _sort_call(route_t)
    d0, d1 = dest[0], dest[1]

    xb = [_sc_scatter_rows2(p.reshape(t, SC_ROW_WORDS), d0, d1, n_rows) for p in h2p]
    yb = _moe_call(blk[:, 0], blk[:, 1], xb, w_up[l], w_down[l])
    out = None
    for b in range(bsz):
        tok = slice(b * s, (b + 1) * s)
        idx = jnp.concatenate([d0[tok], d1[tok]])
        y2 = [_sc_gather_rows(p, idx) for p in yb]
        out = _combine_call(b, out, x1, y2, route, ada, ln2_g[l].reshape(1, d), ln2_b[l].reshape(1, d))
    return out
```

```python
import functools

import jax
import jax.numpy as jnp
from jax import lax
from jax.experimental import pallas as pl
from jax.experimental.pallas import tpu as pltpu
from jax.experimental.pallas import tpu_sc as plsc

F32 = jnp.float32
BF16 = jnp.bfloat16
I32 = jnp.int32
U32 = jnp.uint32

HG_HEADS = 4
HG_DK = 128
HG_CHUNK = 64
HG_SUB = 8
SG_GROUPS = 4
SG_CH = 128
SG_CHUNK = 128
MOE_GROUPS = 4
MOE_EPG = 8
MOE_EXPERTS = MOE_GROUPS * MOE_EPG
MOE_HIDDEN = 512
DEPTH = 1
ALPHA = (2.0 * DEPTH) ** 0.25
LN_EPS = 1e-5
RMS_EPS = 1e-6
LOG2E = 1.4426950408889634

LANES = 128
SUBLANES = 8
SC_WINDOW = 128
SC_ROW_WORDS = 256

MIX_ROWS = 512
PROJ_PIECE = 256
MIX_TAIL_PARTS = 1
SORT_ROWS = 4096
SORT_BLOCK = 256
ROUTE_ROWS = 40
MOE_ROWS = 256
MOE_TALL = 4
MOE_AHEAD = 4
MOE_IN_BUFS = MOE_AHEAD + MOE_TALL
MOE_OUT_BUFS = 2 * MOE_TALL
OUT_ROWS = 512
VMEM_LIMIT = 52 * 1024 * 1024
MIX_VMEM_LIMIT = 60 * 1024 * 1024


def _dot(a, b):
    return jnp.dot(a, b, preferred_element_type=F32)


def _dot_nt(a, b):
    return lax.dot_general(a, b, (((1,), (1,)), ((), ())), preferred_element_type=F32)


def _dot_tn(a, b):
    return lax.dot_general(a, b, (((0,), (0,)), ((), ())), preferred_element_type=F32)


def _sigmoid(x):
    return jax.nn.sigmoid(x)


def _gelu_exact(x):
    return 0.5 * x * (1.0 + lax.erf(x * (2.0 ** -0.5)))


def _layernorm(x, g, b):
    mu = jnp.mean(x, axis=-1, keepdims=True)
    xc = x - mu
    var = jnp.mean(xc * xc, axis=-1, keepdims=True)
    return xc * lax.rsqrt(var + LN_EPS) * g + b


def _pack_bf16_pairs(x):
    n = x.shape[1]
    bits = lax.bitcast_convert_type(x.astype(BF16).astype(F32), U32)
    return bits[:, n // 2:] | (bits[:, :n // 2] >> 16)


def _unpack_bf16_pairs(w):
    lo = lax.bitcast_convert_type(w << 16, F32)
    hi = lax.bitcast_convert_type(w & jnp.uint32(0xFFFF0000), F32)
    return jnp.concatenate([lo, hi], axis=1)


def _split_bf16(x):
    hi = x.astype(BF16)
    lo = (x - hi.astype(F32)).astype(BF16)
    return hi, lo


def _ada_kernel(ct_ref, w_ref, b_ref, o_ref):
    ct = ct_ref[...]
    ca = ct * _sigmoid(ct)
    w = w_ref[...]
    rows = [jnp.sum(ca[:, b:b + 1] * w, axis=0, keepdims=True) for b in range(ct.shape[1])]
    pad = o_ref.shape[0] - len(rows)
    if pad:
        rows.append(jnp.zeros((pad, w.shape[1]), F32))
    o_ref[...] = jnp.concatenate(rows, axis=0) + b_ref[...]


def _ada_call(c, w_ada, b_ada):
    bsz, d = c.shape
    n = w_ada.shape[1]
    rows = -(-bsz // SUBLANES) * SUBLANES
    out = pl.pallas_call(
        _ada_kernel,
        grid=(n // d,),
        in_specs=[pl.BlockSpec((d, bsz), lambda i: (0, 0)),
                  pl.BlockSpec((d, d), lambda i: (0, i)),
                  pl.BlockSpec((1, d), lambda i: (0, i))],
        out_specs=pl.BlockSpec((rows, d), lambda i: (0, i)),
        out_shape=jax.ShapeDtypeStruct((rows, n), F32),
        name="ada",
    )(c.T, w_ada, b_ada.reshape(1, n))
    return out[:bsz].reshape(bsz, n // d, d)


def _hgrn2_matmul_phase(c, proj_ref, st_ref, g_ref, h_ref, lb, tril):
    hw = HG_HEADS * HG_DK
    cl = HG_CHUNK
    rows = pl.ds(c * cl, cl)
    qz = proj_ref[rows, 0:hw]
    fz = proj_ref[rows, hw:2 * hw]
    v = proj_ref[rows, 2 * hw:3 * hw]

    q = qz * _sigmoid(qz)
    f = lb + (1.0 - lb) * _sigmoid(fz)
    lf = jnp.log(f)
    k = 1.0 - f
    lf_hi, lf_lo = _split_bf16(lf)
    g = (_dot(tril, lf_hi) + _dot(tril, lf_lo)) * LOG2E
    glast = g[cl - 1:cl, :]
    qg = (q * jnp.exp2(g)).astype(BF16)
    kd = (k * jnp.exp2(glast - g)).astype(BF16)
    vb = v.astype(BF16)
    g_ref[...] = g
    h_ref[...] = g - jnp.log(k) * LOG2E

    n_sub = cl // HG_SUB
    o_state = []
    acc_all = []
    for hd in range(HG_HEADS):
        cs = slice(hd * HG_DK, (hd + 1) * HG_DK)
        st = st_ref[hd]
        o_state.append(_dot_nt(qg[:, cs], st.astype(BF16)))
        st_ref[hd] = st * jnp.exp2(glast[:, cs]) + _dot_tn(vb[:, cs], kd[:, cs])
        gh = g[:, cs]
        kh = k[:, cs]
        qh = q[:, cs]
        acc = [None] * n_sub
        w = cl // 2
        while w >= HG_SUB:
            for p in range(0, cl, 2 * w):
                ref = g_ref[p + w - 1:p + w, cs]
                qt = (qh[p + w:p + 2 * w] * jnp.exp2(gh[p + w:p + 2 * w] - ref)).astype(BF16)
                kt = (kh[p:p + w] * jnp.exp2(ref - gh[p:p + w])).astype(BF16)
                pieces = [kt]
                if p:
                    pieces.insert(0, jnp.zeros((p, HG_DK), BF16))
                if cl - p - w:
                    pieces.append(jnp.zeros((cl - p - w, HG_DK), BF16))
                blk = _dot_nt(qt, jnp.concatenate(pieces, axis=0))
                for r in range(w // HG_SUB):
                    i = (p + w) // HG_SUB + r
                    part = blk[r * HG_SUB:(r + 1) * HG_SUB]
                    acc[i] = part if acc[i] is None else acc[i] + part
            w //= 2
        acc_all.append(acc)
    return q, g, vb, o_state, acc_all


def _hgrn2_diag_phase(c, phase1, proj_ref, y_ref, h_ref, gnw):
    q, g, vb, o_state, acc_all = phase1
    hw = HG_HEADS * HG_DK
    cl = HG_CHUNK
    rows = pl.ds(c * cl, cl)
    og = proj_ref[rows, 3 * hw:4 * hw]
    n_sub = cl // HG_SUB
    lane_c = lax.broadcasted_iota(I32, (HG_SUB, cl), 1)
    trow = lax.broadcasted_iota(I32, (HG_SUB, cl), 0)
    for hd in range(HG_HEADS):
        cs = slice(hd * HG_DK, (hd + 1) * HG_DK)
        gh = g[:, cs]
        qh = q[:, cs]
        a_rows = []
        for i in range(n_sub):
            b0 = i * HG_SUB
            gb = gh[b0:b0 + HG_SUB]
            qb = qh[b0:b0 + HG_SUB]
            a = jnp.zeros((HG_SUB, cl), F32) if acc_all[hd][i] is None else acc_all[hd][i]
            for s in range(HG_SUB):
                hs = h_ref[b0 + s:b0 + s + 1, cs]
                col = jnp.sum(qb * jnp.exp2(gb - hs), axis=-1, keepdims=True)
                a = jnp.where(lane_c == b0 + s, col, a)
            a_rows.append(jnp.where(lane_c <= b0 + trow, a, 0.0))
        amat = jnp.concatenate(a_rows, axis=0).astype(BF16)
        o = o_state[hd] + _dot(amat, vb[:, cs])
        ms = jnp.mean(o * o, axis=-1, keepdims=True)
        ogh = og[:, cs]
        ya = o * lax.rsqrt(ms + RMS_EPS) * gnw * (ogh * _sigmoid(ogh))
        y_ref[rows, cs] = ya.astype(BF16)


def _route_t(logits):
    nr, ts = logits.shape
    row = lax.broadcasted_iota(I32, (nr, ts), 0)
    neg = -jnp.inf
    gl = jnp.where(row < MOE_GROUPS, logits, neg)
    gmax = jnp.max(gl, axis=0, keepdims=True)
    gidx = jnp.min(jnp.where(gl == gmax, row, nr), axis=0, keepdims=True)
    p_group = 1.0 / jnp.sum(jnp.exp(gl - gmax), axis=0, keepdims=True)
    e_lo = MOE_GROUPS + gidx * MOE_EPG
    el = jnp.where((row >= e_lo) & (row < e_lo + MOE_EPG), logits, neg)
    m1 = jnp.max(el, axis=0, keepdims=True)
    i1 = jnp.min(jnp.where(el == m1, row, nr), axis=0, keepdims=True)
    el2 = jnp.where(row == i1, neg, el)
    m2 = jnp.max(el2, axis=0, keepdims=True)
    i2 = jnp.min(jnp.where(el2 == m2, row, nr), axis=0, keepdims=True)
    esum = jnp.sum(jnp.exp(el - m1), axis=0, keepdims=True)
    p1 = 1.0 / esum
    p2 = jnp.exp(m2 - m1) / esum
    den = p1 + p2
    w1 = p_group * p1 / den
    w2 = p_group * p2 / den
    e1 = (i1 - MOE_GROUPS).astype(F32)
    e2 = (i2 - MOE_GROUPS).astype(F32)
    out_row = lax.broadcasted_iota(I32, (SUBLANES, ts), 0)
    return jnp.where(out_row == 0, e1, jnp.where(out_row == 1, e2,
                     jnp.where(out_row == 2, w1, jnp.where(out_row == 3, w2, 0.0))))


def _mix_kernel(nt, x_ref, xn_ref, ada_ref, adan_ref, win32_ref, lbl_ref, gnw_ref, sglg_ref,
                sglb_ref, sgw_ref, sgbt_ref, wout32_ref, ln1g_ref, ln1b_ref, wrs_ref, br_ref,
                x1_ref, routet_ref, *rest):
    h2p_refs = rest[:-8]
    st_ref, proj_ref, projn_ref, y_ref, g_ref, h_ref, win_ref, wout_ref = rest[-8:]
    ts, d = x_ref.shape
    hw = HG_HEADS * HG_DK
    sgw = SG_GROUPS * SG_CH
    ncol = proj_ref.shape[1]
    step = pl.program_id(0)

    def modulated(xr, adar):
        a = adar[...]
        return (xr[...] * (1.0 + a[1:2]) + a[0:1]).astype(BF16)

    @pl.when(step % nt == 0)
    def _():
        st_ref[...] = jnp.zeros_like(st_ref)

    @pl.when(step == 0)
    def _():
        win_ref[...] = win32_ref[...].astype(BF16)
        wout_ref[...] = wout32_ref[...].astype(BF16)
        proj_ref[...] = _dot(modulated(x_ref, ada_ref), win_ref[...])

    x = x_ref[...]
    ada = ada_ref[...]
    g1, sh2, sc2 = ada[2:3], ada[3:4], ada[4:5]
    hn = modulated(xn_ref, adan_ref)

    lbl = lbl_ref[...]
    slots = [lbl[i:i + 1] for i in range(lbl.shape[0])]
    mx = functools.reduce(jnp.maximum, slots)
    ex = [jnp.exp(s - mx) for s in slots]
    lb = ex[0] / functools.reduce(lambda a, b: a + b, ex)

    ci = lax.broadcasted_iota(I32, (HG_CHUNK, HG_CHUNK), 0)
    cj = lax.broadcasted_iota(I32, (HG_CHUNK, HG_CHUNK), 1)
    tril = (ci >= cj).astype(BF16)
    gnw = gnw_ref[...]

    n_chunks = ts // HG_CHUNK
    n_sg = ts // SG_CHUNK
    n_pp = ncol // PROJ_PIECE
    n_tail = max(n_pp - n_chunks, 0)
    slot_of = [p if p < n_pp - n_tail else n_chunks + (2 * (p - (n_pp - n_tail))) // max(n_tail, 1)
               for p in range(n_pp)]

    def next_proj_pieces(slot):
        for p in range(n_pp):
            if min(slot_of[p], n_chunks + 1) == slot:
                cols = slice(p * PROJ_PIECE, (p + 1) * PROJ_PIECE)
                projn_ref[:, cols] = _dot(hn, win_ref[:, cols])

    pi = lax.broadcasted_iota(I32, (SG_CHUNK, SG_CHUNK), 0)
    pj = lax.broadcasted_iota(I32, (SG_CHUNK, SG_CHUNK), 1)
    sglg = sglg_ref[...]
    sglb = sglb_ref[...]

    def spatial_gating(p):
        rows = slice(p * SG_CHUNK, (p + 1) * SG_CHUNK)
        z = _gelu_exact(proj_ref[rows, 4 * hw:4 * hw + 2 * sgw])
        u = z[:, :sgw]
        vn = _layernorm(z[:, sgw:], sglg, sglb).astype(BF16)
        for gi in range(SG_GROUPS):
            cs = slice(gi * SG_CH, (gi + 1) * SG_CH)
            wc = jnp.where(pi >= pj, sgw_ref[gi], 0.0).astype(BF16)
            mixed = _dot(wc, vn[:, cs]) + sgbt_ref[:, gi:gi + 1]
            y_ref[rows, hw + gi * SG_CH:hw + (gi + 1) * SG_CH] = (u[:, cs] * mixed).astype(BF16)

    phase1 = _hgrn2_matmul_phase(0, proj_ref, st_ref, g_ref.at[0], h_ref.at[0], lb, tril)
    for c in range(n_chunks):
        next_proj_pieces(c)
        if c + 1 < n_chunks:
            nxt = (c + 1) % 2
            phase1_next = _hgrn2_matmul_phase(c + 1, proj_ref, st_ref, g_ref.at[nxt], h_ref.at[nxt],
                                              lb, tril)
        _hgrn2_diag_phase(c, phase1, proj_ref, y_ref, h_ref.at[c % 2], gnw)
        phase1 = phase1_next
    for p in range(n_sg):
        spatial_gating(p)

    nr = br_ref.shape[0]
    ln1g = ln1g_ref[...]
    ln1b = ln1b_ref[...]
    part_rows = ts // MIX_TAIL_PARTS
    for r in range(MIX_TAIL_PARTS):
        rows = slice(r * part_rows, (r + 1) * part_rows)
        y = _dot(y_ref[rows, :], wout_ref[...])
        if r == 0:
            next_proj_pieces(n_chunks)
        x1 = _layernorm(ALPHA * x[rows] + (1.0 + g1) * y, ln1g, ln1b)
        x1_ref[rows, :] = x1
        h2 = x1 * (1.0 + sc2) + sh2
        h2_hi, h2_lo = _split_bf16(h2)
        packed = _pack_bf16_pairs(h2)
        for i, ref in enumerate(h2p_refs):
            ref[rows, :] = packed[:, i * SC_ROW_WORDS:(i + 1) * SC_ROW_WORDS]
        part = _dot_nt(wrs_ref[...], h2_hi)
        logits = part[:nr] + part[nr:] + _dot_nt(wrs_ref[:nr, :], h2_lo) + br_ref[:, 0:1]
        if r == 0:
            next_proj_pieces(n_chunks + 1)
        rt = _route_t(logits)
        routet_ref[:, rows] = rt
    proj_ref[...] = projn_ref[...]


def _mix_call(x, ada, w_in, lb_logits, gnw, sglg, sglb, sg_w, sg_bt, w_out, ln1g, ln1b, wrs, br):
    bsz, s, d = x.shape
    ts = min(MIX_ROWS, s)
    nt = s // ts
    n_steps = bsz * nt
    ncol = w_in.shape[1]
    hw = HG_HEADS * HG_DK
    n_pieces = d // 2 // SC_ROW_WORDS
    const = lambda a: pl.BlockSpec(a.shape, lambda i: (0,) * a.ndim, pipeline_mode=pl.Buffered(1))
    cur = lambda i: (i // nt, i % nt, 0)
    nxt = lambda i: cur(jnp.minimum(i + 1, n_steps - 1))
    return pl.pallas_call(
        functools.partial(_mix_kernel, nt),
        grid=(n_steps,),
        in_specs=[pl.BlockSpec((None, ts, d), cur),
                  pl.BlockSpec((None, ts, d), nxt),
                  pl.BlockSpec((None,) + ada.shape[1:], lambda i: (cur(i)[0], 0, 0)),
                  pl.BlockSpec((None,) + ada.shape[1:], lambda i: (nxt(i)[0], 0, 0)),
                  const(w_in), const(lb_logits), const(gnw), const(sglg), const(sglb), const(sg_w),
                  const(sg_bt), const(w_out), const(ln1g), const(ln1b), const(wrs), const(br)],
        out_specs=[pl.BlockSpec((None, ts, d), cur),
                   pl.BlockSpec((None, SUBLANES, ts), lambda i: (i // nt, 0, i % nt))]
                  + [pl.BlockSpec((None, ts, SC_ROW_WORDS), cur)] * n_pieces,
        out_shape=[jax.ShapeDtypeStruct((bsz, s, d), F32),
                   jax.ShapeDtypeStruct((bsz, SUBLANES, s), F32)]
                  + [jax.ShapeDtypeStruct((bsz, s, SC_ROW_WORDS), U32)] * n_pieces,
        scratch_shapes=[pltpu.VMEM((HG_HEADS, HG_DK, HG_DK), F32),
                        pltpu.VMEM((ts, ncol), F32),
                        pltpu.VMEM((ts, ncol), F32),
                        pltpu.VMEM((ts, d), BF16),
                        pltpu.VMEM((2, HG_CHUNK, hw), F32),
                        pltpu.VMEM((2, HG_CHUNK, hw), F32),
                        pltpu.VMEM(w_in.shape, BF16),
                        pltpu.VMEM(w_out.shape, BF16)],
        compiler_params=pltpu.CompilerParams(
            dimension_semantics=("arbitrary",), vmem_limit_bytes=MIX_VMEM_LIMIT),
        name="mix",
    )(x, x, ada, ada, w_in, lb_logits, gnw, sglg, sglb, sg_w, sg_bt, w_out, ln1g, ln1b, wrs, br)


def _sort_kernel(nb, rt_ref, dest_ref, blk_ref, cnt_ref, pst_ref, carry_ref):
    ph = pl.program_id(0)
    first_step = (pl.program_id(1) == 0) & (pl.program_id(2) == 0)
    tk = rt_ref.shape[1]
    r = rt_ref[...]
    row = lax.broadcasted_iota(I32, (MOE_EXPERTS, tk), 0)
    oh1 = row == r[0:1].astype(I32)
    oh2 = row == r[1:2].astype(I32)
    hot = jnp.where(oh1 | oh2, 1.0, 0.0)

    @pl.when((ph == 0) & first_step)
    def _():
        cnt_ref[...] = jnp.zeros_like(cnt_ref)

    @pl.when(ph == 0)
    def _():
        cnt_ref[...] += jnp.sum(hot, axis=1, keepdims=True)

    @pl.when((ph == 1) & first_step)
    def _():
        nblk = jnp.floor((cnt_ref[...] + (MOE_ROWS - 1)) * (1.0 / MOE_ROWS))
        ei = lax.broadcasted_iota(I32, (MOE_EXPERTS, MOE_EXPERTS), 0)
        ej = lax.broadcasted_iota(I32, (MOE_EXPERTS, MOE_EXPERTS), 1)
        pstart = _dot((ej < ei).astype(BF16), nblk.astype(BF16))
        pst_ref[...] = pstart * float(MOE_ROWS)
        carry_ref[...] = jnp.zeros_like(carry_ref)
        lane_b = lax.broadcasted_iota(I32, blk_ref.shape, 1)
        blk_ref[...] = jnp.where(lane_b == 0, pstart, jnp.where(lane_b == 1, nblk, 0.0)).astype(I32)

    @pl.when(ph == 1)
    def _():
        ti = lax.broadcasted_iota(I32, (nb, nb), 0)
        tj = lax.broadcasted_iota(I32, (nb, nb), 1)
        before = (ti < tj).astype(BF16)
        carry = carry_ref[...] + pst_ref[...]
        rows1 = []
        rows2 = []
        for j in range(tk // nb):
            cs = slice(j * nb, (j + 1) * nb)
            hb = hot[:, cs]
            base = _dot(hb.astype(BF16), before) + carry[:, 0:1]
            rows1.append(jnp.sum(jnp.where(oh1[:, cs], base, 0.0), axis=0, keepdims=True))
            rows2.append(jnp.sum(jnp.where(oh2[:, cs], base, 0.0), axis=0, keepdims=True))
            carry = carry + jnp.sum(hb, axis=1, keepdims=True)
        d1 = jnp.concatenate(rows1, axis=1)
        d2 = jnp.concatenate(rows2, axis=1)
        out_row = lax.broadcasted_iota(I32, (SUBLANES, tk), 0)
        dest_ref[...] = jnp.where(out_row == 0, d1, jnp.where(out_row == 1, d2, 0.0)).astype(I32)
        carry_ref[...] = carry - pst_ref[...]


def _sort_call(route_t):
    bsz, _, s = route_t.shape
    tk = min(SORT_ROWS, s)
    nj = s // tk
    return pl.pallas_call(
        functools.partial(_sort_kernel, min(SORT_BLOCK, tk)),
        grid=(2, bsz, nj),
        in_specs=[pl.BlockSpec((None, SUBLANES, tk), lambda p, b, j: (b, 0, j))],
        out_specs=[pl.BlockSpec((SUBLANES, tk), lambda p, b, j: (0, (b * nj + j) * p)),
                   pl.BlockSpec((MOE_EXPERTS, LANES), lambda p, b, j: (0, 0))],
        out_shape=[jax.ShapeDtypeStruct((SUBLANES, bsz * s), I32),
                   jax.ShapeDtypeStruct((MOE_EXPERTS, LANES), I32)],
        scratch_shapes=[pltpu.VMEM((MOE_EXPERTS, LANES), F32), pltpu.VMEM((MOE_EXPERTS, LANES), F32),
                        pltpu.VMEM((MOE_EXPERTS, LANES), F32)],
        compiler_params=pltpu.CompilerParams(
            dimension_semantics=("arbitrary", "arbitrary", "arbitrary")),
        name="sort",
    )(route_t)


def _sc_mesh():
    return plsc.VectorSubcoreMesh(core_axis_name="c", subcore_axis_name="s")


def _sc_scatter_rows2(rows, idx_a, idx_b, n_out):
    n, w = rows.shape

    @pl.kernel(out_type=jax.ShapeDtypeStruct((n_out, w), rows.dtype), mesh=_sc_mesh(), scratch_types=[])
    def scatter(x_hbm, ia_hbm, ib_hbm, o_hbm):
        def body(x_vmem, ia_vmem, ib_vmem):
            pltpu.sync_copy(x_vmem, o_hbm.at[ia_vmem.at[0]])
            pltpu.sync_copy(x_vmem, o_hbm.at[ib_vmem.at[0]])

        pltpu.emit_pipeline(
            body,
            grid=(n // SC_WINDOW,),
            in_specs=[pl.BlockSpec((SC_WINDOW, w), lambda i: (i, 0)),
                      pl.BlockSpec((1, SC_WINDOW), lambda i: (0, i)),
                      pl.BlockSpec((1, SC_WINDOW), lambda i: (0, i))],
            out_specs=[],
            core_axis_name=("c", "s"),
            dimension_semantics=(pltpu.PARALLEL,),
        )(x_hbm, ia_hbm, ib_hbm)

    return scatter(rows, idx_a.reshape(1, n), idx_b.reshape(1, n))


def _sc_gather_rows(src, idx):
    n = idx.shape[0]
    w = src.shape[1]

    @pl.kernel(out_type=jax.ShapeDtypeStruct((n, w), src.dtype), mesh=_sc_mesh(), scratch_types=[])
    def gather(x_hbm, i_hbm, o_hbm):
        def body(i_vmem, o_vmem):
            pltpu.sync_copy(x_hbm.at[i_vmem.at[0]], o_vmem)

        pltpu.emit_pipeline(
            body,
            grid=(n // SC_WINDOW,),
            in_specs=[pl.BlockSpec((1, SC_WINDOW), lambda i: (0, i))],
            out_specs=[pl.BlockSpec((SC_WINDOW, w), lambda i: (i, 0))],
            core_axis_name=("c", "s"),
            dimension_semantics=(pltpu.PARALLEL,),
        )(i_hbm, o_hbm)

    return gather(src, idx.reshape(1, n))


def _moe_kernel(n_in, first_ref, count_ref, *refs):
    xb_hbm = refs[:n_in]
    wup_ref, wdn_ref = refs[n_in:n_in + 2]
    y_hbm = refs[n_in + 2:2 * n_in + 2]
    wupb_ref, wdnb_ref, xbuf, ybuf, in_sem, out_sem = refs[2 * n_in + 2:]
    e = pl.program_id(0)
    n_exp = pl.num_programs(0)
    first = first_ref[e]
    count = count_ref[e]
    total = first_ref[n_exp - 1] + count_ref[n_exp - 1]

    def rows_of(g):
        return pl.ds(pl.multiple_of(g * MOE_ROWS, MOE_ROWS), MOE_ROWS)

    def in_copy(g, i):
        slot = g % MOE_IN_BUFS
        return pltpu.make_async_copy(xb_hbm[i].at[rows_of(g)], xbuf.at[slot, i], in_sem.at[slot, i])

    def out_copy(g, i):
        slot = g % MOE_OUT_BUFS
        return pltpu.make_async_copy(ybuf.at[slot, i], y_hbm[i].at[rows_of(g)], out_sem.at[slot, i])

    @pl.when(e == 0)
    def _():
        for g in range(MOE_AHEAD):
            @pl.when(g < total)
            def _():
                for i in range(n_in):
                    in_copy(g, i).start()

    @pl.when(count > 0)
    def _():
        wupb_ref[...] = wup_ref[...].astype(BF16)
        wdnb_ref[...] = wdn_ref[...].astype(BF16)

        def run_blocks(gs):
            for g in gs:
                for i in range(n_in):
                    in_copy(g, i).wait()
            for g in gs:
                @pl.when(g + MOE_AHEAD < total)
                def _():
                    for i in range(n_in):
                        in_copy(g + MOE_AHEAD, i).start()
            for g in gs:
                @pl.when(g >= MOE_OUT_BUFS)
                def _():
                    for i in range(n_in):
                        out_copy(g - MOE_OUT_BUFS, i).wait()
            w = jnp.concatenate(
                [jnp.concatenate([xbuf[g % MOE_IN_BUFS, i] for i in range(n_in)], axis=1)
                 for g in gs], axis=0)
            gu = _dot(_unpack_bf16_pairs(w).astype(BF16), wupb_ref[...])
            gate = gu[:, :MOE_HIDDEN]
            act = (gate * _sigmoid(gate) * gu[:, MOE_HIDDEN:]).astype(BF16)
            y = _pack_bf16_pairs(_dot(act, wdnb_ref[...]))
            for k, g in enumerate(gs):
                for i in range(n_in):
                    ybuf[g % MOE_OUT_BUFS, i] = y[k * MOE_ROWS:(k + 1) * MOE_ROWS,
                                                  i * SC_ROW_WORDS:(i + 1) * SC_ROW_WORDS]
                    out_copy(g, i).start()

        def full_group(p, carry):
            g = first + MOE_TALL * p
            run_blocks([g + k for k in range(MOE_TALL)])
            return carry

        n_full = count // MOE_TALL
        lax.fori_loop(0, n_full, full_group, 0)
        done = n_full * MOE_TALL
        size = MOE_TALL // 2
        while size >= 1:
            start = first + done + ((count - done) // (2 * size)) * (2 * size)

            @pl.when(((count - done) // size) % 2 == 1)
            def _(start=start, size=size):
                run_blocks([start + k for k in range(size)])

            size //= 2

    @pl.when(e == n_exp - 1)
    def _():
        for k in range(MOE_OUT_BUFS):
            @pl.when(total - 1 - k >= 0)
            def _():
                for i in range(n_in):
                    out_copy(total - 1 - k, i).wait()


def _moe_call(first_block, block_count, xb_pieces, w_up, w_down):
    n_rows = xb_pieces[0].shape[0]
    n_in = len(xb_pieces)
    d = 2 * n_in * SC_ROW_WORDS
    n_exp, _, hid2 = w_up.shape
    hid = w_down.shape[1]
    hbm = pl.BlockSpec(memory_space=pl.ANY)
    buf = lambda n: pltpu.VMEM((n, n_in, MOE_ROWS, SC_ROW_WORDS), U32)
    return pl.pallas_call(
        functools.partial(_moe_kernel, n_in),
        grid_spec=pltpu.PrefetchScalarGridSpec(
            num_scalar_prefetch=2,
            grid=(n_exp,),
            in_specs=[hbm] * n_in + [
                      pl.BlockSpec((None, d, hid2), lambda e, fb, bc: (e, 0, 0)),
                      pl.BlockSpec((None, hid, d), lambda e, fb, bc: (e, 0, 0))],
            out_specs=[hbm] * n_in,
            scratch_shapes=[pltpu.VMEM((d, hid2), BF16), pltpu.VMEM((hid, d), BF16),
                            buf(MOE_IN_BUFS), buf(MOE_OUT_BUFS),
                            pltpu.SemaphoreType.DMA((MOE_IN_BUFS, n_in)),
                            pltpu.SemaphoreType.DMA((MOE_OUT_BUFS, n_in))]),
        out_shape=[jax.ShapeDtypeStruct((n_rows, SC_ROW_WORDS), U32)] * n_in,
        compiler_params=pltpu.CompilerParams(
            dimension_semantics=("arbitrary",), vmem_limit_bytes=VMEM_LIMIT),
        name="moe",
    )(first_block, block_count, *xb_pieces, w_up, w_down)


def _combine_kernel(n_pieces, x1_ref, routet_ref, ada_ref, g_ref, b_ref, *refs):
    o_ref = refs[-1]
    ya = _unpack_bf16_pairs(jnp.concatenate([r[...] for r in refs[:n_pieces]], axis=1))
    yb = _unpack_bf16_pairs(jnp.concatenate([r[...] for r in refs[n_pieces:2 * n_pieces]], axis=1))
    rt = routet_ref[...]
    r = jnp.concatenate([rt, jnp.zeros((LANES - rt.shape[0], rt.shape[1]), F32)], axis=0).T
    m = ya * r[:, 2:3] + yb * r[:, 3:4]
    g2 = ada_ref[5:6]
    o_ref[...] = _layernorm(ALPHA * x1_ref[...] + (1.0 + g2) * m, g_ref[...], b_ref[...])


def _combine_call(b, out_prev, x1, y_pieces, route_t, ada, ln2g, ln2b):
    bsz, s, d = x1.shape
    ts = min(OUT_ROWS, s)
    nj = s // ts
    n_pieces = len(y_pieces)
    slot = lambda k: pl.BlockSpec((ts, SC_ROW_WORDS), lambda j: (k * nj + j, 0))
    in_specs = ([pl.BlockSpec((None, ts, d), lambda j: (b, j, 0)),
                 pl.BlockSpec((None, SUBLANES, ts), lambda j: (b, 0, j)),
                 pl.BlockSpec((None,) + ada.shape[1:], lambda j: (b, 0, 0)),
                 pl.BlockSpec((1, d), lambda j: (0, 0)),
                 pl.BlockSpec((1, d), lambda j: (0, 0))]
                + [slot(0)] * n_pieces + [slot(1)] * n_pieces)
    args = [x1, route_t, ada, ln2g, ln2b, *y_pieces, *y_pieces]
    aliases = {}
    if out_prev is not None:
        in_specs.append(pl.BlockSpec(memory_space=pl.ANY))
        aliases = {len(args): 0}
        args.append(out_prev)
    n_extra = len(args) - 5 - 2 * n_pieces

    def body(*refs):
        _combine_kernel(n_pieces, *refs[:5 + 2 * n_pieces], *refs[5 + 2 * n_pieces + n_extra:])

    return pl.pallas_call(
        body,
        grid=(nj,),
        in_specs=in_specs,
        out_specs=pl.BlockSpec((None, ts, d), lambda j: (b, j, 0)),
        out_shape=jax.ShapeDtypeStruct((bsz, s, d), F32),
        input_output_aliases=aliases,
        compiler_params=pltpu.CompilerParams(dimension_semantics=("arbitrary",)),
        name="combine",
    )(*args)


def kernel(x, c, w_ada, b_ada, w_in, lb_logits, hg_norm_w, sg_ln_g, sg_ln_b, sg_w, sg_b, w_out, ln1_g, ln1_b, router_group_w, router_group_b, router_expert_w, router_expert_b, w_up, w_down, ln2_g, ln2_b):
    assert w_in.shape[0] == DEPTH
    bsz, s, d = x.shape
    t = bsz * s
    l = 0

    ada = _ada_call(c, w_ada[l], b_ada[l])

    wr = jnp.concatenate(
        [router_group_w[l].T, router_expert_w[l].transpose(0, 2, 1).reshape(MOE_EXPERTS, d)], axis=0)
    wr = jnp.pad(wr, ((0, ROUTE_ROWS - wr.shape[0]), (0, 0)))
    br = jnp.concatenate([router_group_b[l], router_expert_b[l].reshape(MOE_EXPERTS)])
    br = jnp.broadcast_to(jnp.pad(br, (0, ROUTE_ROWS - br.shape[0]))[:, None], (ROUTE_ROWS, LANES))
    wrs = jnp.concatenate(_split_bf16(wr), axis=0)

    x1, route_t, *h2p = _mix_call(
        x, ada, w_in[l], lb_logits, hg_norm_w[l].reshape(1, -1),
        sg_ln_g[l].reshape(1, -1), sg_ln_b[l].reshape(1, -1), sg_w[l], sg_b[l].T,
        w_out[l], ln1_g[l].reshape(1, d), ln1_b[l].reshape(1, d), wrs, br)

    n_blocks = -(-(2 * t) // MOE_ROWS) + MOE_EXPERTS
    n_rows = n_blocks * MOE_ROWS
    dest, blk = _sort_call(route_t)
    d0, d1 = dest[0], dest[1]

    xb = [_sc_scatter_rows2(p.reshape(t, SC_ROW_WORDS), d0, d1, n_rows) for p in h2p]
    yb = _moe_call(blk[:, 0], blk[:, 1], xb, w_up[l], w_down[l])
    out = None
    for b in range(bsz):
        tok = slice(b * s, (b + 1) * s)
        idx = jnp.concatenate([d0[tok], d1[tok]])
        y2 = [_sc_gather_rows(p, idx) for p in yb]
        out = _combine_call(b, out, x1, y2, route_t, ada, ln2_g[l].reshape(1, d), ln2_b[l].reshape(1, d))
    return out
```

```python
import functools

import jax
import jax.numpy as jnp
from jax import lax
from jax.experimental import pallas as pl
from jax.experimental.pallas import tpu as pltpu
from jax.experimental.pallas import tpu_sc as plsc

F32 = jnp.float32
BF16 = jnp.bfloat16
I32 = jnp.int32
U32 = jnp.uint32

HG_HEADS = 4
HG_DK = 128
HG_CHUNK = 64
HG_SUB = 8
SG_GROUPS = 4
SG_CH = 128
SG_CHUNK = 128
MOE_GROUPS = 4
MOE_EPG = 8
MOE_EXPERTS = MOE_GROUPS * MOE_EPG
MOE_HIDDEN = 512
DEPTH = 1
ALPHA = (2.0 * DEPTH) ** 0.25
LN_EPS = 1e-5
RMS_EPS = 1e-6
LOG2E = 1.4426950408889634

LANES = 128
SUBLANES = 8
SC_WINDOW = 128
SC_ROW_WORDS = 256

MIX_ROWS = 512
PROJ_PIECE = 256
SORT_ROWS = 4096
SORT_BLOCK = 256
ROUTE_ROWS = -(-(MOE_GROUPS + MOE_EXPERTS) // SUBLANES) * SUBLANES
MOE_ROWS = 256
MOE_TALL = 4
MOE_AHEAD = 4
MOE_IN_BUFS = MOE_AHEAD + MOE_TALL
MOE_OUT_BUFS = 2 * MOE_TALL
OUT_ROWS = 512
MOE_VMEM_LIMIT = 52 * 1024 * 1024
MIX_VMEM_LIMIT = 60 * 1024 * 1024


def _dot(a, b):
    return jnp.dot(a, b, preferred_element_type=F32)


def _dot_nt(a, b):
    return lax.dot_general(a, b, (((1,), (1,)), ((), ())), preferred_element_type=F32)


def _dot_tn(a, b):
    return lax.dot_general(a, b, (((0,), (0,)), ((), ())), preferred_element_type=F32)


def _sigmoid(x):
    return jax.nn.sigmoid(x)


def _gelu_exact(x):
    return 0.5 * x * (1.0 + lax.erf(x * (2.0 ** -0.5)))


def _layernorm(x, g, b):
    mu = jnp.mean(x, axis=-1, keepdims=True)
    xc = x - mu
    var = jnp.mean(xc * xc, axis=-1, keepdims=True)
    return xc * lax.rsqrt(var + LN_EPS) * g + b


def _pack_bf16_pairs(x):
    n = x.shape[1]
    bits = lax.bitcast_convert_type(x.astype(BF16).astype(F32), U32)
    return bits[:, n // 2:] | (bits[:, :n // 2] >> 16)


def _unpack_bf16_pairs(w):
    lo = lax.bitcast_convert_type(w << 16, F32)
    hi = lax.bitcast_convert_type(w & jnp.uint32(0xFFFF0000), F32)
    return jnp.concatenate([lo, hi], axis=1)


def _split_bf16(x):
    hi = x.astype(BF16)
    lo = (x - hi.astype(F32)).astype(BF16)
    return hi, lo


def _ada_kernel(ct_ref, w_ref, b_ref, o_ref):
    ct = ct_ref[...]
    ca = ct * _sigmoid(ct)
    w = w_ref[...]
    rows = [jnp.sum(ca[:, b:b + 1] * w, axis=0, keepdims=True) for b in range(ct.shape[1])]
    pad = o_ref.shape[0] - len(rows)
    if pad:
        rows.append(jnp.zeros((pad, w.shape[1]), F32))
    o_ref[...] = jnp.concatenate(rows, axis=0) + b_ref[...]


def _ada_call(c, w_ada, b_ada):
    bsz, d = c.shape
    n = w_ada.shape[1]
    rows = -(-bsz // SUBLANES) * SUBLANES
    out = pl.pallas_call(
        _ada_kernel,
        grid=(n // d,),
        in_specs=[pl.BlockSpec((d, bsz), lambda i: (0, 0)),
                  pl.BlockSpec((d, d), lambda i: (0, i)),
                  pl.BlockSpec((1, d), lambda i: (0, i))],
        out_specs=pl.BlockSpec((rows, d), lambda i: (0, i)),
        out_shape=jax.ShapeDtypeStruct((rows, n), F32),
        name="ada",
    )(c.T, w_ada, b_ada.reshape(1, n))
    return out[:bsz].reshape(bsz, n // d, d)


def _hgrn2_matmul_phase(c, proj_ref, st_ref, g_ref, h_ref, lb, tril):
    hw = HG_HEADS * HG_DK
    cl = HG_CHUNK
    rows = pl.ds(c * cl, cl)
    qz = proj_ref[rows, 0:hw]
    fz = proj_ref[rows, hw:2 * hw]
    v = proj_ref[rows, 2 * hw:3 * hw]

    q = qz * _sigmoid(qz)
    f = lb + (1.0 - lb) * _sigmoid(fz)
    lf = jnp.log(f)
    k = 1.0 - f
    lf_hi, lf_lo = _split_bf16(lf)
    g = (_dot(tril, lf_hi) + _dot(tril, lf_lo)) * LOG2E
    glast = g[cl - 1:cl, :]
    qg = (q * jnp.exp2(g)).astype(BF16)
    kd = (k * jnp.exp2(glast - g)).astype(BF16)
    vb = v.astype(BF16)
    g_ref[...] = g
    h_ref[...] = g - jnp.log(k) * LOG2E

    n_sub = cl // HG_SUB
    o_state = []
    acc_all = []
    for hd in range(HG_HEADS):
        cs = slice(hd * HG_DK, (hd + 1) * HG_DK)
        st = st_ref[hd]
        o_state.append(_dot_nt(qg[:, cs], st.astype(BF16)))
        st_ref[hd] = st * jnp.exp2(glast[:, cs]) + _dot_tn(vb[:, cs], kd[:, cs])
        gh = g[:, cs]
        kh = k[:, cs]
        qh = q[:, cs]
        acc = [None] * n_sub
        w = cl // 2
        while w >= HG_SUB:
            for p in range(0, cl, 2 * w):
                ref = g_ref[p + w - 1:p + w, cs]
                qt = (qh[p + w:p + 2 * w] * jnp.exp2(gh[p + w:p + 2 * w] - ref)).astype(BF16)
                kt = (kh[p:p + w] * jnp.exp2(ref - gh[p:p + w])).astype(BF16)
                pieces = [kt]
                if p:
                    pieces.insert(0, jnp.zeros((p, HG_DK), BF16))
                if cl - p - w:
                    pieces.append(jnp.zeros((cl - p - w, HG_DK), BF16))
                blk = _dot_nt(qt, jnp.concatenate(pieces, axis=0))
                for r in range(w // HG_SUB):
                    i = (p + w) // HG_SUB + r
                    part = blk[r * HG_SUB:(r + 1) * HG_SUB]
                    acc[i] = part if acc[i] is None else acc[i] + part
            w //= 2
        acc_all.append(acc)
    return q, g, vb, o_state, acc_all


def _hgrn2_diag_phase(c, phase1, proj_ref, y_ref, h_ref, gnw):
    q, g, vb, o_state, acc_all = phase1
    hw = HG_HEADS * HG_DK
    cl = HG_CHUNK
    rows = pl.ds(c * cl, cl)
    og = proj_ref[rows, 3 * hw:4 * hw]
    n_sub = cl // HG_SUB
    lane_c = lax.broadcasted_iota(I32, (HG_SUB, cl), 1)
    trow = lax.broadcasted_iota(I32, (HG_SUB, cl), 0)
    for hd in range(HG_HEADS):
        cs = slice(hd * HG_DK, (hd + 1) * HG_DK)
        gh = g[:, cs]
        qh = q[:, cs]
        a_rows = []
        for i in range(n_sub):
            b0 = i * HG_SUB
            gb = gh[b0:b0 + HG_SUB]
            qb = qh[b0:b0 + HG_SUB]
            a = jnp.zeros((HG_SUB, cl), F32) if acc_all[hd][i] is None else acc_all[hd][i]
            for s in range(HG_SUB):
                hs = h_ref[b0 + s:b0 + s + 1, cs]
                col = jnp.sum(qb * jnp.exp2(gb - hs), axis=-1, keepdims=True)
                a = jnp.where(lane_c == b0 + s, col, a)
            a_rows.append(jnp.where(lane_c <= b0 + trow, a, 0.0))
        amat = jnp.concatenate(a_rows, axis=0).astype(BF16)
        o = o_state[hd] + _dot(amat, vb[:, cs])
        ms = jnp.mean(o * o, axis=-1, keepdims=True)
        ogh = og[:, cs]
        ya = o * lax.rsqrt(ms + RMS_EPS) * gnw * (ogh * _sigmoid(ogh))
        y_ref[rows, cs] = ya.astype(BF16)


def _route_t(logits):
    nr, ts = logits.shape
    row = lax.broadcasted_iota(I32, (nr, ts), 0)
    neg = -jnp.inf
    gl = jnp.where(row < MOE_GROUPS, logits, neg)
    gmax = jnp.max(gl, axis=0, keepdims=True)
    gidx = jnp.min(jnp.where(gl == gmax, row, nr), axis=0, keepdims=True)
    p_group = 1.0 / jnp.sum(jnp.exp(gl - gmax), axis=0, keepdims=True)
    e_lo = MOE_GROUPS + gidx * MOE_EPG
    el = jnp.where((row >= e_lo) & (row < e_lo + MOE_EPG), logits, neg)
    m1 = jnp.max(el, axis=0, keepdims=True)
    i1 = jnp.min(jnp.where(el == m1, row, nr), axis=0, keepdims=True)
    el2 = jnp.where(row == i1, neg, el)
    m2 = jnp.max(el2, axis=0, keepdims=True)
    i2 = jnp.min(jnp.where(el2 == m2, row, nr), axis=0, keepdims=True)
    esum = jnp.sum(jnp.exp(el - m1), axis=0, keepdims=True)
    p1 = 1.0 / esum
    p2 = jnp.exp(m2 - m1) / esum
    den = p1 + p2
    w1 = p_group * p1 / den
    w2 = p_group * p2 / den
    e1 = (i1 - MOE_GROUPS).astype(F32)
    e2 = (i2 - MOE_GROUPS).astype(F32)
    out_row = lax.broadcasted_iota(I32, (SUBLANES, ts), 0)
    return jnp.where(out_row == 0, e1, jnp.where(out_row == 1, e2,
                     jnp.where(out_row == 2, w1, jnp.where(out_row == 3, w2, 0.0))))


def _mix_kernel(nt, x_ref, xn_ref, ada_ref, adan_ref, win32_ref, lbl_ref, gnw_ref, sglg_ref,
                sglb_ref, sgw_ref, sgbt_ref, wout32_ref, ln1g_ref, ln1b_ref, wrs_ref, br_ref,
                x1_ref, routet_ref, *rest):
    h2p_refs = rest[:-8]
    st_ref, proj_ref, projn_ref, y_ref, g_ref, h_ref, win_ref, wout_ref = rest[-8:]
    ts, d = x_ref.shape
    hw = HG_HEADS * HG_DK
    sgw = SG_GROUPS * SG_CH
    ncol = proj_ref.shape[1]
    step = pl.program_id(0)

    def modulated(xr, adar):
        a = adar[...]
        return (xr[...] * (1.0 + a[1:2]) + a[0:1]).astype(BF16)

    @pl.when(step % nt == 0)
    def _():
        st_ref[...] = jnp.zeros_like(st_ref)

    @pl.when(step == 0)
    def _():
        win_ref[...] = win32_ref[...].astype(BF16)
        wout_ref[...] = wout32_ref[...].astype(BF16)
        proj_ref[...] = _dot(modulated(x_ref, ada_ref), win_ref[...])

    x = x_ref[...]
    ada = ada_ref[...]
    g1, sh2, sc2 = ada[2:3], ada[3:4], ada[4:5]
    hn = modulated(xn_ref, adan_ref)

    lbl = lbl_ref[...]
    slots = [lbl[i:i + 1] for i in range(lbl.shape[0])]
    mx = functools.reduce(jnp.maximum, slots)
    ex = [jnp.exp(s - mx) for s in slots]
    lb = ex[0] / functools.reduce(lambda a, b: a + b, ex)

    ci = lax.broadcasted_iota(I32, (HG_CHUNK, HG_CHUNK), 0)
    cj = lax.broadcasted_iota(I32, (HG_CHUNK, HG_CHUNK), 1)
    tril = (ci >= cj).astype(BF16)
    gnw = gnw_ref[...]

    n_chunks = ts // HG_CHUNK
    n_sg = ts // SG_CHUNK
    n_pp = ncol // PROJ_PIECE
    n_tail = max(n_pp - n_chunks, 0)
    slot_of = [p if p < n_pp - n_tail else n_chunks + (2 * (p - (n_pp - n_tail))) // max(n_tail, 1)
               for p in range(n_pp)]

    def next_proj_pieces(slot):
        for p in range(n_pp):
            if min(slot_of[p], n_chunks + 1) == slot:
                cols = slice(p * PROJ_PIECE, (p + 1) * PROJ_PIECE)
                projn_ref[:, cols] = _dot(hn, win_ref[:, cols])

    pi = lax.broadcasted_iota(I32, (SG_CHUNK, SG_CHUNK), 0)
    pj = lax.broadcasted_iota(I32, (SG_CHUNK, SG_CHUNK), 1)
    sglg = sglg_ref[...]
    sglb = sglb_ref[...]

    def spatial_gating(p):
        rows = slice(p * SG_CHUNK, (p + 1) * SG_CHUNK)
        z = _gelu_exact(proj_ref[rows, 4 * hw:4 * hw + 2 * sgw])
        u = z[:, :sgw]
        vn = _layernorm(z[:, sgw:], sglg, sglb).astype(BF16)
        for gi in range(SG_GROUPS):
            cs = slice(gi * SG_CH, (gi + 1) * SG_CH)
            wc = jnp.where(pi >= pj, sgw_ref[gi], 0.0).astype(BF16)
            mixed = _dot(wc, vn[:, cs]) + sgbt_ref[:, gi:gi + 1]
            y_ref[rows, hw + gi * SG_CH:hw + (gi + 1) * SG_CH] = (u[:, cs] * mixed).astype(BF16)

    phase1 = _hgrn2_matmul_phase(0, proj_ref, st_ref, g_ref.at[0], h_ref.at[0], lb, tril)
    for c in range(n_chunks):
        next_proj_pieces(c)
        if c + 1 < n_chunks:
            nxt = (c + 1) % 2
            phase1_next = _hgrn2_matmul_phase(c + 1, proj_ref, st_ref, g_ref.at[nxt], h_ref.at[nxt],
                                              lb, tril)
        _hgrn2_diag_phase(c, phase1, proj_ref, y_ref, h_ref.at[c % 2], gnw)
        phase1 = phase1_next
    for p in range(n_sg):
        spatial_gating(p)

    y = _dot(y_ref[...], wout_ref[...])
    next_proj_pieces(n_chunks)
    x1 = _layernorm(ALPHA * x + (1.0 + g1) * y, ln1g_ref[...], ln1b_ref[...])
    x1_ref[...] = x1
    h2 = x1 * (1.0 + sc2) + sh2
    h2_hi, h2_lo = _split_bf16(h2)
    packed = _pack_bf16_pairs(h2)
    for i, ref in enumerate(h2p_refs):
        ref[...] = packed[:, i * SC_ROW_WORDS:(i + 1) * SC_ROW_WORDS]
    nr = br_ref.shape[0]
    part = _dot_nt(wrs_ref[...], h2_hi)
    logits = part[:nr] + part[nr:] + _dot_nt(wrs_ref[:nr, :], h2_lo) + br_ref[:, 0:1]
    next_proj_pieces(n_chunks + 1)
    routet_ref[...] = _route_t(logits)
    proj_ref[...] = projn_ref[...]


def _mix_call(x, ada, w_in, lb_logits, gnw, sglg, sglb, sg_w, sg_bt, w_out, ln1g, ln1b, wrs, br):
    bsz, s, d = x.shape
    ts = min(MIX_ROWS, s)
    nt = s // ts
    n_steps = bsz * nt
    ncol = w_in.shape[1]
    hw = HG_HEADS * HG_DK
    n_pieces = d // 2 // SC_ROW_WORDS
    const = lambda a: pl.BlockSpec(a.shape, lambda i: (0,) * a.ndim, pipeline_mode=pl.Buffered(1))
    cur = lambda i: (i // nt, i % nt, 0)
    nxt = lambda i: cur(jnp.minimum(i + 1, n_steps - 1))
    return pl.pallas_call(
        functools.partial(_mix_kernel, nt),
        grid=(n_steps,),
        in_specs=[pl.BlockSpec((None, ts, d), cur),
                  pl.BlockSpec((None, ts, d), nxt),
                  pl.BlockSpec((None,) + ada.shape[1:], lambda i: (cur(i)[0], 0, 0)),
                  pl.BlockSpec((None,) + ada.shape[1:], lambda i: (nxt(i)[0], 0, 0)),
                  const(w_in), const(lb_logits), const(gnw), const(sglg), const(sglb), const(sg_w),
                  const(sg_bt), const(w_out), const(ln1g), const(ln1b), const(wrs), const(br)],
        out_specs=[pl.BlockSpec((None, ts, d), cur),
                   pl.BlockSpec((None, SUBLANES, ts), lambda i: (i // nt, 0, i % nt))]
                  + [pl.BlockSpec((None, ts, SC_ROW_WORDS), cur)] * n_pieces,
        out_shape=[jax.ShapeDtypeStruct((bsz, s, d), F32),
                   jax.ShapeDtypeStruct((bsz, SUBLANES, s), F32)]
                  + [jax.ShapeDtypeStruct((bsz, s, SC_ROW_WORDS), U32)] * n_pieces,
        scratch_shapes=[pltpu.VMEM((HG_HEADS, HG_DK, HG_DK), F32),
                        pltpu.VMEM((ts, ncol), F32),
                        pltpu.VMEM((ts, ncol), F32),
                        pltpu.VMEM((ts, d), BF16),
                        pltpu.VMEM((2, HG_CHUNK, hw), F32),
                        pltpu.VMEM((2, HG_CHUNK, hw), F32),
                        pltpu.VMEM(w_in.shape, BF16),
                        pltpu.VMEM(w_out.shape, BF16)],
        compiler_params=pltpu.CompilerParams(
            dimension_semantics=("arbitrary",), vmem_limit_bytes=MIX_VMEM_LIMIT),
        name="mix",
    )(x, x, ada, ada, w_in, lb_logits, gnw, sglg, sglb, sg_w, sg_bt, w_out, ln1g, ln1b, wrs, br)


def _sort_kernel(nb, rt_ref, dest_ref, blk_ref, cnt_ref, pst_ref, carry_ref):
    ph = pl.program_id(0)
    first_step = (pl.program_id(1) == 0) & (pl.program_id(2) == 0)
    tk = rt_ref.shape[1]
    r = rt_ref[...]
    row = lax.broadcasted_iota(I32, (MOE_EXPERTS, tk), 0)
    oh1 = row == r[0:1].astype(I32)
    oh2 = row == r[1:2].astype(I32)
    hot = jnp.where(oh1 | oh2, 1.0, 0.0)

    @pl.when((ph == 0) & first_step)
    def _():
        cnt_ref[...] = jnp.zeros_like(cnt_ref)

    @pl.when(ph == 0)
    def _():
        cnt_ref[...] += jnp.sum(hot, axis=1, keepdims=True)

    @pl.when((ph == 1) & first_step)
    def _():
        nblk = jnp.floor((cnt_ref[...] + (MOE_ROWS - 1)) * (1.0 / MOE_ROWS))
        ei = lax.broadcasted_iota(I32, (MOE_EXPERTS, MOE_EXPERTS), 0)
        ej = lax.broadcasted_iota(I32, (MOE_EXPERTS, MOE_EXPERTS), 1)
        pstart = _dot((ej < ei).astype(BF16), nblk.astype(BF16))
        pst_ref[...] = pstart * float(MOE_ROWS)
        carry_ref[...] = jnp.zeros_like(carry_ref)
        lane_b = lax.broadcasted_iota(I32, blk_ref.shape, 1)
        blk_ref[...] = jnp.where(lane_b == 0, pstart, jnp.where(lane_b == 1, nblk, 0.0)).astype(I32)

    @pl.when(ph == 1)
    def _():
        ti = lax.broadcasted_iota(I32, (nb, nb), 0)
        tj = lax.broadcasted_iota(I32, (nb, nb), 1)
        before = (ti < tj).astype(BF16)
        carry = carry_ref[...] + pst_ref[...]
        rows1 = []
        rows2 = []
        for j in range(tk // nb):
            cs = slice(j * nb, (j + 1) * nb)
            hb = hot[:, cs]
            base = _dot(hb.astype(BF16), before) + carry[:, 0:1]
            rows1.append(jnp.sum(jnp.where(oh1[:, cs], base, 0.0), axis=0, keepdims=True))
            rows2.append(jnp.sum(jnp.where(oh2[:, cs], base, 0.0), axis=0, keepdims=True))
            carry = carry + jnp.sum(hb, axis=1, keepdims=True)
        d1 = jnp.concatenate(rows1, axis=1)
        d2 = jnp.concatenate(rows2, axis=1)
        out_row = lax.broadcasted_iota(I32, (SUBLANES, tk), 0)
        dest_ref[...] = jnp.where(out_row == 0, d1, jnp.where(out_row == 1, d2, 0.0)).astype(I32)
        carry_ref[...] = carry - pst_ref[...]


def _sort_call(route_t):
    bsz, _, s = route_t.shape
    tk = min(SORT_ROWS, s)
    nj = s // tk
    return pl.pallas_call(
        functools.partial(_sort_kernel, min(SORT_BLOCK, tk)),
        grid=(2, bsz, nj),
        in_specs=[pl.BlockSpec((None, SUBLANES, tk), lambda p, b, j: (b, 0, j))],
        out_specs=[pl.BlockSpec((SUBLANES, tk), lambda p, b, j: (0, (b * nj + j) * p)),
                   pl.BlockSpec((MOE_EXPERTS, LANES), lambda p, b, j: (0, 0))],
        out_shape=[jax.ShapeDtypeStruct((SUBLANES, bsz * s), I32),
                   jax.ShapeDtypeStruct((MOE_EXPERTS, LANES), I32)],
        scratch_shapes=[pltpu.VMEM((MOE_EXPERTS, LANES), F32), pltpu.VMEM((MOE_EXPERTS, LANES), F32),
                        pltpu.VMEM((MOE_EXPERTS, LANES), F32)],
        compiler_params=pltpu.CompilerParams(
            dimension_semantics=("arbitrary", "arbitrary", "arbitrary")),
        name="sort",
    )(route_t)


def _sc_mesh():
    return plsc.VectorSubcoreMesh(core_axis_name="c", subcore_axis_name="s")


def _sc_scatter_rows2(rows, idx_a, idx_b, n_out):
    n, w = rows.shape

    @pl.kernel(out_type=jax.ShapeDtypeStruct((n_out, w), rows.dtype), mesh=_sc_mesh(), scratch_types=[])
    def scatter(x_hbm, ia_hbm, ib_hbm, o_hbm):
        def body(x_vmem, ia_vmem, ib_vmem):
            pltpu.sync_copy(x_vmem, o_hbm.at[ia_vmem.at[0]])
            pltpu.sync_copy(x_vmem, o_hbm.at[ib_vmem.at[0]])

        pltpu.emit_pipeline(
            body,
            grid=(n // SC_WINDOW,),
            in_specs=[pl.BlockSpec((SC_WINDOW, w), lambda i: (i, 0)),
                      pl.BlockSpec((1, SC_WINDOW), lambda i: (0, i)),
                      pl.BlockSpec((1, SC_WINDOW), lambda i: (0, i))],
            out_specs=[],
            core_axis_name=("c", "s"),
            dimension_semantics=(pltpu.PARALLEL,),
        )(x_hbm, ia_hbm, ib_hbm)

    return scatter(rows, idx_a.reshape(1, n), idx_b.reshape(1, n))


def _sc_gather_rows(src, idx):
    n = idx.shape[0]
    w = src.shape[1]

    @pl.kernel(out_type=jax.ShapeDtypeStruct((n, w), src.dtype), mesh=_sc_mesh(), scratch_types=[])
    def gather(x_hbm, i_hbm, o_hbm):
        def body(i_vmem, o_vmem):
            pltpu.sync_copy(x_hbm.at[i_vmem.at[0]], o_vmem)

        pltpu.emit_pipeline(
            body,
            grid=(n // SC_WINDOW,),
            in_specs=[pl.BlockSpec((1, SC_WINDOW), lambda i: (0, i))],
            out_specs=[pl.BlockSpec((SC_WINDOW, w), lambda i: (i, 0))],
            core_axis_name=("c", "s"),
            dimension_semantics=(pltpu.PARALLEL,),
        )(i_hbm, o_hbm)

    return gather(src, idx.reshape(1, n))


def _moe_kernel(n_in, first_ref, count_ref, *refs):
    xb_hbm = refs[:n_in]
    wup_ref, wdn_ref = refs[n_in:n_in + 2]
    y_hbm = refs[n_in + 2:2 * n_in + 2]
    wupb_ref, wdnb_ref, xbuf, ybuf, in_sem, out_sem = refs[2 * n_in + 2:]
    e = pl.program_id(0)
    n_exp = pl.num_programs(0)
    first = first_ref[e]
    count = count_ref[e]
    total = first_ref[n_exp - 1] + count_ref[n_exp - 1]

    def rows_of(g):
        return pl.ds(pl.multiple_of(g * MOE_ROWS, MOE_ROWS), MOE_ROWS)

    def in_copy(g, i):
        slot = g % MOE_IN_BUFS
        return pltpu.make_async_copy(xb_hbm[i].at[rows_of(g)], xbuf.at[slot, i], in_sem.at[slot, i])

    def out_copy(g, i):
        slot = g % MOE_OUT_BUFS
        return pltpu.make_async_copy(ybuf.at[slot, i], y_hbm[i].at[rows_of(g)], out_sem.at[slot, i])

    @pl.when(e == 0)
    def _():
        for g in range(MOE_AHEAD):
            @pl.when(g < total)
            def _():
                for i in range(n_in):
                    in_copy(g, i).start()

    @pl.when(count > 0)
    def _():
        wupb_ref[...] = wup_ref[...].astype(BF16)
        wdnb_ref[...] = wdn_ref[...].astype(BF16)

        def run_blocks(gs):
            for g in gs:
                for i in range(n_in):
                    in_copy(g, i).wait()
            for g in gs:
                @pl.when(g + MOE_AHEAD < total)
                def _():
                    for i in range(n_in):
                        in_copy(g + MOE_AHEAD, i).start()
            for g in gs:
                @pl.when(g >= MOE_OUT_BUFS)
                def _():
                    for i in range(n_in):
                        out_copy(g - MOE_OUT_BUFS, i).wait()
            w = jnp.concatenate(
                [jnp.concatenate([xbuf[g % MOE_IN_BUFS, i] for i in range(n_in)], axis=1)
                 for g in gs], axis=0)
            gu = _dot(_unpack_bf16_pairs(w).astype(BF16), wupb_ref[...])
            gate = gu[:, :MOE_HIDDEN]
            act = (gate * _sigmoid(gate) * gu[:, MOE_HIDDEN:]).astype(BF16)
            y = _pack_bf16_pairs(_dot(act, wdnb_ref[...]))
            for k, g in enumerate(gs):
                for i in range(n_in):
                    ybuf[g % MOE_OUT_BUFS, i] = y[k * MOE_ROWS:(k + 1) * MOE_ROWS,
                                                  i * SC_ROW_WORDS:(i + 1) * SC_ROW_WORDS]
                    out_copy(g, i).start()

        def full_group(p, carry):
            g = first + MOE_TALL * p
            run_blocks([g + k for k in range(MOE_TALL)])
            return carry

        n_full = count // MOE_TALL
        lax.fori_loop(0, n_full, full_group, 0)
        done = n_full * MOE_TALL
        size = MOE_TALL // 2
        while size >= 1:
            start = first + done + ((count - done) // (2 * size)) * (2 * size)

            @pl.when(((count - done) // size) % 2 == 1)
            def _(start=start, size=size):
                run_blocks([start + k for k in range(size)])

            size //= 2

    @pl.when(e == n_exp - 1)
    def _():
        for k in range(MOE_OUT_BUFS):
            @pl.when(total - 1 - k >= 0)
            def _():
                for i in range(n_in):
                    out_copy(total - 1 - k, i).wait()


def _moe_call(first_block, block_count, xb_pieces, w_up, w_down):
    n_rows = xb_pieces[0].shape[0]
    n_in = len(xb_pieces)
    d = 2 * n_in * SC_ROW_WORDS
    n_exp, _, hid2 = w_up.shape
    hid = w_down.shape[1]
    hbm = pl.BlockSpec(memory_space=pl.ANY)
    buf = lambda n: pltpu.VMEM((n, n_in, MOE_ROWS, SC_ROW_WORDS), U32)
    return pl.pallas_call(
        functools.partial(_moe_kernel, n_in),
        grid_spec=pltpu.PrefetchScalarGridSpec(
            num_scalar_prefetch=2,
            grid=(n_exp,),
            in_specs=[hbm] * n_in + [
                      pl.BlockSpec((None, d, hid2), lambda e, fb, bc: (e, 0, 0)),
                      pl.BlockSpec((None, hid, d), lambda e, fb, bc: (e, 0, 0))],
            out_specs=[hbm] * n_in,
            scratch_shapes=[pltpu.VMEM((d, hid2), BF16), pltpu.VMEM((hid, d), BF16),
                            buf(MOE_IN_BUFS), buf(MOE_OUT_BUFS),
                            pltpu.SemaphoreType.DMA((MOE_IN_BUFS, n_in)),
                            pltpu.SemaphoreType.DMA((MOE_OUT_BUFS, n_in))]),
        out_shape=[jax.ShapeDtypeStruct((n_rows, SC_ROW_WORDS), U32)] * n_in,
        compiler_params=pltpu.CompilerParams(
            dimension_semantics=("arbitrary",), vmem_limit_bytes=MOE_VMEM_LIMIT),
        name="moe",
    )(first_block, block_count, *xb_pieces, w_up, w_down)


def _combine_kernel(n_pieces, x1_ref, routet_ref, ada_ref, g_ref, b_ref, *refs):
    o_ref = refs[-1]
    ya = _unpack_bf16_pairs(jnp.concatenate([r[...] for r in refs[:n_pieces]], axis=1))
    yb = _unpack_bf16_pairs(jnp.concatenate([r[...] for r in refs[n_pieces:2 * n_pieces]], axis=1))
    rt = routet_ref[...]
    r = jnp.concatenate([rt, jnp.zeros((LANES - rt.shape[0], rt.shape[1]), F32)], axis=0).T
    m = ya * r[:, 2:3] + yb * r[:, 3:4]
    g2 = ada_ref[5:6]
    o_ref[...] = _layernorm(ALPHA * x1_ref[...] + (1.0 + g2) * m, g_ref[...], b_ref[...])


def _combine_call(b, out_prev, x1, y_pieces, route_t, ada, ln2g, ln2b):
    bsz, s, d = x1.shape
    ts = min(OUT_ROWS, s)
    nj = s // ts
    n_pieces = len(y_pieces)
    slot = lambda k: pl.BlockSpec((ts, SC_ROW_WORDS), lambda j: (k * nj + j, 0))
    in_specs = ([pl.BlockSpec((None, ts, d), lambda j: (b, j, 0)),
                 pl.BlockSpec((None, SUBLANES, ts), lambda j: (b, 0, j)),
                 pl.BlockSpec((None,) + ada.shape[1:], lambda j: (b, 0, 0)),
                 pl.BlockSpec((1, d), lambda j: (0, 0)),
                 pl.BlockSpec((1, d), lambda j: (0, 0))]
                + [slot(0)] * n_pieces + [slot(1)] * n_pieces)
    args = [x1, route_t, ada, ln2g, ln2b, *y_pieces, *y_pieces]
    aliases = {}
    if out_prev is not None:
        in_specs.append(pl.BlockSpec(memory_space=pl.ANY))
        aliases = {len(args): 0}
        args.append(out_prev)
    n_extra = len(args) - 5 - 2 * n_pieces

    def body(*refs):
        _combine_kernel(n_pieces, *refs[:5 + 2 * n_pieces], *refs[5 + 2 * n_pieces + n_extra:])

    return pl.pallas_call(
        body,
        grid=(nj,),
        in_specs=in_specs,
        out_specs=pl.BlockSpec((None, ts, d), lambda j: (b, j, 0)),
        out_shape=jax.ShapeDtypeStruct((bsz, s, d), F32),
        input_output_aliases=aliases,
        compiler_params=pltpu.CompilerParams(dimension_semantics=("arbitrary",)),
        name="combine",
    )(*args)


def kernel(x, c, w_ada, b_ada, w_in, lb_logits, hg_norm_w, sg_ln_g, sg_ln_b, sg_w, sg_b, w_out, ln1_g, ln1_b, router_group_w, router_group_b, router_expert_w, router_expert_b, w_up, w_down, ln2_g, ln2_b):
    assert w_in.shape[0] == DEPTH
    bsz, s, d = x.shape
    t = bsz * s
    l = 0

    ada = _ada_call(c, w_ada[l], b_ada[l])

    wr = jnp.concatenate(
        [router_group_w[l].T, router_expert_w[l].transpose(0, 2, 1).reshape(MOE_EXPERTS, d)], axis=0)
    wr = jnp.pad(wr, ((0, ROUTE_ROWS - wr.shape[0]), (0, 0)))
    br = jnp.concatenate([router_group_b[l], router_expert_b[l].reshape(MOE_EXPERTS)])
    br = jnp.broadcast_to(jnp.pad(br, (0, ROUTE_ROWS - br.shape[0]))[:, None], (ROUTE_ROWS, LANES))
    wrs = jnp.concatenate(_split_bf16(wr), axis=0)

    x1, route_t, *h2p = _mix_call(
        x, ada, w_in[l], lb_logits, hg_norm_w[l].reshape(1, -1),
        sg_ln_g[l].reshape(1, -1), sg_ln_b[l].reshape(1, -1), sg_w[l], sg_b[l].T,
        w_out[l], ln1_g[l].reshape(1, d), ln1_b[l].reshape(1, d), wrs, br)

    n_blocks = -(-(2 * t) // MOE_ROWS) + MOE_EXPERTS
    n_rows = n_blocks * MOE_ROWS
    dest, blk = _sort_call(route_t)
    d0, d1 = dest[0], dest[1]

    xb = [_sc_scatter_rows2(p.reshape(t, SC_ROW_WORDS), d0, d1, n_rows) for p in h2p]
    yb = _moe_call(blk[:, 0], blk[:, 1], xb, w_up[l], w_down[l])
    out = None
    for b in range(bsz):
        tok = slice(b * s, (b + 1) * s)
        idx = jnp.concatenate([d0[tok], d1[tok]])
        y2 = [_sc_gather_rows(p, idx) for p in yb]
        out = _combine_call(b, out, x1, y2, route_t, ada, ln2_g[l].reshape(1, d), ln2_b[l].reshape(1, d))
    return out
```

```python
import functools

import jax
import jax.numpy as jnp
from jax import lax
from jax.experimental import pallas as pl
from jax.experimental.pallas import tpu as pltpu
from jax.experimental.pallas import tpu_sc as plsc

F32 = jnp.float32
BF16 = jnp.bfloat16
I32 = jnp.int32
U32 = jnp.uint32

HG_HEADS = 4
HG_DK = 128
HG_CHUNK = 64
HG_SUB = 8
SG_GROUPS = 4
SG_CH = 128
SG_CHUNK = 128
SG_PAIR = 4
MOE_GROUPS = 4
MOE_EPG = 8
MOE_EXPERTS = MOE_GROUPS * MOE_EPG
MOE_HIDDEN = 512
DEPTH = 1
ALPHA = (2.0 * DEPTH) ** 0.25
LN_EPS = 1e-5
RMS_EPS = 1e-6
LOG2E = 1.4426950408889634

LANES = 128
SUBLANES = 8
SC_WINDOW = 128
SC_ROW_WORDS = 256

MIX_ROWS = 512
PROJ_PIECE = 256
SORT_ROWS = 4096
SORT_BLOCK = 256
ROUTE_ROWS = -(-(MOE_GROUPS + MOE_EXPERTS) // SUBLANES) * SUBLANES
MOE_ROWS = 256
MOE_TALL = 4
MOE_AHEAD = 4
MOE_IN_BUFS = MOE_AHEAD + MOE_TALL
MOE_OUT_BUFS = 2 * MOE_TALL
OUT_ROWS = 512
MOE_VMEM_LIMIT = 52 * 1024 * 1024
MIX_VMEM_LIMIT = 60 * 1024 * 1024


def _dot(a, b):
    return jnp.dot(a, b, preferred_element_type=F32)


def _dot_nt(a, b):
    return lax.dot_general(a, b, (((1,), (1,)), ((), ())), preferred_element_type=F32)


def _dot_tn(a, b):
    return lax.dot_general(a, b, (((0,), (0,)), ((), ())), preferred_element_type=F32)


def _sigmoid(x):
    return jax.nn.sigmoid(x)


def _gelu_exact(x):
    return 0.5 * x * (1.0 + lax.erf(x * (2.0 ** -0.5)))


def _layernorm(x, g, b):
    mu = jnp.mean(x, axis=-1, keepdims=True)
    xc = x - mu
    var = jnp.mean(xc * xc, axis=-1, keepdims=True)
    return xc * lax.rsqrt(var + LN_EPS) * g + b


def _pack_bf16_pairs(x):
    n = x.shape[1]
    bits = lax.bitcast_convert_type(x.astype(BF16).astype(F32), U32)
    return bits[:, n // 2:] | (bits[:, :n // 2] >> 16)


def _unpack_bf16_pairs(w):
    lo = lax.bitcast_convert_type(w << 16, F32)
    hi = lax.bitcast_convert_type(w & jnp.uint32(0xFFFF0000), F32)
    return jnp.concatenate([lo, hi], axis=1)


def _split_bf16(x):
    hi = x.astype(BF16)
    lo = (x - hi.astype(F32)).astype(BF16)
    return hi, lo


def _ada_kernel(ct_ref, w_ref, b_ref, o_ref):
    ct = ct_ref[...]
    ca = ct * _sigmoid(ct)
    w = w_ref[...]
    rows = [jnp.sum(ca[:, b:b + 1] * w, axis=0, keepdims=True) for b in range(ct.shape[1])]
    pad = o_ref.shape[0] - len(rows)
    if pad:
        rows.append(jnp.zeros((pad, w.shape[1]), F32))
    o_ref[...] = jnp.concatenate(rows, axis=0) + b_ref[...]


def _ada_call(c, w_ada, b_ada):
    bsz, d = c.shape
    n = w_ada.shape[1]
    rows = -(-bsz // SUBLANES) * SUBLANES
    out = pl.pallas_call(
        _ada_kernel,
        grid=(n // d,),
        in_specs=[pl.BlockSpec((d, bsz), lambda i: (0, 0)),
                  pl.BlockSpec((d, d), lambda i: (0, i)),
                  pl.BlockSpec((1, d), lambda i: (0, i))],
        out_specs=pl.BlockSpec((rows, d), lambda i: (0, i)),
        out_shape=jax.ShapeDtypeStruct((rows, n), F32),
        name="ada",
    )(c.T, w_ada, b_ada.reshape(1, n))
    return out[:bsz].reshape(bsz, n // d, d)


def _hgrn2_matmul_phase(c, proj_ref, st_ref, g_ref, h_ref, lb, tril):
    hw = HG_HEADS * HG_DK
    cl = HG_CHUNK
    rows = pl.ds(c * cl, cl)
    qz = proj_ref[rows, 0:hw]
    fz = proj_ref[rows, hw:2 * hw]
    v = proj_ref[rows, 2 * hw:3 * hw]

    q = qz * _sigmoid(qz)
    f = lb + (1.0 - lb) * _sigmoid(fz)
    lf = jnp.log(f)
    k = 1.0 - f
    lf_hi, lf_lo = _split_bf16(lf)
    g = (_dot(tril, lf_hi) + _dot(tril, lf_lo)) * LOG2E
    glast = g[cl - 1:cl, :]
    qg = (q * jnp.exp2(g)).astype(BF16)
    kd = (k * jnp.exp2(glast - g)).astype(BF16)
    vb = v.astype(BF16)
    g_ref[...] = g
    h_ref[...] = g - jnp.log(k) * LOG2E

    n_sub = cl // HG_SUB
    o_state = []
    acc_all = []
    for hd in range(HG_HEADS):
        cs = slice(hd * HG_DK, (hd + 1) * HG_DK)
        st = st_ref[hd]
        o_state.append(_dot_nt(qg[:, cs], st.astype(BF16)))
        st_ref[hd] = st * jnp.exp2(glast[:, cs]) + _dot_tn(vb[:, cs], kd[:, cs])
        gh = g[:, cs]
        kh = k[:, cs]
        qh = q[:, cs]
        acc = [None] * n_sub
        w = cl // 2
        while w >= HG_SUB:
            for p in range(0, cl, 2 * w):
                ref = g_ref[p + w - 1:p + w, cs]
                qt = (qh[p + w:p + 2 * w] * jnp.exp2(gh[p + w:p + 2 * w] - ref)).astype(BF16)
                kt = (kh[p:p + w] * jnp.exp2(ref - gh[p:p + w])).astype(BF16)
                pieces = [kt]
                if p:
                    pieces.insert(0, jnp.zeros((p, HG_DK), BF16))
                if cl - p - w:
                    pieces.append(jnp.zeros((cl - p - w, HG_DK), BF16))
                blk = _dot_nt(qt, jnp.concatenate(pieces, axis=0))
                for r in range(w // HG_SUB):
                    i = (p + w) // HG_SUB + r
                    part = blk[r * HG_SUB:(r + 1) * HG_SUB]
                    acc[i] = part if acc[i] is None else acc[i] + part
            w //= 2
        acc_all.append(acc)
    return q, g, vb, o_state, acc_all


def _hgrn2_diag_phase(c, phase1, proj_ref, y_ref, h_ref, gnw):
    q, g, vb, o_state, acc_all = phase1
    hw = HG_HEADS * HG_DK
    cl = HG_CHUNK
    rows = pl.ds(c * cl, cl)
    og = proj_ref[rows, 3 * hw:4 * hw]
    n_sub = cl // HG_SUB
    lane_c = lax.broadcasted_iota(I32, (HG_SUB, cl), 1)
    trow = lax.broadcasted_iota(I32, (HG_SUB, cl), 0)
    for hd in range(HG_HEADS):
        cs = slice(hd * HG_DK, (hd + 1) * HG_DK)
        gh = g[:, cs]
        qh = q[:, cs]
        a_rows = []
        for i in range(n_sub):
            b0 = i * HG_SUB
            gb = gh[b0:b0 + HG_SUB]
            qb = qh[b0:b0 + HG_SUB]
            a = jnp.zeros((HG_SUB, cl), F32) if acc_all[hd][i] is None else acc_all[hd][i]
            for s in range(HG_SUB):
                hs = h_ref[b0 + s:b0 + s + 1, cs]
                col = jnp.sum(qb * jnp.exp2(gb - hs), axis=-1, keepdims=True)
                a = jnp.where(lane_c == b0 + s, col, a)
            a_rows.append(jnp.where(lane_c <= b0 + trow, a, 0.0))
        amat = jnp.concatenate(a_rows, axis=0).astype(BF16)
        o = o_state[hd] + _dot(amat, vb[:, cs])
        ms = jnp.mean(o * o, axis=-1, keepdims=True)
        ogh = og[:, cs]
        ya = o * lax.rsqrt(ms + RMS_EPS) * gnw * (ogh * _sigmoid(ogh))
        y_ref[rows, cs] = ya.astype(BF16)


def _route_t(logits):
    nr, ts = logits.shape
    row = lax.broadcasted_iota(I32, (nr, ts), 0)
    neg = -jnp.inf
    gl = jnp.where(row < MOE_GROUPS, logits, neg)
    gmax = jnp.max(gl, axis=0, keepdims=True)
    gidx = jnp.min(jnp.where(gl == gmax, row, nr), axis=0, keepdims=True)
    p_group = 1.0 / jnp.sum(jnp.exp(gl - gmax), axis=0, keepdims=True)
    e_lo = MOE_GROUPS + gidx * MOE_EPG
    el = jnp.where((row >= e_lo) & (row < e_lo + MOE_EPG), logits, neg)
    m1 = jnp.max(el, axis=0, keepdims=True)
    i1 = jnp.min(jnp.where(el == m1, row, nr), axis=0, keepdims=True)
    el2 = jnp.where(row == i1, neg, el)
    m2 = jnp.max(el2, axis=0, keepdims=True)
    i2 = jnp.min(jnp.where(el2 == m2, row, nr), axis=0, keepdims=True)
    esum = jnp.sum(jnp.exp(el - m1), axis=0, keepdims=True)
    p1 = 1.0 / esum
    p2 = jnp.exp(m2 - m1) / esum
    den = p1 + p2
    w1 = p_group * p1 / den
    w2 = p_group * p2 / den
    e1 = (i1 - MOE_GROUPS).astype(F32)
    e2 = (i2 - MOE_GROUPS).astype(F32)
    out_row = lax.broadcasted_iota(I32, (SUBLANES, ts), 0)
    return jnp.where(out_row == 0, e1, jnp.where(out_row == 1, e2,
                     jnp.where(out_row == 2, w1, jnp.where(out_row == 3, w2, 0.0))))


def _mix_kernel(nt, x_ref, xn_ref, ada_ref, adan_ref, win32_ref, lbl_ref, gnw_ref, sglg_ref,
                sglb_ref, sgw_ref, sgbt_ref, wout32_ref, ln1g_ref, ln1b_ref, wrs_ref, br_ref,
                x1_ref, routet_ref, *rest):
    h2p_refs = rest[:-8]
    st_ref, proj_ref, projn_ref, y_ref, g_ref, h_ref, win_ref, wout_ref = rest[-8:]
    ts, d = x_ref.shape
    hw = HG_HEADS * HG_DK
    sgw = SG_GROUPS * SG_CH
    ncol = proj_ref.shape[1]
    step = pl.program_id(0)

    def modulated(xr, adar):
        a = adar[...]
        return (xr[...] * (1.0 + a[1:2]) + a[0:1]).astype(BF16)

    @pl.when(step % nt == 0)
    def _():
        st_ref[...] = jnp.zeros_like(st_ref)

    @pl.when(step == 0)
    def _():
        win_ref[...] = win32_ref[...].astype(BF16)
        wout_ref[...] = wout32_ref[...].astype(BF16)
        proj_ref[...] = _dot(modulated(x_ref, ada_ref), win_ref[...])

    x = x_ref[...]
    ada = ada_ref[...]
    g1, sh2, sc2 = ada[2:3], ada[3:4], ada[4:5]
    hn = modulated(xn_ref, adan_ref)

    lbl = lbl_ref[...]
    slots = [lbl[i:i + 1] for i in range(lbl.shape[0])]
    mx = functools.reduce(jnp.maximum, slots)
    ex = [jnp.exp(s - mx) for s in slots]
    lb = ex[0] / functools.reduce(lambda a, b: a + b, ex)

    ci = lax.broadcasted_iota(I32, (HG_CHUNK, HG_CHUNK), 0)
    cj = lax.broadcasted_iota(I32, (HG_CHUNK, HG_CHUNK), 1)
    tril = (ci >= cj).astype(BF16)
    gnw = gnw_ref[...]

    n_chunks = ts // HG_CHUNK
    n_sg = ts // SG_CHUNK
    n_pp = ncol // PROJ_PIECE
    n_tail = max(n_pp - n_chunks, 0)
    slot_of = [p if p < n_pp - n_tail else n_chunks + (2 * (p - (n_pp - n_tail))) // max(n_tail, 1)
               for p in range(n_pp)]

    def next_proj_pieces(slot):
        for p in range(n_pp):
            if min(slot_of[p], n_chunks + 1) == slot:
                cols = slice(p * PROJ_PIECE, (p + 1) * PROJ_PIECE)
                projn_ref[:, cols] = _dot(hn, win_ref[:, cols])

    pi = lax.broadcasted_iota(I32, (SG_CHUNK, SG_CHUNK), 0)
    pj = lax.broadcasted_iota(I32, (SG_CHUNK, SG_CHUNK), 1)
    sglg = sglg_ref[...]
    sglb = sglb_ref[...]

    def spatial_gating(ps):
        us, vns = [], []
        for p in ps:
            rows = slice(p * SG_CHUNK, (p + 1) * SG_CHUNK)
            z = _gelu_exact(proj_ref[rows, 4 * hw:4 * hw + 2 * sgw])
            us.append(z[:, :sgw])
            vns.append(_layernorm(z[:, sgw:], sglg, sglb).astype(BF16))
        for gi in range(SG_GROUPS):
            cs = slice(gi * SG_CH, (gi + 1) * SG_CH)
            wc = jnp.where(pi >= pj, sgw_ref[gi], 0.0).astype(BF16)
            mixed = _dot(wc, jnp.concatenate([vn[:, cs] for vn in vns], axis=1))
            for k, p in enumerate(ps):
                rows = slice(p * SG_CHUNK, (p + 1) * SG_CHUNK)
                mk = mixed[:, k * SG_CH:(k + 1) * SG_CH] + sgbt_ref[:, gi:gi + 1]
                y_ref[rows, hw + gi * SG_CH:hw + (gi + 1) * SG_CH] = (us[k][:, cs] * mk).astype(BF16)

    phase1 = _hgrn2_matmul_phase(0, proj_ref, st_ref, g_ref.at[0], h_ref.at[0], lb, tril)
    for c in range(n_chunks):
        next_proj_pieces(c)
        if c + 1 < n_chunks:
            nxt = (c + 1) % 2
            phase1_next = _hgrn2_matmul_phase(c + 1, proj_ref, st_ref, g_ref.at[nxt], h_ref.at[nxt],
                                              lb, tril)
        _hgrn2_diag_phase(c, phase1, proj_ref, y_ref, h_ref.at[c % 2], gnw)
        phase1 = phase1_next
    for p in range(0, n_sg, SG_PAIR):
        spatial_gating(list(range(p, min(p + SG_PAIR, n_sg))))

    y = _dot(y_ref[...], wout_ref[...])
    next_proj_pieces(n_chunks)
    x1 = _layernorm(ALPHA * x + (1.0 + g1) * y, ln1g_ref[...], ln1b_ref[...])
    x1_ref[...] = x1
    h2 = x1 * (1.0 + sc2) + sh2
    h2_hi, h2_lo = _split_bf16(h2)
    packed = _pack_bf16_pairs(h2)
    for i, ref in enumerate(h2p_refs):
        ref[...] = packed[:, i * SC_ROW_WORDS:(i + 1) * SC_ROW_WORDS]
    nr = br_ref.shape[0]
    part = _dot_nt(wrs_ref[...], h2_hi)
    logits = part[:nr] + part[nr:] + _dot_nt(wrs_ref[:nr, :], h2_lo) + br_ref[:, 0:1]
    next_proj_pieces(n_chunks + 1)
    routet_ref[...] = _route_t(logits)
    proj_ref[...] = projn_ref[...]


def _mix_call(x, ada, w_in, lb_logits, gnw, sglg, sglb, sg_w, sg_bt, w_out, ln1g, ln1b, wrs, br):
    bsz, s, d = x.shape
    ts = min(MIX_ROWS, s)
    nt = s // ts
    n_steps = bsz * nt
    ncol = w_in.shape[1]
    hw = HG_HEADS * HG_DK
    n_pieces = d // 2 // SC_ROW_WORDS
    const = lambda a: pl.BlockSpec(a.shape, lambda i: (0,) * a.ndim, pipeline_mode=pl.Buffered(1))
    cur = lambda i: (i // nt, i % nt, 0)
    nxt = lambda i: cur(jnp.minimum(i + 1, n_steps - 1))
    return pl.pallas_call(
        functools.partial(_mix_kernel, nt),
        grid=(n_steps,),
        in_specs=[pl.BlockSpec((None, ts, d), cur),
                  pl.BlockSpec((None, ts, d), nxt),
                  pl.BlockSpec((None,) + ada.shape[1:], lambda i: (cur(i)[0], 0, 0)),
                  pl.BlockSpec((None,) + ada.shape[1:], lambda i: (nxt(i)[0], 0, 0)),
                  const(w_in), const(lb_logits), const(gnw), const(sglg), const(sglb), const(sg_w),
                  const(sg_bt), const(w_out), const(ln1g), const(ln1b), const(wrs), const(br)],
        out_specs=[pl.BlockSpec((None, ts, d), cur),
                   pl.BlockSpec((None, SUBLANES, ts), lambda i: (i // nt, 0, i % nt))]
                  + [pl.BlockSpec((None, ts, SC_ROW_WORDS), cur)] * n_pieces,
        out_shape=[jax.ShapeDtypeStruct((bsz, s, d), F32),
                   jax.ShapeDtypeStruct((bsz, SUBLANES, s), F32)]
                  + [jax.ShapeDtypeStruct((bsz, s, SC_ROW_WORDS), U32)] * n_pieces,
        scratch_shapes=[pltpu.VMEM((HG_HEADS, HG_DK, HG_DK), F32),
                        pltpu.VMEM((ts, ncol), F32),
                        pltpu.VMEM((ts, ncol), F32),
                        pltpu.VMEM((ts, d), BF16),
                        pltpu.VMEM((2, HG_CHUNK, hw), F32),
                        pltpu.VMEM((2, HG_CHUNK, hw), F32),
                        pltpu.VMEM(w_in.shape, BF16),
                        pltpu.VMEM(w_out.shape, BF16)],
        compiler_params=pltpu.CompilerParams(
            dimension_semantics=("arbitrary",), vmem_limit_bytes=MIX_VMEM_LIMIT),
        name="mix",
    )(x, x, ada, ada, w_in, lb_logits, gnw, sglg, sglb, sg_w, sg_bt, w_out, ln1g, ln1b, wrs, br)


def _sort_kernel(nb, rt_ref, dest_ref, blk_ref, cnt_ref, pst_ref, carry_ref):
    ph = pl.program_id(0)
    first_step = (pl.program_id(1) == 0) & (pl.program_id(2) == 0)
    tk = rt_ref.shape[1]
    r = rt_ref[...]
    row = lax.broadcasted_iota(I32, (MOE_EXPERTS, tk), 0)
    oh1 = row == r[0:1].astype(I32)
    oh2 = row == r[1:2].astype(I32)
    hot = jnp.where(oh1 | oh2, 1.0, 0.0)

    @pl.when((ph == 0) & first_step)
    def _():
        cnt_ref[...] = jnp.zeros_like(cnt_ref)

    @pl.when(ph == 0)
    def _():
        cnt_ref[...] += jnp.sum(hot, axis=1, keepdims=True)

    @pl.when((ph == 1) & first_step)
    def _():
        nblk = jnp.floor((cnt_ref[...] + (MOE_ROWS - 1)) * (1.0 / MOE_ROWS))
        ei = lax.broadcasted_iota(I32, (MOE_EXPERTS, MOE_EXPERTS), 0)
        ej = lax.broadcasted_iota(I32, (MOE_EXPERTS, MOE_EXPERTS), 1)
        pstart = _dot((ej < ei).astype(BF16), nblk.astype(BF16))
        pst_ref[...] = pstart * float(MOE_ROWS)
        carry_ref[...] = jnp.zeros_like(carry_ref)
        lane_b = lax.broadcasted_iota(I32, blk_ref.shape, 1)
        blk_ref[...] = jnp.where(lane_b == 0, pstart, jnp.where(lane_b == 1, nblk, 0.0)).astype(I32)

    @pl.when(ph == 1)
    def _():
        ti = lax.broadcasted_iota(I32, (nb, nb), 0)
        tj = lax.broadcasted_iota(I32, (nb, nb), 1)
        before = (ti < tj).astype(BF16)
        carry = carry_ref[...] + pst_ref[...]
        rows1 = []
        rows2 = []
        for j in range(tk // nb):
            cs = slice(j * nb, (j + 1) * nb)
            hb = hot[:, cs]
            base = _dot(hb.astype(BF16), before) + carry[:, 0:1]
            rows1.append(jnp.sum(jnp.where(oh1[:, cs], base, 0.0), axis=0, keepdims=True))
            rows2.append(jnp.sum(jnp.where(oh2[:, cs], base, 0.0), axis=0, keepdims=True))
            carry = carry + jnp.sum(hb, axis=1, keepdims=True)
        d1 = jnp.concatenate(rows1, axis=1)
        d2 = jnp.concatenate(rows2, axis=1)
        out_row = lax.broadcasted_iota(I32, (SUBLANES, tk), 0)
        dest_ref[...] = jnp.where(out_row == 0, d1, jnp.where(out_row == 1, d2, 0.0)).astype(I32)
        carry_ref[...] = carry - pst_ref[...]


def _sort_call(route_t):
    bsz, _, s = route_t.shape
    tk = min(SORT_ROWS, s)
    nj = s // tk
    return pl.pallas_call(
        functools.partial(_sort_kernel, min(SORT_BLOCK, tk)),
        grid=(2, bsz, nj),
        in_specs=[pl.BlockSpec((None, SUBLANES, tk), lambda p, b, j: (b, 0, j))],
        out_specs=[pl.BlockSpec((SUBLANES, tk), lambda p, b, j: (0, (b * nj + j) * p)),
                   pl.BlockSpec((MOE_EXPERTS, LANES), lambda p, b, j: (0, 0))],
        out_shape=[jax.ShapeDtypeStruct((SUBLANES, bsz * s), I32),
                   jax.ShapeDtypeStruct((MOE_EXPERTS, LANES), I32)],
        scratch_shapes=[pltpu.VMEM((MOE_EXPERTS, LANES), F32), pltpu.VMEM((MOE_EXPERTS, LANES), F32),
                        pltpu.VMEM((MOE_EXPERTS, LANES), F32)],
        compiler_params=pltpu.CompilerParams(
            dimension_semantics=("arbitrary", "arbitrary", "arbitrary")),
        name="sort",
    )(route_t)


def _sc_mesh():
    return plsc.VectorSubcoreMesh(core_axis_name="c", subcore_axis_name="s")


def _sc_scatter_rows2(rows, idx_a, idx_b, n_out):
    n, w = rows.shape

    @pl.kernel(out_type=jax.ShapeDtypeStruct((n_out, w), rows.dtype), mesh=_sc_mesh(), scratch_types=[])
    def scatter(x_hbm, ia_hbm, ib_hbm, o_hbm):
        def body(x_vmem, ia_vmem, ib_vmem):
            pltpu.sync_copy(x_vmem, o_hbm.at[ia_vmem.at[0]])
            pltpu.sync_copy(x_vmem, o_hbm.at[ib_vmem.at[0]])

        pltpu.emit_pipeline(
            body,
            grid=(n // SC_WINDOW,),
            in_specs=[pl.BlockSpec((SC_WINDOW, w), lambda i: (i, 0)),
                      pl.BlockSpec((1, SC_WINDOW), lambda i: (0, i)),
                      pl.BlockSpec((1, SC_WINDOW), lambda i: (0, i))],
            out_specs=[],
            core_axis_name=("c", "s"),
            dimension_semantics=(pltpu.PARALLEL,),
        )(x_hbm, ia_hbm, ib_hbm)

    return scatter(rows, idx_a.reshape(1, n), idx_b.reshape(1, n))


def _sc_gather_rows(src, idx):
    n = idx.shape[0]
    w = src.shape[1]

    @pl.kernel(out_type=jax.ShapeDtypeStruct((n, w), src.dtype), mesh=_sc_mesh(), scratch_types=[])
    def gather(x_hbm, i_hbm, o_hbm):
        def body(i_vmem, o_vmem):
            pltpu.sync_copy(x_hbm.at[i_vmem.at[0]], o_vmem)

        pltpu.emit_pipeline(
            body,
            grid=(n // SC_WINDOW,),
            in_specs=[pl.BlockSpec((1, SC_WINDOW), lambda i: (0, i))],
            out_specs=[pl.BlockSpec((SC_WINDOW, w), lambda i: (i, 0))],
            core_axis_name=("c", "s"),
            dimension_semantics=(pltpu.PARALLEL,),
        )(i_hbm, o_hbm)

    return gather(src, idx.reshape(1, n))


def _moe_kernel(n_in, first_ref, count_ref, *refs):
    xb_hbm = refs[:n_in]
    wup_ref, wdn_ref = refs[n_in:n_in + 2]
    y_hbm = refs[n_in + 2:2 * n_in + 2]
    wupb_ref, wdnb_ref, xbuf, ybuf, in_sem, out_sem = refs[2 * n_in + 2:]
    e = pl.program_id(0)
    n_exp = pl.num_programs(0)
    first = first_ref[e]
    count = count_ref[e]
    total = first_ref[n_exp - 1] + count_ref[n_exp - 1]

    def rows_of(g):
        return pl.ds(pl.multiple_of(g * MOE_ROWS, MOE_ROWS), MOE_ROWS)

    def in_copy(g, i):
        slot = g % MOE_IN_BUFS
        return pltpu.make_async_copy(xb_hbm[i].at[rows_of(g)], xbuf.at[slot, i], in_sem.at[slot, i])

    def out_copy(g, i):
        slot = g % MOE_OUT_BUFS
        return pltpu.make_async_copy(ybuf.at[slot, i], y_hbm[i].at[rows_of(g)], out_sem.at[slot, i])

    @pl.when(e == 0)
    def _():
        for g in range(MOE_AHEAD):
            @pl.when(g < total)
            def _():
                for i in range(n_in):
                    in_copy(g, i).start()

    @pl.when(count > 0)
    def _():
        wupb_ref[...] = wup_ref[...].astype(BF16)
        wdnb_ref[...] = wdn_ref[...].astype(BF16)

        def run_blocks(gs):
            for g in gs:
                for i in range(n_in):
                    in_copy(g, i).wait()
            for g in gs:
                @pl.when(g + MOE_AHEAD < total)
                def _():
                    for i in range(n_in):
                        in_copy(g + MOE_AHEAD, i).start()
            for g in gs:
                @pl.when(g >= MOE_OUT_BUFS)
                def _():
                    for i in range(n_in):
                        out_copy(g - MOE_OUT_BUFS, i).wait()
            w = jnp.concatenate(
                [jnp.concatenate([xbuf[g % MOE_IN_BUFS, i] for i in range(n_in)], axis=1)
                 for g in gs], axis=0)
            gu = _dot(_unpack_bf16_pairs(w).astype(BF16), wupb_ref[...])
            gate = gu[:, :MOE_HIDDEN]
            act = (gate * _sigmoid(gate) * gu[:, MOE_HIDDEN:]).astype(BF16)
            y = _pack_bf16_pairs(_dot(act, wdnb_ref[...]))
            for k, g in enumerate(gs):
                for i in range(n_in):
                    ybuf[g % MOE_OUT_BUFS, i] = y[k * MOE_ROWS:(k + 1) * MOE_ROWS,
                                                  i * SC_ROW_WORDS:(i + 1) * SC_ROW_WORDS]
                    out_copy(g, i).start()

        def full_group(p, carry):
            g = first + MOE_TALL * p
            run_blocks([g + k for k in range(MOE_TALL)])
            return carry

        n_full = count // MOE_TALL
        lax.fori_loop(0, n_full, full_group, 0)
        done = n_full * MOE_TALL
        size = MOE_TALL // 2
        while size >= 1:
            start = first + done + ((count - done) // (2 * size)) * (2 * size)

            @pl.when(((count - done) // size) % 2 == 1)
            def _(start=start, size=size):
                run_blocks([start + k for k in range(size)])

            size //= 2

    @pl.when(e == n_exp - 1)
    def _():
        for k in range(MOE_OUT_BUFS):
            @pl.when(total - 1 - k >= 0)
            def _():
                for i in range(n_in):
                    out_copy(total - 1 - k, i).wait()


def _moe_call(first_block, block_count, xb_pieces, w_up, w_down):
    n_rows = xb_pieces[0].shape[0]
    n_in = len(xb_pieces)
    d = 2 * n_in * SC_ROW_WORDS
    n_exp, _, hid2 = w_up.shape
    hid = w_down.shape[1]
    hbm = pl.BlockSpec(memory_space=pl.ANY)
    buf = lambda n: pltpu.VMEM((n, n_in, MOE_ROWS, SC_ROW_WORDS), U32)
    return pl.pallas_call(
        functools.partial(_moe_kernel, n_in),
        grid_spec=pltpu.PrefetchScalarGridSpec(
            num_scalar_prefetch=2,
            grid=(n_exp,),
            in_specs=[hbm] * n_in + [
                      pl.BlockSpec((None, d, hid2), lambda e, fb, bc: (e, 0, 0)),
                      pl.BlockSpec((None, hid, d), lambda e, fb, bc: (e, 0, 0))],
            out_specs=[hbm] * n_in,
            scratch_shapes=[pltpu.VMEM((d, hid2), BF16), pltpu.VMEM((hid, d), BF16),
                            buf(MOE_IN_BUFS), buf(MOE_OUT_BUFS),
                            pltpu.SemaphoreType.DMA((MOE_IN_BUFS, n_in)),
                            pltpu.SemaphoreType.DMA((MOE_OUT_BUFS, n_in))]),
        out_shape=[jax.ShapeDtypeStruct((n_rows, SC_ROW_WORDS), U32)] * n_in,
        compiler_params=pltpu.CompilerParams(
            dimension_semantics=("arbitrary",), vmem_limit_bytes=MOE_VMEM_LIMIT),
        name="moe",
    )(first_block, block_count, *xb_pieces, w_up, w_down)


def _combine_kernel(n_pieces, x1_ref, routet_ref, ada_ref, g_ref, b_ref, *refs):
    o_ref = refs[-1]
    ya = _unpack_bf16_pairs(jnp.concatenate([r[...] for r in refs[:n_pieces]], axis=1))
    yb = _unpack_bf16_pairs(jnp.concatenate([r[...] for r in refs[n_pieces:2 * n_pieces]], axis=1))
    rt = routet_ref[...]
    r = jnp.concatenate([rt, jnp.zeros((LANES - rt.shape[0], rt.shape[1]), F32)], axis=0).T
    m = ya * r[:, 2:3] + yb * r[:, 3:4]
    g2 = ada_ref[5:6]
    o_ref[...] = _layernorm(ALPHA * x1_ref[...] + (1.0 + g2) * m, g_ref[...], b_ref[...])


def _combine_call(b, out_prev, x1, y_pieces, route_t, ada, ln2g, ln2b):
    bsz, s, d = x1.shape
    ts = min(OUT_ROWS, s)
    nj = s // ts
    n_pieces = len(y_pieces)
    slot = lambda k: pl.BlockSpec((ts, SC_ROW_WORDS), lambda j: (k * nj + j, 0))
    in_specs = ([pl.BlockSpec((None, ts, d), lambda j: (b, j, 0)),
                 pl.BlockSpec((None, SUBLANES, ts), lambda j: (b, 0, j)),
                 pl.BlockSpec((None,) + ada.shape[1:], lambda j: (b, 0, 0)),
                 pl.BlockSpec((1, d), lambda j: (0, 0)),
                 pl.BlockSpec((1, d), lambda j: (0, 0))]
                + [slot(0)] * n_pieces + [slot(1)] * n_pieces)
    args = [x1, route_t, ada, ln2g, ln2b, *y_pieces, *y_pieces]
    aliases = {}
    if out_prev is not None:
        in_specs.append(pl.BlockSpec(memory_space=pl.ANY))
        aliases = {len(args): 0}
        args.append(out_prev)
    n_extra = len(args) - 5 - 2 * n_pieces

    def body(*refs):
        _combine_kernel(n_pieces, *refs[:5 + 2 * n_pieces], *refs[5 + 2 * n_pieces + n_extra:])

    return pl.pallas_call(
        body,
        grid=(nj,),
        in_specs=in_specs,
        out_specs=pl.BlockSpec((None, ts, d), lambda j: (b, j, 0)),
        out_shape=jax.ShapeDtypeStruct((bsz, s, d), F32),
        input_output_aliases=aliases,
        compiler_params=pltpu.CompilerParams(dimension_semantics=("arbitrary",)),
        name="combine",
    )(*args)


def kernel(x, c, w_ada, b_ada, w_in, lb_logits, hg_norm_w, sg_ln_g, sg_ln_b, sg_w, sg_b, w_out, ln1_g, ln1_b, router_group_w, router_group_b, router_expert_w, router_expert_b, w_up, w_down, ln2_g, ln2_b):
    assert w_in.shape[0] == DEPTH
    bsz, s, d = x.shape
    t = bsz * s
    l = 0

    ada = _ada_call(c, w_ada[l], b_ada[l])

    wr = jnp.concatenate(
        [router_group_w[l].T, router_expert_w[l].transpose(0, 2, 1).reshape(MOE_EXPERTS, d)], axis=0)
    wr = jnp.pad(wr, ((0, ROUTE_ROWS - wr.shape[0]), (0, 0)))
    br = jnp.concatenate([router_group_b[l], router_expert_b[l].reshape(MOE_EXPERTS)])
    br = jnp.broadcast_to(jnp.pad(br, (0, ROUTE_ROWS - br.shape[0]))[:, None], (ROUTE_ROWS, LANES))
    wrs = jnp.concatenate(_split_bf16(wr), axis=0)

    x1, route_t, *h2p = _mix_call(
        x, ada, w_in[l], lb_logits, hg_norm_w[l].reshape(1, -1),
        sg_ln_g[l].reshape(1, -1), sg_ln_b[l].reshape(1, -1), sg_w[l], sg_b[l].T,
        w_out[l], ln1_g[l].reshape(1, d), ln1_b[l].reshape(1, d), wrs, br)

    n_blocks = -(-(2 * t) // MOE_ROWS) + MOE_EXPERTS
    n_rows = n_blocks * MOE_ROWS
    dest, blk = _sort_call(route_t)
    d0, d1 = dest[0], dest[1]

    xb = [_sc_scatter_rows2(p.reshape(t, SC_ROW_WORDS), d0, d1, n_rows) for p in h2p]
    yb = _moe_call(blk[:, 0], blk[:, 1], xb, w_up[l], w_down[l])
    out = None
    for b in range(bsz):
        tok = slice(b * s, (b + 1) * s)
        idx = jnp.concatenate([d0[tok], d1[tok]])
        y2 = [_sc_gather_rows(p, idx) for p in yb]
        out = _combine_call(b, out, x1, y2, route_t, ada, ln2_g[l].reshape(1, d), ln2_b[l].reshape(1, d))
    return out
```
